```python
import math
import jax, jax.numpy as jnp
from jax import lax
import numpy as np

D_MODEL = 1024
BATCH = 2
SEQ = 8192
DEPTH = 4
DEC_BATCH = 32
DEC_SEQ = 4
PAST_LEN = 8192
PAGE_SIZE = 128

N_HEADS = 16
HEAD_DIM = 64
N_KV = 4
Q_PER_KV = N_HEADS // N_KV
CMP_BLK = 32
CMP_STRIDE = 16
CMP_HIDDEN = 2 * HEAD_DIM
SEL_BLK = 64
TOPK = 16
WINDOW = 512
Q_BLOCK = 128
ROPE_THETA = 10000.0
N_KV_COMP = 4
NSA_IN = N_HEADS * HEAD_DIM + 6 * N_KV * HEAD_DIM + 3 * N_HEADS
D_INNER = 2 * D_MODEL
SSM_HEAD_DIM = 64
SSM_HEADS = D_INNER // SSM_HEAD_DIM
SSM_GROUPS = 4
D_STATE = 128
CONV_W = 4
CONV_DIM = D_INNER + 2 * SSM_GROUPS * D_STATE
SSM_CHUNK = 128
SSM_IN = D_INNER + CONV_DIM + SSM_HEADS
D_FF = 256 * ((8 * D_MODEL // 3 + 255) // 256)
N_MIXERS = 2
N_NSA = (DEPTH + N_MIXERS - 1) // N_MIXERS
N_SSM = DEPTH // N_MIXERS
ALPHA = (2 * DEPTH) ** 0.25
BETA = (8 * DEPTH) ** -0.25
N_ADA = 9
LN_EPS = 1e-5

kernel_name = "nsa_ssd_macaron_deepnorm_adaln_step"

f32 = jnp.float32


def layer_norm(x, g, b):
    xf = x.astype(f32)
    mu = xf.mean(-1, keepdims=True)
    var = jnp.square(xf - mu).mean(-1, keepdims=True)
    return ((xf - mu) * lax.rsqrt(var + LN_EPS) * g + b).astype(x.dtype)


def rope(x, pos):
    half = HEAD_DIM // 2
    inv = ROPE_THETA ** (-jnp.arange(half, dtype=f32) / half)
    ang = pos.astype(f32)[:, None] * inv
    cos, sin = jnp.cos(ang)[:, None, :], jnp.sin(ang)[:, None, :]
    x1, x2 = x[..., :half].astype(f32), x[..., half:].astype(f32)
    return jnp.concatenate([x1 * cos - x2 * sin, x2 * cos + x1 * sin], -1).astype(x.dtype)


def masked_softmax(s, mask):
    s = jnp.where(mask, s, -jnp.inf)
    m = jnp.max(s, axis=-1, keepdims=True)
    e = jnp.exp(s - jnp.where(jnp.isfinite(m), m, 0.0))
    return e / jnp.maximum(jnp.sum(e, axis=-1, keepdims=True), 1e-30)


def swiglu(h, w1, w3, w2):
    return (jax.nn.silu(h @ w1) * (h @ w3)) @ w2


def compress(kv, pe, w1, b1, w2, b2):
    b_, l_, g_, d_ = kv.shape
    n_sub = CMP_BLK // CMP_STRIDE
    n_c = (l_ - CMP_BLK) // CMP_STRIDE + 1
    n_ch = n_c + n_sub - 1
    ch = kv[:, :n_ch * CMP_STRIDE].reshape(b_, n_ch, CMP_STRIDE, g_, d_)
    ch = jnp.moveaxis(ch, 3, 2).reshape(b_, n_ch, g_, CMP_STRIDE * d_)
    w1s = w1.reshape(n_sub, CMP_STRIDE * d_, CMP_HIDDEN)
    proj = jnp.einsum('bngf,jfh->jbngh', ch, w1s)
    pre = sum(proj[j][:, j:j + n_c] for j in range(n_sub)) + pe.reshape(-1) @ w1 + b1
    return jax.nn.gelu(pre) @ w2 + b2


def compressed_kv(k_raw, v_raw, pe, w1, b1, w2, b2):
    kc = compress(k_raw, pe[0], w1[0], b1[0], w2[0], b2[0])
    vc = compress(v_raw, pe[1], w1[1], b1[1], w2[1], b2[1])
    cend = jnp.arange(kc.shape[1]) * CMP_STRIDE + CMP_BLK - 1
    return rope(kc, cend), vc, cend


def cmp_to_sel(n_c, n_sel):
    cs = jnp.arange(n_c)[:, None] * CMP_STRIDE
    ss = jnp.arange(n_sel)[None, :] * SEL_BLK
    ov = jnp.minimum(cs + CMP_BLK, ss + SEL_BLK) - jnp.maximum(cs, ss)
    return jnp.clip(ov, 0, None).astype(f32) / CMP_BLK


def select_blocks(p_cmp, qpos, n_sel):
    imp = jnp.einsum('bqgc,cs->bqgs', p_cmp, cmp_to_sel(p_cmp.shape[-1], n_sel))
    j = jnp.arange(n_sel)[None, :]
    cur = (qpos // SEL_BLK)[:, None]
    valid = (j * SEL_BLK <= qpos[:, None])[:, None, :]
    forced = ((j == 0) | (j == cur) | (j == cur - 1))[:, None, :]
    score = jnp.where(valid, jnp.where(forced, jnp.inf, imp), -jnp.inf)
    top, idx = lax.top_k(score, min(TOPK, n_sel))
    return idx, top > -jnp.inf


def nsa_core(q, qpos, kc, vc, cend, n_sel, fetch_sel, kw, vw, kwpos, gates):
    qf = q.astype(f32) * HEAD_DIM ** -0.5
    s = jnp.einsum('bqgrd,bcgd->bqgrc', qf, kc.astype(f32))
    pc = masked_softmax(s, (cend[None, :] <= qpos[:, None])[None, :, None, None, :])
    o_c = jnp.einsum('bqgrc,bcgd->bqgrd', pc, vc.astype(f32))
    idx, ok = select_blocks(pc.sum(3), qpos, n_sel)
    ks, vs = fetch_sel(idx)
    b_, q_, g_, k_ = idx.shape
    kpos = idx[..., None] * SEL_BLK + jnp.arange(SEL_BLK)
    ms = (ok[..., None] & (kpos <= qpos[None, :, None, None, None])).reshape(b_, q_, g_, 1, k_ * SEL_BLK)
    ks = ks.reshape(b_, q_, g_, k_ * SEL_BLK, HEAD_DIM).astype(f32)
    vs = vs.reshape(b_, q_, g_, k_ * SEL_BLK, HEAD_DIM).astype(f32)
    ps = masked_softmax(jnp.einsum('bqgrd,bqgkd->bqgrk', qf, ks), ms)
    o_s = jnp.einsum('bqgrk,bqgkd->bqgrd', ps, vs)
    dpos = qpos[:, None] - kwpos[None, :]
    mw = ((dpos >= 0) & (dpos < WINDOW) & (kwpos >= 0)[None, :])[None, :, None, None, :]
    pw = masked_softmax(jnp.einsum('bqgrd,bkgd->bqgrk', qf, kw.astype(f32)), mw)
    o_w = jnp.einsum('bqgrk,bkgd->bqgrd', pw, vw.astype(f32))
    return gates[..., 0:1] * o_c + gates[..., 1:2] * o_s + gates[..., 2:3] * o_w


def nsa_project(h, w_in, pos):
    b_, t_, _ = h.shape
    qd, kd = N_HEADS * HEAD_DIM, N_KV * HEAD_DIM
    proj = h @ w_in
    q = rope(proj[..., :qd].reshape(b_, t_, N_HEADS, HEAD_DIM), pos).reshape(b_, t_, N_KV, Q_PER_KV, HEAD_DIM)
    kv = proj[..., qd:qd + 6 * kd].reshape(b_, t_, 6, N_KV, HEAD_DIM)
    rows = jnp.stack([kv[:, :, 0], kv[:, :, 1], rope(kv[:, :, 2], pos), kv[:, :, 3]], axis=2)
    win = jnp.stack([rope(kv[:, :, 4], pos), kv[:, :, 5]], axis=2)
    gates = jax.nn.sigmoid(proj[..., qd + 6 * kd:].astype(f32)).reshape(b_, t_, N_KV, Q_PER_KV, 3)
    return q, rows, win, gates


def nsa_prompt(h, w_in, w_o, pe, cw1, cb1, cw2, cb2):
    b_, t_, _ = h.shape
    q, rows, win, gates = nsa_project(h, w_in, jnp.arange(t_))
    kc, vc, cend = compressed_kv(rows[:, :, 0], rows[:, :, 1], pe, cw1, cb1, cw2, cb2)
    n_sel = t_ // SEL_BLK
    k_blk = rows[:, :, 2].reshape(b_, n_sel, SEL_BLK, N_KV, HEAD_DIM)
    v_blk = rows[:, :, 3].reshape(b_, n_sel, SEL_BLK, N_KV, HEAD_DIM)
    bi = jnp.arange(b_)[:, None, None, None]
    gi = jnp.arange(N_KV)[None, None, :, None]

    def fetch(idx):
        return k_blk[bi, idx, :, gi], v_blk[bi, idx, :, gi]

    w_pad = jnp.pad(win, ((0, 0), (WINDOW, 0), (0, 0), (0, 0), (0, 0)))

    def block(i):
        s0 = i * Q_BLOCK
        qb = lax.dynamic_slice_in_dim(q, s0, Q_BLOCK, axis=1)
        gb = lax.dynamic_slice_in_dim(gates, s0, Q_BLOCK, axis=1)
        wb = lax.dynamic_slice_in_dim(w_pad, s0, Q_BLOCK + WINDOW, axis=1)
        qpos = s0 + jnp.arange(Q_BLOCK)
        kwpos = s0 - WINDOW + jnp.arange(Q_BLOCK + WINDOW)
        return nsa_core(qb, qpos, kc, vc, cend, n_sel, fetch, wb[:, :, 0], wb[:, :, 1], kwpos, gb)

    o = lax.map(block, jnp.arange(t_ // Q_BLOCK))
    o = jnp.moveaxis(o, 0, 1).reshape(b_, t_, N_HEADS * HEAD_DIM).astype(h.dtype)
    return o @ w_o, (rows, win[:, t_ - min(WINDOW, t_):])


def nsa_sample(h, cache_kv, page_table, li, win_buf, w_in, w_o, pe, cw1, cb1, cw2, cb2):
    b_, s_, _ = h.shape
    pos = PAST_LEN + jnp.arange(s_)
    q, rows, win, gates = nsa_project(h, w_in, pos)
    past = cache_kv[page_table, li, :, :2].reshape(b_, -1, 2, N_KV, HEAD_DIM)
    past_len = past.shape[1]
    k_raw = jnp.concatenate([past[:, :, 0], rows[:, :, 0]], axis=1)
    v_raw = jnp.concatenate([past[:, :, 1], rows[:, :, 1]], axis=1)
    kc, vc, cend = compressed_kv(k_raw, v_raw, pe, cw1, cb1, cw2, cb2)
    n_past_blk = past_len // SEL_BLK
    n_new_blk = -(-s_ // SEL_BLK)
    n_sel = n_past_blk + n_new_blk
    bpp = PAGE_SIZE // SEL_BLK
    new_sel = jnp.pad(rows[:, :, 2:4], ((0, 0), (0, n_new_blk * SEL_BLK - s_), (0, 0), (0, 0), (0, 0)))
    new_k = new_sel[:, :, 0].reshape(b_, n_new_blk, SEL_BLK, N_KV, HEAD_DIM)
    new_v = new_sel[:, :, 1].reshape(b_, n_new_blk, SEL_BLK, N_KV, HEAD_DIM)
    bi = jnp.arange(b_)[:, None, None, None]
    gi = jnp.arange(N_KV)[None, None, :, None]

    def fetch(idx):
        jp = jnp.minimum(idx, n_past_blk - 1)
        phys = page_table[bi, jp // bpp][..., None]
        rr = (jp % bpp)[..., None] * SEL_BLK + jnp.arange(SEL_BLK)
        pk = cache_kv[phys, li, rr, 2, gi[..., None]]
        pv = cache_kv[phys, li, rr, 3, gi[..., None]]
        jn = jnp.clip(idx - n_past_blk, 0, n_new_blk - 1)
        is_past = (idx < n_past_blk)[..., None, None]
        return (jnp.where(is_past, pk, new_k[bi, jn, :, gi]), jnp.where(is_past, pv, new_v[bi, jn, :, gi]))

    w_buf = win_buf.shape[1]
    w_all = jnp.concatenate([win_buf.astype(win.dtype), win], axis=1)
    kwpos = PAST_LEN - w_buf + jnp.arange(w_buf + s_)
    o = nsa_core(q, pos, kc, vc, cend, n_sel, fetch, w_all[:, :, 0], w_all[:, :, 1], kwpos, gates)
    o = o.reshape(b_, s_, N_HEADS * HEAD_DIM).astype(h.dtype)
    return o @ w_o, (rows, w_all[:, -w_buf:])


def segsum(a):
    t_ = a.shape[-1]
    ii = jnp.arange(t_)
    x = jnp.where(ii[:, None] > ii[None, :], jnp.broadcast_to(a[..., :, None], a.shape + (t_,)), 0.0)
    return jnp.where(ii[:, None] >= ii[None, :], jnp.cumsum(x, axis=-2), -jnp.inf)


def ssd_scan(x, dt, a_head, bm, cm, h0):
    b_, t_, g_, r_, p_ = x.shape
    n_ = bm.shape[-1]
    lc = min(SSM_CHUNK, t_)
    nc = -(-t_ // lc)
    pad = nc * lc - t_

    def padt(z):
        return jnp.pad(z, [(0, 0), (0, pad)] + [(0, 0)] * (z.ndim - 2))

    xd = padt(x * dt[..., None]).reshape(b_, nc, lc, g_, r_, p_)
    a = jnp.moveaxis(padt(dt * a_head).reshape(b_, nc, lc, g_, r_), (1, 2), (3, 4))
    bc = padt(bm).reshape(b_, nc, lc, g_, n_)
    cc = padt(cm).reshape(b_, nc, lc, g_, n_)
    a_cs = jnp.cumsum(a, axis=-1)
    lmat = jnp.exp(segsum(a))
    cb = jnp.einsum('bclgn,bcsgn->bgcls', cc, bc)
    y_diag = jnp.einsum('bgrcls,bcsgrp->bclgrp', cb[:, :, None] * lmat, xd)
    decay = jnp.exp(a_cs[..., -1:] - a_cs)
    states = jnp.einsum('bclgn,bgrcl,bclgrp->bcgrpn', bc, decay, xd)
    states = jnp.concatenate([h0[:, None], states], axis=1)
    chunk_a = jnp.pad(a_cs[..., -1], [(0, 0)] * 3 + [(1, 0)])
    states = jnp.einsum('bgrzc,bcgrpn->bzgrpn', jnp.exp(segsum(chunk_a)), states)
    y_off = jnp.einsum('bclgn,bcgrpn,bgrcl->bclgrp', cc, states[:, :-1], jnp.exp(a_cs))
    y = (y_diag + y_off).reshape(b_, nc * lc, g_, r_, p_)[:, :t_]
    return y, states[:, -1]


def ssd_mixer(h, conv_state, ssm_state, w_in, conv_w, conv_b, dt_bias, a_log, d_skip, norm_g, w_out):
    b_, t_, _ = h.shape
    hpg = SSM_HEADS // SSM_GROUPS
    gn = SSM_GROUPS * D_STATE
    proj = h @ w_in
    z = proj[..., :D_INNER]
    xbc = proj[..., D_INNER:D_INNER + CONV_DIM]
    dt_raw = proj[..., D_INNER + CONV_DIM:]
    full = jnp.concatenate([conv_state.astype(xbc.dtype), xbc], axis=1)
    xbc = jax.nn.silu(sum(full[:, k:k + t_] * conv_w[k] for k in range(CONV_W)) + conv_b)
    xs = xbc[..., :D_INNER].reshape(b_, t_, SSM_GROUPS, hpg, SSM_HEAD_DIM).astype(f32)
    bm = xbc[..., D_INNER:D_INNER + gn].reshape(b_, t_, SSM_GROUPS, D_STATE).astype(f32)
    cm = xbc[..., D_INNER + gn:].reshape(b_, t_, SSM_GROUPS, D_STATE).astype(f32)
    dt = jax.nn.softplus(dt_raw.astype(f32) + dt_bias.astype(f32)).reshape(b_, t_, SSM_GROUPS, hpg)
    a_head = -jnp.exp(a_log.astype(f32)).reshape(SSM_GROUPS, hpg)
    h0 = ssm_state.astype(f32).reshape(b_, SSM_GROUPS, hpg, SSM_HEAD_DIM, D_STATE)
    y, h_t = ssd_scan(xs, dt, a_head, bm, cm, h0)
    y = y + d_skip.astype(f32).reshape(SSM_GROUPS, hpg, 1) * xs
    y = y.reshape(b_, t_, SSM_GROUPS, D_INNER // SSM_GROUPS) * jax.nn.silu(z.astype(f32)).reshape(b_, t_, SSM_GROUPS, D_INNER // SSM_GROUPS)
    y = y * lax.rsqrt(jnp.mean(y * y, axis=-1, keepdims=True) + LN_EPS)
    y = (y.reshape(b_, t_, D_INNER) * norm_g).astype(h.dtype)
    new_ssm = h_t.reshape(b_, SSM_HEADS, SSM_HEAD_DIM, D_STATE).astype(h.dtype)
    return y @ w_out, (full[:, t_:], new_ssm)


def trunk_layer(x, c, i, mixer, ada_w, ada_b, ln_g, ln_b, ffn_w1, ffn_w3, ffn_w2):
    m = (jax.nn.silu(c) @ ada_w[i] + ada_b[i]).reshape(c.shape[0], 3, 3, 1, D_MODEL)

    def mod(v, s):
        return v * (1 + m[:, s, 1]) + m[:, s, 0]

    def post(v, y, s):
        return layer_norm(ALPHA * v + (1 + m[:, s, 2]) * y, ln_g[i, s], ln_b[i, s])

    x = post(x, 0.5 * swiglu(mod(x, 0), ffn_w1[i, 0], ffn_w3[i, 0], ffn_w2[i, 0]), 0)
    y, st = mixer(mod(x, 1))
    x = post(x, y, 1)
    x = post(x, 0.5 * swiglu(mod(x, 2), ffn_w1[i, 1], ffn_w3[i, 1], ffn_w2[i, 1]), 2)
    return x, st


def setup_inputs(seed: int = 0) -> dict:
    key = jax.random.key(seed)
    ks = jax.random.split(key, 40)

    def nrm(k, shape, scale=1.0):
        return jax.random.normal(k, shape, f32) * scale

    n_pages = PAST_LEN // PAGE_SIZE
    n_used = DEC_BATCH * n_pages
    n_pool = n_used + max(1, n_used // 4)
    w_buf = min(WINDOW, PAST_LEN)
    page_table = jax.random.permutation(ks[0], n_pool)[:n_used].reshape(DEC_BATCH, n_pages).astype(jnp.int32)
    dt0 = jnp.exp(jax.random.uniform(ks[30], (N_SSM, SSM_HEADS), f32, math.log(1e-3), math.log(1e-1)))
    qd = N_HEADS * HEAD_DIM
    return {
        "x_prompt": nrm(ks[1], (BATCH, SEQ, D_MODEL)),
        "x_sample": nrm(ks[2], (DEC_BATCH, DEC_SEQ, D_MODEL)),
        "cache_kv": nrm(ks[3], (n_pool, N_NSA, PAGE_SIZE, N_KV_COMP, N_KV, HEAD_DIM)),
        "cache_win": nrm(ks[4], (DEC_BATCH, N_NSA, w_buf, 2, N_KV, HEAD_DIM)),
        "state_conv": nrm(ks[5], (DEC_BATCH, N_SSM, CONV_W - 1, CONV_DIM)),
        "state_ssm": nrm(ks[6], (DEC_BATCH, N_SSM, SSM_HEADS, SSM_HEAD_DIM, D_STATE), 0.1),
        "page_table": page_table,
        "c_prompt": nrm(ks[7], (BATCH, D_MODEL)),
        "c_sample": nrm(ks[8], (DEC_BATCH, D_MODEL)),
        "ada_w": nrm(ks[9], (DEPTH, D_MODEL, N_ADA * D_MODEL), 0.1 * D_MODEL ** -0.5),
        "ada_b": nrm(ks[10], (DEPTH, N_ADA * D_MODEL), 0.01),
        "ln_g": 1.0 + nrm(ks[11], (DEPTH, 3, D_MODEL), 0.05),
        "ln_b": nrm(ks[12], (DEPTH, 3, D_MODEL), 0.02),
        "ffn_w1": nrm(ks[13], (DEPTH, 2, D_MODEL, D_FF), D_MODEL ** -0.5),
        "ffn_w3": nrm(ks[14], (DEPTH, 2, D_MODEL, D_FF), D_MODEL ** -0.5),
        "ffn_w2": nrm(ks[15], (DEPTH, 2, D_FF, D_MODEL), BETA * D_FF ** -0.5),
        "nsa_w_in": nrm(ks[16], (N_NSA, D_MODEL, NSA_IN), D_MODEL ** -0.5),
        "nsa_w_o": nrm(ks[17], (N_NSA, qd, D_MODEL), BETA * qd ** -0.5),
        "nsa_cmp_pe": nrm(ks[18], (N_NSA, 2, CMP_BLK, HEAD_DIM), 0.1),
        "nsa_cmp_w1": nrm(ks[19], (N_NSA, 2, CMP_BLK * HEAD_DIM, CMP_HIDDEN), (CMP_BLK * HEAD_DIM) ** -0.5),
        "nsa_cmp_b1": nrm(ks[20], (N_NSA, 2, CMP_HIDDEN), 0.02),
        "nsa_cmp_w2": nrm(ks[21], (N_NSA, 2, CMP_HIDDEN, HEAD_DIM), CMP_HIDDEN ** -0.5),
        "nsa_cmp_b2": nrm(ks[22], (N_NSA, 2, HEAD_DIM), 0.02),
        "ssm_w_in": nrm(ks[23], (N_SSM, D_MODEL, SSM_IN), D_MODEL ** -0.5),
        "ssm_conv_w": nrm(ks[24], (N_SSM, CONV_W, CONV_DIM), CONV_W ** -0.5),
        "ssm_conv_b": nrm(ks[25], (N_SSM, CONV_DIM), 0.02),
        "ssm_dt_bias": dt0 + jnp.log(-jnp.expm1(-dt0)),
        "ssm_a_log": jnp.log(jax.random.uniform(ks[26], (N_SSM, SSM_HEADS), f32, 1.0, 16.0)),
        "ssm_d": 1.0 + nrm(ks[27], (N_SSM, SSM_HEADS), 0.1),
        "ssm_norm_g": 1.0 + nrm(ks[28], (N_SSM, D_INNER), 0.05),
        "ssm_w_out": nrm(ks[29], (N_SSM, D_INNER, D_MODEL), BETA * D_INNER ** -0.5),
    }


def reference(x_prompt, x_sample, cache_kv, cache_win, state_conv, state_ssm, page_table, c_prompt, c_sample,
              ada_w, ada_b, ln_g, ln_b, ffn_w1, ffn_w3, ffn_w2,
              nsa_w_in, nsa_w_o, nsa_cmp_pe, nsa_cmp_w1, nsa_cmp_b1, nsa_cmp_w2, nsa_cmp_b2,
              ssm_w_in, ssm_conv_w, ssm_conv_b, ssm_dt_bias, ssm_a_log, ssm_d, ssm_norm_g, ssm_w_out):
    lw = (ada_w, ada_b, ln_g, ln_b, ffn_w1, ffn_w3, ffn_w2)
    xp, xs = x_prompt, x_sample
    kv_p, kv_s, win_p, win_s, conv_p, conv_s, ssm_p, ssm_s = [], [], [], [], [], [], [], []
    for i in range(DEPTH):
        j = i // N_MIXERS
        if i % N_MIXERS == 0:
            nw = (nsa_w_in[j], nsa_w_o[j], nsa_cmp_pe[j], nsa_cmp_w1[j], nsa_cmp_b1[j], nsa_cmp_w2[j], nsa_cmp_b2[j])
            xp, (r_p, w_p) = trunk_layer(xp, c_prompt, i, lambda h: nsa_prompt(h, *nw), *lw)
            xs, (r_s, w_s) = trunk_layer(xs, c_sample, i, lambda h: nsa_sample(h, cache_kv, page_table, j, cache_win[:, j], *nw), *lw)
            kv_p.append(r_p); kv_s.append(r_s); win_p.append(w_p); win_s.append(w_s)
        else:
            sw = (ssm_w_in[j], ssm_conv_w[j], ssm_conv_b[j], ssm_dt_bias[j], ssm_a_log[j], ssm_d[j], ssm_norm_g[j], ssm_w_out[j])
            zc = jnp.zeros((xp.shape[0], CONV_W - 1, CONV_DIM), xp.dtype)
            zs = jnp.zeros((xp.shape[0], SSM_HEADS, SSM_HEAD_DIM, D_STATE), f32)
            xp, (cv_p, st_p) = trunk_layer(xp, c_prompt, i, lambda h: ssd_mixer(h, zc, zs, *sw), *lw)
            xs, (cv_s, st_s) = trunk_layer(xs, c_sample, i, lambda h: ssd_mixer(h, state_conv[:, j], state_ssm[:, j], *sw), *lw)
            conv_p.append(cv_p); conv_s.append(cv_s); ssm_p.append(st_p); ssm_s.append(st_s)
    return (xp, xs, jnp.stack(kv_p, 1), jnp.stack(kv_s, 1), jnp.stack(win_p, 1), jnp.stack(win_s, 1),
            jnp.stack(conv_p, 1), jnp.stack(conv_s, 1), jnp.stack(ssm_p, 1), jnp.stack(ssm_s, 1))
```

```python
import functools
import math

import jax
import jax.numpy as jnp
from jax import lax
from jax.experimental import pallas as pl
from jax.experimental.pallas import tpu as pltpu

f32 = jnp.float32
bf16 = jnp.bfloat16

D_MODEL = 1024
DEPTH = 4
PAGE_SIZE = 128
N_HEADS = 16
HEAD_DIM = 64
N_KV = 4
Q_PER_KV = N_HEADS // N_KV
CMP_BLK = 32
CMP_STRIDE = 16
CMP_HIDDEN = 2 * HEAD_DIM
SEL_BLK = 64
TOPK = 16
WINDOW = 512
Q_BLOCK = 128
ROPE_THETA = 10000.0
D_INNER = 2 * D_MODEL
SSM_HEAD_DIM = 64
SSM_HEADS = D_INNER // SSM_HEAD_DIM
SSM_GROUPS = 4
D_STATE = 128
CONV_W = 4
CONV_DIM = D_INNER + 2 * SSM_GROUPS * D_STATE
SSM_CHUNK = 128
D_FF = 256 * ((8 * D_MODEL // 3 + 255) // 256)
N_MIXERS = 2
ALPHA = (2 * DEPTH) ** 0.25
N_ADA = 9
LN_EPS = 1e-5

V7X_VMEM_LIMIT_BYTES = 56 * 1024 * 1024
FF_CHUNK = 256
FFN_ROWS = 512


def _ffn_kernel(x_ref, sh_ref, sc_ref, gt_ref, w1_ref, w3_ref, w2_ref, lg_ref, lb_ref, o_ref, acc_ref):
    x = x_ref[0]
    hb = (x * (1.0 + sc_ref[0]) + sh_ref[0]).astype(bf16)
    acc_ref[...] = jnp.zeros_like(acc_ref)

    def chunk(c, carry):
        a = jnp.dot(hb, w1_ref[c], preferred_element_type=f32)
        b = jnp.dot(hb, w3_ref[c], preferred_element_type=f32)
        g = (a * jax.nn.sigmoid(a) * b).astype(bf16)
        acc_ref[...] += jnp.dot(g, w2_ref[c], preferred_element_type=f32)
        return carry

    lax.fori_loop(0, w1_ref.shape[0], chunk, 0)
    y = ALPHA * x + (1.0 + gt_ref[0]) * (0.5 * acc_ref[...])
    mu = jnp.mean(y, axis=-1, keepdims=True)
    yc = y - mu
    var = jnp.mean(yc * yc, axis=-1, keepdims=True)
    o_ref[0] = yc * lax.rsqrt(var + LN_EPS) * lg_ref[...] + lb_ref[...]


def ffn_sublayer(x, shift, scale, gate, w1c, w3c, w2c, ln_g, ln_b):
    b_, t_, d_ = x.shape
    rows = min(FFN_ROWS, t_)
    per_row = shift.shape[1] != 1
    mod_spec = pl.BlockSpec((1, rows if per_row else 1, d_), (lambda b, t: (b, t, 0)) if per_row else (lambda b, t: (b, 0, 0)))
    const3 = lambda b, t: (0, 0, 0)
    wspec = lambda w: pl.BlockSpec(w.shape, const3, pipeline_mode=pl.Buffered(1))
    vec = pl.BlockSpec((1, d_), lambda b, t: (0, 0))
    return pl.pallas_call(
        _ffn_kernel,
        grid=(b_, t_ // rows),
        in_specs=[pl.BlockSpec((1, rows, d_), lambda b, t: (b, t, 0)), mod_spec, mod_spec, mod_spec,
                  wspec(w1c), wspec(w3c), wspec(w2c), vec, vec],
        out_specs=pl.BlockSpec((1, rows, d_), lambda b, t: (b, t, 0)),
        out_shape=jax.ShapeDtypeStruct(x.shape, f32),
        scratch_shapes=[pltpu.VMEM((rows, d_), f32)],
        compiler_params=pltpu.CompilerParams(dimension_semantics=("arbitrary", "arbitrary"),
                                             vmem_limit_bytes=V7X_VMEM_LIMIT_BYTES),
        name="ffn_sublayer",
    )(x, shift, scale, gate, w1c, w3c, w2c, ln_g.reshape(1, d_), ln_b.reshape(1, d_))


def _chunk_ffn_weights(w1, w3, w2):
    n = D_FF // FF_CHUNK
    w1c = w1.astype(bf16).reshape(D_MODEL, n, FF_CHUNK).transpose(1, 0, 2)
    w3c = w3.astype(bf16).reshape(D_MODEL, n, FF_CHUNK).transpose(1, 0, 2)
    w2c = w2.astype(bf16).reshape(n, FF_CHUNK, D_MODEL)
    return w1c, w3c, w2c


def _layer_norm(x, g, b):
    mu = x.mean(-1, keepdims=True)
    var = jnp.square(x - mu).mean(-1, keepdims=True)
    return (x - mu) * lax.rsqrt(var + LN_EPS) * g + b


def _rope(x, pos):
    half = HEAD_DIM // 2
    inv = ROPE_THETA ** (-jnp.arange(half, dtype=f32) / half)
    ang = pos.astype(f32)[:, None] * inv
    cos, sin = jnp.cos(ang)[:, None, :], jnp.sin(ang)[:, None, :]
    x1, x2 = x[..., :half], x[..., half:]
    return jnp.concatenate([x1 * cos - x2 * sin, x2 * cos + x1 * sin], -1)


def _masked_softmax(s, mask):
    s = jnp.where(mask, s, -jnp.inf)
    m = jnp.max(s, axis=-1, keepdims=True)
    e = jnp.exp(s - jnp.where(jnp.isfinite(m), m, 0.0))
    return e / jnp.maximum(jnp.sum(e, axis=-1, keepdims=True), 1e-30)


def _compress(kv, pe, w1, b1, w2, b2):
    b_, l_, g_, d_ = kv.shape
    n_sub = CMP_BLK // CMP_STRIDE
    n_c = (l_ - CMP_BLK) // CMP_STRIDE + 1
    n_ch = n_c + n_sub - 1
    ch = kv[:, :n_ch * CMP_STRIDE].reshape(b_, n_ch, CMP_STRIDE, g_, d_)
    ch = jnp.moveaxis(ch, 3, 2).reshape(b_, n_ch, g_, CMP_STRIDE * d_)
    w1s = w1.reshape(n_sub, CMP_STRIDE * d_, CMP_HIDDEN)
    proj = jnp.einsum('bngf,jfh->jbngh', ch, w1s)
    pre = sum(proj[j][:, j:j + n_c] for j in range(n_sub)) + pe.reshape(-1) @ w1 + b1
    return jax.nn.gelu(pre) @ w2 + b2


def _compressed_kv(k_raw, v_raw, pe, w1, b1, w2, b2):
    kc = _compress(k_raw, pe[0], w1[0], b1[0], w2[0], b2[0])
    vc = _compress(v_raw, pe[1], w1[1], b1[1], w2[1], b2[1])
    cend = jnp.arange(kc.shape[1]) * CMP_STRIDE + CMP_BLK - 1
    return _rope(kc, cend), vc, cend


def _cmp_to_sel(n_c, n_sel):
    cs = jnp.arange(n_c)[:, None] * CMP_STRIDE
    ss = jnp.arange(n_sel)[None, :] * SEL_BLK
    ov = jnp.minimum(cs + CMP_BLK, ss + SEL_BLK) - jnp.maximum(cs, ss)
    return jnp.clip(ov, 0, None).astype(f32) / CMP_BLK


def _select_blocks(p_cmp, qpos, n_sel):
    imp = jnp.einsum('bqgc,cs->bqgs', p_cmp, _cmp_to_sel(p_cmp.shape[-1], n_sel))
    j = jnp.arange(n_sel)[None, :]
    cur = (qpos // SEL_BLK)[:, None]
    valid = (j * SEL_BLK <= qpos[:, None])[:, None, :]
    forced = ((j == 0) | (j == cur) | (j == cur - 1))[:, None, :]
    score = jnp.where(valid, jnp.where(forced, jnp.inf, imp), -jnp.inf)
    top, idx = lax.top_k(score, min(TOPK, n_sel))
    return idx, top > -jnp.inf


def _nsa_core(q, qpos, kc, vc, cend, n_sel, fetch_sel, kw, vw, kwpos, gates):
    qf = q * HEAD_DIM ** -0.5
    s = jnp.einsum('bqgrd,bcgd->bqgrc', qf, kc)
    pc = _masked_softmax(s, (cend[None, :] <= qpos[:, None])[None, :, None, None, :])
    o_c = jnp.einsum('bqgrc,bcgd->bqgrd', pc, vc)
    idx, ok = _select_blocks(pc.sum(3), qpos, n_sel)
    ks, vs = fetch_sel(idx)
    b_, q_, g_, k_ = idx.shape
    kpos = idx[..., None] * SEL_BLK + jnp.arange(SEL_BLK)
    ms = (ok[..., None] & (kpos <= qpos[None, :, None, None, None])).reshape(b_, q_, g_, 1, k_ * SEL_BLK)
    ks = ks.reshape(b_, q_, g_, k_ * SEL_BLK, HEAD_DIM)
    vs = vs.reshape(b_, q_, g_, k_ * SEL_BLK, HEAD_DIM)
    ps = _masked_softmax(jnp.einsum('bqgrd,bqgkd->bqgrk', qf, ks), ms)
    o_s = jnp.einsum('bqgrk,bqgkd->bqgrd', ps, vs)
    dpos = qpos[:, None] - kwpos[None, :]
    mw = ((dpos >= 0) & (dpos < WINDOW) & (kwpos >= 0)[None, :])[None, :, None, None, :]
    pw = _masked_softmax(jnp.einsum('bqgrd,bkgd->bqgrk', qf, kw), mw)
    o_w = jnp.einsum('bqgrk,bkgd->bqgrd', pw, vw)
    return gates[..., 0:1] * o_c + gates[..., 1:2] * o_s + gates[..., 2:3] * o_w


def _nsa_project(h, w_in, pos):
    b_, t_, _ = h.shape
    qd, kd = N_HEADS * HEAD_DIM, N_KV * HEAD_DIM
    proj = h @ w_in
    q = _rope(proj[..., :qd].reshape(b_, t_, N_HEADS, HEAD_DIM), pos).reshape(b_, t_, N_KV, Q_PER_KV, HEAD_DIM)
    kv = proj[..., qd:qd + 6 * kd].reshape(b_, t_, 6, N_KV, HEAD_DIM)
    rows = jnp.stack([kv[:, :, 0], kv[:, :, 1], _rope(kv[:, :, 2], pos), kv[:, :, 3]], axis=2)
    win = jnp.stack([_rope(kv[:, :, 4], pos), kv[:, :, 5]], axis=2)
    gates = jax.nn.sigmoid(proj[..., qd + 6 * kd:]).reshape(b_, t_, N_KV, Q_PER_KV, 3)
    return q, rows, win, gates


def _nsa_prompt(h, w_in, w_o, pe, cw1, cb1, cw2, cb2):
    b_, t_, _ = h.shape
    q, rows, win, gates = _nsa_project(h, w_in, jnp.arange(t_))
    kc, vc, cend = _compressed_kv(rows[:, :, 0], rows[:, :, 1], pe, cw1, cb1, cw2, cb2)
    n_sel = t_ // SEL_BLK
    k_blk = rows[:, :, 2].reshape(b_, n_sel, SEL_BLK, N_KV, HEAD_DIM)
    v_blk = rows[:, :, 3].reshape(b_, n_sel, SEL_BLK, N_KV, HEAD_DIM)
    bi = jnp.arange(b_)[:, None, None, None]
    gi = jnp.arange(N_KV)[None, None, :, None]

    def fetch(idx):
        return k_blk[bi, idx, :, gi], v_blk[bi, idx, :, gi]

    w_pad = jnp.pad(win, ((0, 0), (WINDOW, 0), (0, 0), (0, 0), (0, 0)))

    def block(i):
        s0 = i * Q_BLOCK
        qb = lax.dynamic_slice_in_dim(q, s0, Q_BLOCK, axis=1)
        gb = lax.dynamic_slice_in_dim(gates, s0, Q_BLOCK, axis=1)
        wb = lax.dynamic_slice_in_dim(w_pad, s0, Q_BLOCK + WINDOW, axis=1)
        qpos = s0 + jnp.arange(Q_BLOCK)
        kwpos = s0 - WINDOW + jnp.arange(Q_BLOCK + WINDOW)
        return _nsa_core(qb, qpos, kc, vc, cend, n_sel, fetch, wb[:, :, 0], wb[:, :, 1], kwpos, gb)

    o = lax.map(block, jnp.arange(t_ // Q_BLOCK))
    o = jnp.moveaxis(o, 0, 1).reshape(b_, t_, N_HEADS * HEAD_DIM)
    return o @ w_o, (rows, win[:, t_ - min(WINDOW, t_):])


def _nsa_sample(h, past_len, cache_kv, page_table, li, win_buf, w_in, w_o, pe, cw1, cb1, cw2, cb2):
    b_, s_, _ = h.shape
    pos = past_len + jnp.arange(s_)
    q, rows, win, gates = _nsa_project(h, w_in, pos)
    cmp_pages = cache_kv[:, li, :, :2].reshape(cache_kv.shape[0], -1)
    past = jnp.take(cmp_pages, page_table.reshape(-1), axis=0).reshape(b_, -1, 2, N_KV, HEAD_DIM)
    k_raw = jnp.concatenate([past[:, :, 0], rows[:, :, 0]], axis=1)
    v_raw = jnp.concatenate([past[:, :, 1], rows[:, :, 1]], axis=1)
    kc, vc, cend = _compressed_kv(k_raw, v_raw, pe, cw1, cb1, cw2, cb2)
    n_past_blk = past_len // SEL_BLK
    n_new_blk = -(-s_ // SEL_BLK)
    n_sel = n_past_blk + n_new_blk
    bpp = PAGE_SIZE // SEL_BLK
    new_sel = jnp.pad(rows[:, :, 2:4], ((0, 0), (0, n_new_blk * SEL_BLK - s_), (0, 0), (0, 0), (0, 0)))
    new_k = new_sel[:, :, 0].reshape(b_, n_new_blk, SEL_BLK, N_KV, HEAD_DIM)
    new_v = new_sel[:, :, 1].reshape(b_, n_new_blk, SEL_BLK, N_KV, HEAD_DIM)
    bi = jnp.arange(b_)[:, None, None, None]
    gi = jnp.arange(N_KV)[None, None, :, None]

    def fetch(idx):
        jp = jnp.minimum(idx, n_past_blk - 1)
        phys = page_table[bi, jp // bpp][..., None]
        rr = (jp % bpp)[..., None] * SEL_BLK + jnp.arange(SEL_BLK)
        pk = cache_kv[phys, li, rr, 2, gi[..., None]]
        pv = cache_kv[phys, li, rr, 3, gi[..., None]]
        jn = jnp.clip(idx - n_past_blk, 0, n_new_blk - 1)
        is_past = (idx < n_past_blk)[..., None, None]
        return (jnp.where(is_past, pk, new_k[bi, jn, :, gi]), jnp.where(is_past, pv, new_v[bi, jn, :, gi]))

    w_buf = win_buf.shape[1]
    w_all = jnp.concatenate([win_buf, win], axis=1)
    kwpos = past_len - w_buf + jnp.arange(w_buf + s_)
    o = _nsa_core(q, pos, kc, vc, cend, n_sel, fetch, w_all[:, :, 0], w_all[:, :, 1], kwpos, gates)
    o = o.reshape(b_, s_, N_HEADS * HEAD_DIM)
    return o @ w_o, (rows, w_all[:, -w_buf:])


def _segsum(a):
    t_ = a.shape[-1]
    ii = jnp.arange(t_)
    x = jnp.where(ii[:, None] > ii[None, :], jnp.broadcast_to(a[..., :, None], a.shape + (t_,)), 0.0)
    return jnp.where(ii[:, None] >= ii[None, :], jnp.cumsum(x, axis=-2), -jnp.inf)


def _ssd_scan(x, dt, a_head, bm, cm, h0):
    b_, t_, g_, r_, p_ = x.shape
    n_ = bm.shape[-1]
    lc = min(SSM_CHUNK, t_)
    nc = -(-t_ // lc)
    pad = nc * lc - t_

    def padt(z):
        return jnp.pad(z, [(0, 0), (0, pad)] + [(0, 0)] * (z.ndim - 2))

    xd = padt(x * dt[..., None]).reshape(b_, nc, lc, g_, r_, p_)
    a = jnp.moveaxis(padt(dt * a_head).reshape(b_, nc, lc, g_, r_), (1, 2), (3, 4))
    bc = padt(bm).reshape(b_, nc, lc, g_, n_)
    cc = padt(cm).reshape(b_, nc, lc, g_, n_)
    a_cs = jnp.cumsum(a, axis=-1)
    lmat = jnp.exp(_segsum(a))
    cb = jnp.einsum('bclgn,bcsgn->bgcls', cc, bc)
    y_diag = jnp.einsum('bgrcls,bcsgrp->bclgrp', cb[:, :, None] * lmat, xd)
    decay = jnp.exp(a_cs[..., -1:] - a_cs)
    states = jnp.einsum('bclgn,bgrcl,bclgrp->bcgrpn', bc, decay, xd)
    states = jnp.concatenate([h0[:, None], states], axis=1)
    chunk_a = jnp.pad(a_cs[..., -1], [(0, 0)] * 3 + [(1, 0)])
    states = jnp.einsum('bgrzc,bcgrpn->bzgrpn', jnp.exp(_segsum(chunk_a)), states)
    y_off = jnp.einsum('bclgn,bcgrpn,bgrcl->bclgrp', cc, states[:, :-1], jnp.exp(a_cs))
    y = (y_diag + y_off).reshape(b_, nc * lc, g_, r_, p_)[:, :t_]
    return y, states[:, -1]


def _ssd_mixer(h, conv_state, ssm_state, w_in, conv_w, conv_b, dt_bias, a_log, d_skip, norm_g, w_out):
    b_, t_, _ = h.shape
    hpg = SSM_HEADS // SSM_GROUPS
    gn = SSM_GROUPS * D_STATE
    proj = h @ w_in
    z = proj[..., :D_INNER]
    xbc = proj[..., D_INNER:D_INNER + CONV_DIM]
    dt_raw = proj[..., D_INNER + CONV_DIM:]
    full = jnp.concatenate([conv_state, xbc], axis=1)
    xbc = jax.nn.silu(sum(full[:, k:k + t_] * conv_w[k] for k in range(CONV_W)) + conv_b)
    xs = xbc[..., :D_INNER].reshape(b_, t_, SSM_GROUPS, hpg, SSM_HEAD_DIM)
    bm = xbc[..., D_INNER:D_INNER + gn].reshape(b_, t_, SSM_GROUPS, D_STATE)
    cm = xbc[..., D_INNER + gn:].reshape(b_, t_, SSM_GROUPS, D_STATE)
    dt = jax.nn.softplus(dt_raw + dt_bias).reshape(b_, t_, SSM_GROUPS, hpg)
    a_head = -jnp.exp(a_log).reshape(SSM_GROUPS, hpg)
    h0 = ssm_state.reshape(b_, SSM_GROUPS, hpg, SSM_HEAD_DIM, D_STATE)
    y, h_t = _ssd_scan(xs, dt, a_head, bm, cm, h0)
    y = y + d_skip.reshape(SSM_GROUPS, hpg, 1) * xs
    y = y.reshape(b_, t_, SSM_GROUPS, D_INNER // SSM_GROUPS) * jax.nn.silu(z).reshape(b_, t_, SSM_GROUPS, D_INNER // SSM_GROUPS)
    y = y * lax.rsqrt(jnp.mean(y * y, axis=-1, keepdims=True) + LN_EPS)
    y = y.reshape(b_, t_, D_INNER) * norm_g
    new_ssm = h_t.reshape(b_, SSM_HEADS, SSM_HEAD_DIM, D_STATE)
    return y @ w_out, (full[:, t_:], new_ssm)


def _trunk_layer(x, c, i, mixer, ada_w, ada_b, ln_g, ln_b, ffn_a, ffn_b):
    b_, t_, d_ = x.shape
    m = (jax.nn.silu(c) @ ada_w[i] + ada_b[i]).reshape(b_, 3, 3, 1, D_MODEL)

    def ffn(v, s, w):
        mods = [m[:, s, k] for k in range(3)]
        if t_ < 8:
            mods = [jnp.broadcast_to(z, (b_, t_, d_)).reshape(1, b_ * t_, d_) for z in mods]
            return ffn_sublayer(v.reshape(1, b_ * t_, d_), *mods, *w, ln_g[i, s], ln_b[i, s]).reshape(b_, t_, d_)
        return ffn_sublayer(v, *mods, *w, ln_g[i, s], ln_b[i, s])

    x = ffn(x, 0, ffn_a)
    y, st = mixer(x * (1 + m[:, 1, 1]) + m[:, 1, 0])
    x = _layer_norm(ALPHA * x + (1 + m[:, 1, 2]) * y, ln_g[i, 1], ln_b[i, 1])
    x = ffn(x, 2, ffn_b)
    return x, st


def kernel(x_prompt, x_sample, cache_kv, cache_win, state_conv, state_ssm, page_table, c_prompt, c_sample, ada_w, ada_b, ln_g, ln_b, ffn_w1, ffn_w3, ffn_w2, nsa_w_in, nsa_w_o, nsa_cmp_pe, nsa_cmp_w1, nsa_cmp_b1, nsa_cmp_w2, nsa_cmp_b2, ssm_w_in, ssm_conv_w, ssm_conv_b, ssm_dt_bias, ssm_a_log, ssm_d, ssm_norm_g, ssm_w_out):
    past_len = page_table.shape[1] * PAGE_SIZE
    xp, xs = x_prompt, x_sample
    kv_p, kv_s, win_p, win_s, conv_p, conv_s, ssm_p, ssm_s = [], [], [], [], [], [], [], []
    for i in range(DEPTH):
        j = i // N_MIXERS
        ffn_a = _chunk_ffn_weights(ffn_w1[i, 0], ffn_w3[i, 0], ffn_w2[i, 0])
        ffn_b = _chunk_ffn_weights(ffn_w1[i, 1], ffn_w3[i, 1], ffn_w2[i, 1])
        lw = (ada_w, ada_b, ln_g, ln_b, ffn_a, ffn_b)
        if i % N_MIXERS == 0:
            nw = (nsa_w_in[j], nsa_w_o[j], nsa_cmp_pe[j], nsa_cmp_w1[j], nsa_cmp_b1[j], nsa_cmp_w2[j], nsa_cmp_b2[j])
            xp, (r_p, w_p) = _trunk_layer(xp, c_prompt, i, lambda h: _nsa_prompt(h, *nw), *lw)
            xs, (r_s, w_s) = _trunk_layer(xs, c_sample, i, lambda h: _nsa_sample(h, past_len, cache_kv, page_table, j, cache_win[:, j], *nw), *lw)
            kv_p.append(r_p); kv_s.append(r_s); win_p.append(w_p); win_s.append(w_s)
        else:
            sw = (ssm_w_in[j], ssm_conv_w[j], ssm_conv_b[j], ssm_dt_bias[j], ssm_a_log[j], ssm_d[j], ssm_norm_g[j], ssm_w_out[j])
            zc = jnp.zeros((xp.shape[0], CONV_W - 1, CONV_DIM), f32)
            zs = jnp.zeros((xp.shape[0], SSM_HEADS, SSM_HEAD_DIM, D_STATE), f32)
            xp, (cv_p, st_p) = _trunk_layer(xp, c_prompt, i, lambda h: _ssd_mixer(h, zc, zs, *sw), *lw)
            xs, (cv_s, st_s) = _trunk_layer(xs, c_sample, i, lambda h: _ssd_mixer(h, state_conv[:, j], state_ssm[:, j], *sw), *lw)
            conv_p.append(cv_p); conv_s.append(cv_s); ssm_p.append(st_p); ssm_s.append(st_s)
    return (xp, xs, jnp.stack(kv_p, 1), jnp.stack(kv_s, 1), jnp.stack(win_p, 1), jnp.stack(win_s, 1),
            jnp.stack(conv_p, 1), jnp.stack(conv_s, 1), jnp.stack(ssm_p, 1), jnp.stack(ssm_s, 1))
```

```python
import functools
import math

import jax
import jax.numpy as jnp
from jax import lax
from jax.experimental import pallas as pl
from jax.experimental.pallas import tpu as pltpu

f32 = jnp.float32
bf16 = jnp.bfloat16

D_MODEL = 1024
DEPTH = 4
PAGE_SIZE = 128
N_HEADS = 16
HEAD_DIM = 64
N_KV = 4
Q_PER_KV = N_HEADS // N_KV
CMP_BLK = 32
CMP_STRIDE = 16
CMP_HIDDEN = 2 * HEAD_DIM
SEL_BLK = 64
TOPK = 16
WINDOW = 512
Q_BLOCK = 128
ROPE_THETA = 10000.0
D_INNER = 2 * D_MODEL
SSM_HEAD_DIM = 64
SSM_HEADS = D_INNER // SSM_HEAD_DIM
SSM_GROUPS = 4
D_STATE = 128
CONV_W = 4
CONV_DIM = D_INNER + 2 * SSM_GROUPS * D_STATE
SSM_CHUNK = 128
D_FF = 256 * ((8 * D_MODEL // 3 + 255) // 256)
N_MIXERS = 2
ALPHA = (2 * DEPTH) ** 0.25
N_ADA = 9
LN_EPS = 1e-5

V7X_VMEM_LIMIT_BYTES = 56 * 1024 * 1024
FF_CHUNK = 256
FFN_ROWS = 512


def _ffn_kernel(x_ref, sh_ref, sc_ref, gt_ref, w1_ref, w3_ref, w2_ref, lg_ref, lb_ref, o_ref, acc_ref):
    x = x_ref[0]
    hb = (x * (1.0 + sc_ref[0]) + sh_ref[0]).astype(bf16)
    acc_ref[...] = jnp.zeros_like(acc_ref)

    def chunk(c, carry):
        a = jnp.dot(hb, w1_ref[c], preferred_element_type=f32)
        b = jnp.dot(hb, w3_ref[c], preferred_element_type=f32)
        g = (a * jax.nn.sigmoid(a) * b).astype(bf16)
        acc_ref[...] += jnp.dot(g, w2_ref[c], preferred_element_type=f32)
        return carry

    lax.fori_loop(0, w1_ref.shape[0], chunk, 0)
    y = ALPHA * x + (1.0 + gt_ref[0]) * (0.5 * acc_ref[...])
    mu = jnp.mean(y, axis=-1, keepdims=True)
    yc = y - mu
    var = jnp.mean(yc * yc, axis=-1, keepdims=True)
    o_ref[0] = yc * lax.rsqrt(var + LN_EPS) * lg_ref[...] + lb_ref[...]


def ffn_sublayer(x, shift, scale, gate, w1c, w3c, w2c, ln_g, ln_b):
    b_, t_, d_ = x.shape
    rows = min(FFN_ROWS, t_)
    per_row = shift.shape[1] != 1
    mod_spec = pl.BlockSpec((1, rows if per_row else 1, d_), (lambda b, t: (b, t, 0)) if per_row else (lambda b, t: (b, 0, 0)))
    const3 = lambda b, t: (0, 0, 0)
    wspec = lambda w: pl.BlockSpec(w.shape, const3, pipeline_mode=pl.Buffered(1))
    vec = pl.BlockSpec((1, d_), lambda b, t: (0, 0))
    return pl.pallas_call(
        _ffn_kernel,
        grid=(b_, t_ // rows),
        in_specs=[pl.BlockSpec((1, rows, d_), lambda b, t: (b, t, 0)), mod_spec, mod_spec, mod_spec,
                  wspec(w1c), wspec(w3c), wspec(w2c), vec, vec],
        out_specs=pl.BlockSpec((1, rows, d_), lambda b, t: (b, t, 0)),
        out_shape=jax.ShapeDtypeStruct(x.shape, f32),
        scratch_shapes=[pltpu.VMEM((rows, d_), f32)],
        compiler_params=pltpu.CompilerParams(dimension_semantics=("arbitrary", "arbitrary"),
                                             vmem_limit_bytes=V7X_VMEM_LIMIT_BYTES),
        name="ffn_sublayer",
    )(x, shift, scale, gate, w1c, w3c, w2c, ln_g.reshape(1, d_), ln_b.reshape(1, d_))


def _chunk_ffn_weights(w1, w3, w2):
    n = D_FF // FF_CHUNK
    w1c = w1.astype(bf16).reshape(D_MODEL, n, FF_CHUNK).transpose(1, 0, 2)
    w3c = w3.astype(bf16).reshape(D_MODEL, n, FF_CHUNK).transpose(1, 0, 2)
    w2c = w2.astype(bf16).reshape(n, FF_CHUNK, D_MODEL)
    return w1c, w3c, w2c


NEG = -1e30
PROJ_ROWS = 512
SEL_TK = 512
KD = N_KV * HEAD_DIM
_C_Q, _C_QR, _C_CMP, _C_SEL, _C_WIN, _C_G, _C_END = 0, 1024, 2048, 2560, 3328, 4096, 4224
N_GATE = 3 * N_HEADS


def _rot_cols(w):
    w4 = w.reshape(w.shape[0], -1, 2, HEAD_DIM // 2)
    return jnp.stack([-w4[:, :, 1], w4[:, :, 0]], axis=2).reshape(w.shape)


def _nsa_proj_weights(w_in):
    qd = N_HEADS * HEAD_DIM
    wq = w_in[:, :qd] * HEAD_DIM ** -0.5
    kc, vc, ks, vs, kw, vw = [w_in[:, qd + i * KD: qd + (i + 1) * KD] for i in range(6)]
    wg = jnp.pad(w_in[:, qd + 6 * KD:], ((0, 0), (0, 128 - N_GATE)))
    return jnp.concatenate([wq, _rot_cols(wq), kc, vc, ks, _rot_cols(ks), vs, kw, _rot_cols(kw), vw, wg], axis=1).astype(bf16)


def _rope_tables(pos, width):
    half = HEAD_DIM // 2
    inv = ROPE_THETA ** (-jnp.arange(half, dtype=f32) / half)
    ang = jnp.tile(pos.astype(f32)[:, None] * inv, (1, width // half))
    return jnp.cos(ang), jnp.sin(ang)


def _nsa_proj_kernel(x_ref, sh_ref, sc_ref, w_ref, cos_ref, sin_ref, q_ref, rows_ref, win_ref, gate_ref, kvb_ref):
    hb = (x_ref[0] * (1.0 + sc_ref[0]) + sh_ref[0]).astype(bf16)
    cos, sin = cos_ref[...], sin_ref[...]

    def mm(lo, hi):
        return jnp.dot(hb, w_ref[:, lo:hi], preferred_element_type=f32)

    def rope(a, b, c):
        return a[:, c * 128:(c + 1) * 128] * cos + b[:, c * 128:(c + 1) * 128] * sin

    qa, qb = mm(_C_Q, _C_QR), mm(_C_QR, _C_CMP)
    for c in range(N_HEADS * HEAD_DIM // 128):
        q_ref[0, :, c * 128:(c + 1) * 128] = rope(qa, qb, c).astype(bf16)
    rows_ref[0, :, 0:2 * KD] = mm(_C_CMP, _C_SEL)
    sel = mm(_C_SEL, _C_WIN)
    win = mm(_C_WIN, _C_G)
    for c in range(KD // 128):
        ks = rope(sel[:, 0:KD], sel[:, KD:2 * KD], c)
        kw = rope(win[:, 0:KD], win[:, KD:2 * KD], c)
        rows_ref[0, :, 2 * KD + c * 128:2 * KD + (c + 1) * 128] = ks
        win_ref[0, :, c * 128:(c + 1) * 128] = kw
        kvb_ref[0, :, c * 128:(c + 1) * 128] = ks.astype(bf16)
        kvb_ref[0, :, 2 * KD + c * 128:2 * KD + (c + 1) * 128] = kw.astype(bf16)
    rows_ref[0, :, 3 * KD:4 * KD] = sel[:, 2 * KD:3 * KD]
    win_ref[0, :, KD:2 * KD] = win[:, 2 * KD:3 * KD]
    kvb_ref[0, :, KD:2 * KD] = sel[:, 2 * KD:3 * KD].astype(bf16)
    kvb_ref[0, :, 3 * KD:4 * KD] = win[:, 2 * KD:3 * KD].astype(bf16)
    gate_ref[0] = jax.nn.sigmoid(mm(_C_G, _C_END))


def nsa_project(x, shift, scale, w_all, cos, sin):
    b_, t_, d_ = x.shape
    rows = min(PROJ_ROWS, t_)
    tile = lambda n: pl.BlockSpec((1, rows, n), lambda b, t: (b, t, 0))
    mod = pl.BlockSpec((1, 1, d_), lambda b, t: (b, 0, 0))
    tab = pl.BlockSpec((rows, 128), lambda b, t: (t, 0))
    return pl.pallas_call(
        _nsa_proj_kernel,
        grid=(b_, t_ // rows),
        in_specs=[tile(d_), mod, mod, pl.BlockSpec(w_all.shape, lambda b, t: (0, 0), pipeline_mode=pl.Buffered(1)), tab, tab],
        out_specs=[tile(4 * KD), tile(4 * KD), tile(2 * KD), tile(128), tile(4 * KD)],
        out_shape=[jax.ShapeDtypeStruct((b_, t_, 4 * KD), bf16), jax.ShapeDtypeStruct((b_, t_, 4 * KD), f32),
                   jax.ShapeDtypeStruct((b_, t_, 2 * KD), f32), jax.ShapeDtypeStruct((b_, t_, 128), f32),
                   jax.ShapeDtypeStruct((b_, t_, 4 * KD), bf16)],
        compiler_params=pltpu.CompilerParams(dimension_semantics=("arbitrary", "arbitrary"),
                                             vmem_limit_bytes=V7X_VMEM_LIMIT_BYTES),
        name="nsa_project",
    )(x, shift, scale, w_all, cos, sin)


def _gelu_tanh(x):
    return 0.5 * x * (1.0 + jnp.tanh(math.sqrt(2.0 / math.pi) * (x + 0.044715 * (x * x * x))))


def _compress_kernel(xa_ref, xb_ref, pe_ref, w1_ref, b1_ref, w2_ref, w2r_ref, b2_ref, b2r_ref, w2t_ref, w2rt_ref,
                     b2t_ref, b2rt_ref, cos_ref, sin_ref, cost_ref, sint_ref, on_ref, ot_ref):
    w1 = w1_ref[0]
    half = w1.shape[0] // 2
    bias = jnp.dot(pe_ref[0], w1, preferred_element_type=f32)[0:1] + b1_ref[0]
    pre = (jnp.dot(xa_ref[0, 0, 0], w1[:half], preferred_element_type=f32)
           + jnp.dot(xb_ref[0, 0, 0], w1[half:], preferred_element_type=f32) + bias)
    hb = _gelu_tanh(pre).astype(bf16)
    y = jnp.dot(hb, w2_ref[0], preferred_element_type=f32) + b2_ref[0]
    yr = jnp.dot(hb, w2r_ref[0], preferred_element_type=f32) + b2r_ref[0]
    on_ref[0, 0, 0] = y * cos_ref[0] + yr * sin_ref[0]
    nt = (((1,), (1,)), ((), ()))
    yt = lax.dot_general(w2t_ref[0], hb, nt, preferred_element_type=f32) + b2t_ref[0]
    yrt = lax.dot_general(w2rt_ref[0], hb, nt, preferred_element_type=f32) + b2rt_ref[0]
    ot_ref[0, 0, 0] = yt * cost_ref[0] + yrt * sint_ref[0]


def compress_kv(chunks, pe, w1, b1, w2, b2):
    b_, _, g_, n_ch, feat = chunks.shape
    nxt = jnp.concatenate([chunks[:, :, :, 1:], jnp.zeros_like(chunks[:, :, :, :1])], axis=3)
    cend = jnp.arange(n_ch) * CMP_STRIDE + CMP_BLK - 1
    cos, sin = _rope_tables(cend, HEAD_DIM)
    cos = jnp.stack([cos, jnp.ones_like(cos)])
    sin = jnp.stack([sin, jnp.zeros_like(sin)])
    w2b = w2.astype(bf16)
    w2r = _rot_cols(w2).astype(bf16)
    b2r = _rot_cols(b2[:, None, :])
    pe8 = jnp.broadcast_to(pe.reshape(2, 1, -1), (2, 8, pe.shape[1] * pe.shape[2])).astype(bf16)
    xspec = pl.BlockSpec((1, 1, 1, n_ch, feat), lambda b, k, g: (b, k, g, 0, 0))
    per_kv = lambda *s: pl.BlockSpec((1,) + s, lambda b, k, g: (k,) + (0,) * len(s))
    return pl.pallas_call(
        _compress_kernel,
        grid=(b_, 2, g_),
        in_specs=[xspec, xspec, per_kv(8, pe8.shape[2]), per_kv(*w1.shape[1:]), per_kv(1, CMP_HIDDEN),
                  per_kv(CMP_HIDDEN, HEAD_DIM), per_kv(CMP_HIDDEN, HEAD_DIM), per_kv(1, HEAD_DIM), per_kv(1, HEAD_DIM),
                  per_kv(HEAD_DIM, CMP_HIDDEN), per_kv(HEAD_DIM, CMP_HIDDEN), per_kv(HEAD_DIM, 1), per_kv(HEAD_DIM, 1),
                  per_kv(n_ch, HEAD_DIM), per_kv(n_ch, HEAD_DIM), per_kv(HEAD_DIM, n_ch), per_kv(HEAD_DIM, n_ch)],
        out_specs=[pl.BlockSpec((1, 1, 1, n_ch, HEAD_DIM), lambda b, k, g: (b, k, g, 0, 0)),
                   pl.BlockSpec((1, 1, 1, HEAD_DIM, n_ch), lambda b, k, g: (b, k, g, 0, 0))],
        out_shape=[jax.ShapeDtypeStruct((b_, 2, g_, n_ch, HEAD_DIM), f32),
                   jax.ShapeDtypeStruct((b_, 2, g_, HEAD_DIM, n_ch), f32)],
        compiler_params=pltpu.CompilerParams(dimension_semantics=("arbitrary",) * 3),
        name="compress_kv",
    )(chunks, nxt, pe8, w1.astype(bf16), b1[:, None, :], w2b, w2r, b2[:, None, :], b2r,
      w2b.transpose(0, 2, 1), w2r.transpose(0, 2, 1), b2[:, :, None], b2r.transpose(0, 2, 1),
      cos, sin, cos.transpose(0, 2, 1), sin.transpose(0, 2, 1))


def _cmp_to_sel_matrix(n_c, n_sel):
    cs = jnp.arange(n_c)[:, None] * CMP_STRIDE
    ss = jnp.arange(n_sel)[None, :] * SEL_BLK
    ov = jnp.minimum(cs + CMP_BLK, ss + SEL_BLK) - jnp.maximum(cs, ss)
    return jnp.clip(ov, 0, None).astype(f32) / CMP_BLK


def _softmax_rows(s, valid):
    m = jnp.max(s, axis=1, keepdims=True)
    e = jnp.where(valid, jnp.exp(s - m), 0.0)
    return e, jnp.sum(e, axis=1, keepdims=True)


def _nsa_attn_kernel(q_ref, kct_ref, vc_ref, ks_ref, vs_ref, kw_ref, vw_ref, gate_ref, msel_ref, o_ref,
                     oc_sc, score_sc, nonsel_sc, m_sc, acc_sc, *, n_cmp, n_sel):
    qt = pl.program_id(1)
    s0 = qt * Q_BLOCK
    rq = Q_PER_KV * Q_BLOCK
    row_q = s0 + (lax.broadcasted_iota(jnp.int32, (rq, 1), 0) & (Q_BLOCK - 1))

    cend = lax.broadcasted_iota(jnp.int32, (1, n_cmp), 1) * CMP_STRIDE + (CMP_BLK - 1)
    cvalid = cend <= row_q
    for g in range(N_KV):
        s = jnp.dot(q_ref[0, g, 0], kct_ref[0, g], preferred_element_type=f32)
        e, l = _softmax_rows(jnp.where(cvalid, s, NEG), cvalid)
        p = e * (1.0 / jnp.maximum(l, 1e-30))
        oc_sc[g] = jnp.dot(p.astype(bf16), vc_ref[0, g], preferred_element_type=f32)
        ps = p[0:Q_BLOCK]
        for r in range(1, Q_PER_KV):
            ps = ps + p[r * Q_BLOCK:(r + 1) * Q_BLOCK]
        hi = ps.astype(bf16)
        lo = (ps - hi.astype(f32)).astype(bf16)
        imp = (jnp.dot(hi, msel_ref[...], preferred_element_type=f32)
               + jnp.dot(lo, msel_ref[...], preferred_element_type=f32))
        score_sc[:, g * Q_BLOCK:(g + 1) * Q_BLOCK] = imp.T

    jj = lax.broadcasted_iota(jnp.int32, (n_sel, N_KV * Q_BLOCK), 0)
    qpos = s0 + (lax.broadcasted_iota(jnp.int32, (n_sel, N_KV * Q_BLOCK), 1) & (Q_BLOCK - 1))
    cur = qpos // SEL_BLK
    forced = (jj == 0) | (jj == cur) | (jj == cur - 1)
    sc0 = jnp.where(jj <= cur, jnp.where(forced, -NEG, score_sc[...]), NEG)
    jjf = jj.astype(f32)

    def pick(_, carry):
        sc, nonsel = carry
        m = jnp.max(sc, axis=0, keepdims=True)
        idx = jnp.min(jnp.where(sc == m, jjf, float(n_sel)), axis=0, keepdims=True)
        hit = (jjf == idx) & (m > NEG)
        return jnp.where(hit, NEG, sc), jnp.where(hit, 0.0, nonsel)

    _, nonsel = lax.fori_loop(0, min(TOPK, n_sel), pick, (sc0, jnp.ones((n_sel, N_KV * Q_BLOCK), f32)))
    for g in range(N_KV):
        nonsel_sc[g] = nonsel[:, g * Q_BLOCK:(g + 1) * Q_BLOCK].T.astype(bf16)

    n_tiles = (s0 + Q_BLOCK + SEL_TK - 1) // SEL_TK
    key_l = lax.broadcasted_iota(jnp.int32, (1, SEL_TK), 1)
    blk_j = lax.broadcasted_iota(jnp.int32, (n_sel, SEL_TK), 0)
    blk_l = lax.broadcasted_iota(jnp.int32, (n_sel, SEL_TK), 1)
    wkey_l = lax.broadcasted_iota(jnp.int32, (1, Q_BLOCK), 1)
    n_win = WINDOW // Q_BLOCK + 1
    for g in range(N_KV):
        q = q_ref[0, g, 0]
        m_sc[...] = jnp.full_like(m_sc, NEG)
        acc_sc[...] = jnp.zeros_like(acc_sc)

        def sel_tile(kt, causal):
            s = jnp.dot(q, ks_ref[0, g, kt], preferred_element_type=f32)
            expand = jnp.where(blk_j == (kt * SEL_TK + blk_l) // SEL_BLK, NEG, 0.0).astype(bf16)
            bias = jnp.dot(nonsel_sc[g], expand, preferred_element_type=f32)
            s = s + jnp.concatenate([bias] * Q_PER_KV, axis=0)
            if causal:
                s = jnp.where(kt * SEL_TK + key_l <= row_q, s, NEG)
            m_old = m_sc[...]
            m_new = jnp.maximum(m_old, jnp.max(s, axis=1, keepdims=True))
            p = jnp.exp(s - m_new).astype(bf16)
            acc_sc[...] = jnp.exp(m_old - m_new) * acc_sc[...] + jnp.dot(p, vs_ref[0, g, kt], preferred_element_type=f32)
            m_sc[...] = m_new

        def body(kt, carry):
            sel_tile(kt, False)
            return carry

        lax.fori_loop(0, n_tiles - 1, body, 0)
        sel_tile(n_tiles - 1, True)
        acc = acc_sc[...]
        o_s = acc[:, 0:HEAD_DIM] * (1.0 / jnp.maximum(acc[:, HEAD_DIM:HEAD_DIM + 1], 1e-30))

        s_parts, w_tiles = [], []
        for i in range(n_win):
            kt = qt - (n_win - 1) + i
            ktc = jnp.maximum(kt, 0)
            kpos = kt * Q_BLOCK + wkey_l
            dpos = row_q - kpos
            ok = (kpos >= 0) & (dpos >= 0) & (dpos < WINDOW)
            s_parts.append(jnp.where(ok, jnp.dot(q, kw_ref[0, g, ktc], preferred_element_type=f32), NEG))
            w_tiles.append(ktc)
        s_w = jnp.concatenate(s_parts, axis=1)
        e_w = jnp.exp(s_w - jnp.max(s_w, axis=1, keepdims=True)).astype(bf16)
        acc_w = jnp.dot(e_w[:, 0:Q_BLOCK], vw_ref[0, g, w_tiles[0]], preferred_element_type=f32)
        for i in range(1, n_win):
            acc_w = acc_w + jnp.dot(e_w[:, i * Q_BLOCK:(i + 1) * Q_BLOCK], vw_ref[0, g, w_tiles[i]], preferred_element_type=f32)
        o_w = acc_w[:, 0:HEAD_DIM] * (1.0 / jnp.maximum(acc_w[:, HEAD_DIM:HEAD_DIM + 1], 1e-30))

        o_c = oc_sc[g]
        gates = gate_ref[0]
        for r in range(Q_PER_KV):
            c = (g * Q_PER_KV + r) * 3
            rows = slice(r * Q_BLOCK, (r + 1) * Q_BLOCK)
            o = (gates[:, c:c + 1] * o_c[rows] + gates[:, c + 1:c + 2] * o_s[rows] + gates[:, c + 2:c + 3] * o_w[rows])
            o_ref[0, g, 0, rows, :] = o.astype(bf16)


def _with_ones(v):
    pad = jnp.zeros(v.shape[:-1] + (128 - v.shape[-1] - 1,), v.dtype)
    return jnp.concatenate([v, jnp.ones(v.shape[:-1] + (1,), v.dtype), pad], axis=-1)


def nsa_attention(q, kvb, gates, kct, vc):
    b_, t_, _ = q.shape
    n_qt, n_st, n_sel = t_ // Q_BLOCK, t_ // SEL_TK, t_ // SEL_BLK
    n_cmp = kct.shape[-1]
    rq = Q_PER_KV * Q_BLOCK
    q5 = q.reshape(b_, n_qt, Q_BLOCK, N_KV, Q_PER_KV, HEAD_DIM).transpose(0, 3, 1, 4, 2, 5).reshape(b_, N_KV, n_qt, rq, HEAD_DIM)

    def tiles(x, tk):
        return x.reshape(b_, t_ // tk, tk, N_KV, HEAD_DIM).transpose(0, 3, 1, 2, 4)

    ks = tiles(kvb[..., 0:KD], SEL_TK).transpose(0, 1, 2, 4, 3)
    vs = _with_ones(tiles(kvb[..., KD:2 * KD], SEL_TK))
    kw = tiles(kvb[..., 2 * KD:3 * KD], Q_BLOCK).transpose(0, 1, 2, 4, 3)
    vw = _with_ones(tiles(kvb[..., 3 * KD:4 * KD], Q_BLOCK))
    msel = _cmp_to_sel_matrix(n_cmp, n_sel).astype(bf16)
    per_b = lambda x: pl.BlockSpec((1,) + x.shape[1:], lambda b, t: (b,) + (0,) * (x.ndim - 1), pipeline_mode=pl.Buffered(1))
    qspec = pl.BlockSpec((1, N_KV, 1, rq, HEAD_DIM), lambda b, t: (b, 0, t, 0, 0))
    o5 = pl.pallas_call(
        functools.partial(_nsa_attn_kernel, n_cmp=n_cmp, n_sel=n_sel),
        grid=(b_, n_qt),
        in_specs=[qspec, per_b(kct), per_b(vc), per_b(ks), per_b(vs), per_b(kw), per_b(vw),
                  pl.BlockSpec((1, Q_BLOCK, 128), lambda b, t: (b, t, 0)),
                  pl.BlockSpec(msel.shape, lambda b, t: (0, 0))],
        out_specs=qspec,
        out_shape=jax.ShapeDtypeStruct(q5.shape, bf16),
        scratch_shapes=[pltpu.VMEM((N_KV, rq, HEAD_DIM), f32), pltpu.VMEM((n_sel, N_KV * Q_BLOCK), f32),
                        pltpu.VMEM((N_KV, Q_BLOCK, n_sel), bf16), pltpu.VMEM((rq, 1), f32), pltpu.VMEM((rq, 128), f32)],
        compiler_params=pltpu.CompilerParams(dimension_semantics=("arbitrary", "arbitrary"),
                                             vmem_limit_bytes=V7X_VMEM_LIMIT_BYTES),
        name="nsa_attention",
    )(q5, kct, vc, ks, vs, kw, vw, gates, msel)
    return o5.reshape(b_, N_KV, n_qt, Q_PER_KV, Q_BLOCK, HEAD_DIM).transpose(0, 2, 4, 1, 3, 5).reshape(b_, t_, N_HEADS * HEAD_DIM)


def _mixer_out_kernel(y_ref, x_ref, gt_ref, w_ref, lg_ref, lb_ref, o_ref):
    f = jnp.dot(y_ref[0], w_ref[...], preferred_element_type=f32)
    y = ALPHA * x_ref[0] + (1.0 + gt_ref[0]) * f
    mu = jnp.mean(y, axis=-1, keepdims=True)
    yc = y - mu
    var = jnp.mean(yc * yc, axis=-1, keepdims=True)
    o_ref[0] = yc * lax.rsqrt(var + LN_EPS) * lg_ref[...] + lb_ref[...]


def mixer_out(y, x, gate, w, ln_g, ln_b):
    b_, t_, d_ = x.shape
    rows = min(PROJ_ROWS, t_)
    vec = pl.BlockSpec((1, d_), lambda b, t: (0, 0))
    return pl.pallas_call(
        _mixer_out_kernel,
        grid=(b_, t_ // rows),
        in_specs=[pl.BlockSpec((1, rows, y.shape[2]), lambda b, t: (b, t, 0)), pl.BlockSpec((1, rows, d_), lambda b, t: (b, t, 0)),
                  pl.BlockSpec((1, 1, d_), lambda b, t: (b, 0, 0)), pl.BlockSpec(w.shape, lambda b, t: (0, 0)), vec, vec],
        out_specs=pl.BlockSpec((1, rows, d_), lambda b, t: (b, t, 0)),
        out_shape=jax.ShapeDtypeStruct(x.shape, f32),
        compiler_params=pltpu.CompilerParams(dimension_semantics=("arbitrary", "arbitrary"),
                                             vmem_limit_bytes=V7X_VMEM_LIMIT_BYTES),
        name="mixer_out",
    )(y, x, gate, w, ln_g.reshape(1, d_), ln_b.reshape(1, d_))


def nsa_prompt_layer(x, shift, scale, gate, w_in, w_o, pe, cw1, cb1, cw2, cb2, ln_g, ln_b):
    b_, t_, _ = x.shape
    cos, sin = _rope_tables(jnp.arange(t_), 128)
    q, rows, win, gates, kvb = nsa_project(x, shift, scale, _nsa_proj_weights(w_in), cos, sin)
    n_ch = t_ // CMP_STRIDE
    chunks = rows[..., 0:2 * KD].astype(bf16).reshape(b_, n_ch, CMP_STRIDE, 2, N_KV, HEAD_DIM)
    chunks = chunks.transpose(0, 3, 4, 1, 2, 5).reshape(b_, 2, N_KV, n_ch, CMP_STRIDE * HEAD_DIM)
    cmp_n, cmp_t = compress_kv(chunks, pe, cw1, cb1, cw2, cb2)
    o = nsa_attention(q, kvb, gates, cmp_t[:, 0].astype(bf16), cmp_n[:, 1].astype(bf16))
    x_new = mixer_out(o, x, gate, w_o.astype(bf16), ln_g, ln_b)
    return x_new, rows.reshape(b_, t_, 4, N_KV, HEAD_DIM), win.reshape(b_, t_, 2, N_KV, HEAD_DIM)


def _layer_norm(x, g, b):
    mu = x.mean(-1, keepdims=True)
    var = jnp.square(x - mu).mean(-1, keepdims=True)
    return (x - mu) * lax.rsqrt(var + LN_EPS) * g + b


def _rope(x, pos):
    half = HEAD_DIM // 2
    inv = ROPE_THETA ** (-jnp.arange(half, dtype=f32) / half)
    ang = pos.astype(f32)[:, None] * inv
    cos, sin = jnp.cos(ang)[:, None, :], jnp.sin(ang)[:, None, :]
    x1, x2 = x[..., :half], x[..., half:]
    return jnp.concatenate([x1 * cos - x2 * sin, x2 * cos + x1 * sin], -1)


def _masked_softmax(s, mask):
    s = jnp.where(mask, s, -jnp.inf)
    m = jnp.max(s, axis=-1, keepdims=True)
    e = jnp.exp(s - jnp.where(jnp.isfinite(m), m, 0.0))
    return e / jnp.maximum(jnp.sum(e, axis=-1, keepdims=True), 1e-30)


def _compress(kv, pe, w1, b1, w2, b2):
    b_, l_, g_, d_ = kv.shape
    n_sub = CMP_BLK // CMP_STRIDE
    n_c = (l_ - CMP_BLK) // CMP_STRIDE + 1
    n_ch = n_c + n_sub - 1
    ch = kv[:, :n_ch * CMP_STRIDE].reshape(b_, n_ch, CMP_STRIDE, g_, d_)
    ch = jnp.moveaxis(ch, 3, 2).reshape(b_, n_ch, g_, CMP_STRIDE * d_)
    w1s = w1.reshape(n_sub, CMP_STRIDE * d_, CMP_HIDDEN)
    proj = jnp.einsum('bngf,jfh->jbngh', ch, w1s)
    pre = sum(proj[j][:, j:j + n_c] for j in range(n_sub)) + pe.reshape(-1) @ w1 + b1
    return jax.nn.gelu(pre) @ w2 + b2


def _compressed_kv(k_raw, v_raw, pe, w1, b1, w2, b2):
    kc = _compress(k_raw, pe[0], w1[0], b1[0], w2[0], b2[0])
    vc = _compress(v_raw, pe[1], w1[1], b1[1], w2[1], b2[1])
    cend = jnp.arange(kc.shape[1]) * CMP_STRIDE + CMP_BLK - 1
    return _rope(kc, cend), vc, cend


def _cmp_to_sel(n_c, n_sel):
    cs = jnp.arange(n_c)[:, None] * CMP_STRIDE
    ss = jnp.arange(n_sel)[None, :] * SEL_BLK
    ov = jnp.minimum(cs + CMP_BLK, ss + SEL_BLK) - jnp.maximum(cs, ss)
    return jnp.clip(ov, 0, None).astype(f32) / CMP_BLK


def _select_blocks(p_cmp, qpos, n_sel):
    imp = jnp.einsum('bqgc,cs->bqgs', p_cmp, _cmp_to_sel(p_cmp.shape[-1], n_sel))
    j = jnp.arange(n_sel)[None, :]
    cur = (qpos // SEL_BLK)[:, None]
    valid = (j * SEL_BLK <= qpos[:, None])[:, None, :]
    forced = ((j == 0) | (j == cur) | (j == cur - 1))[:, None, :]
    score = jnp.where(valid, jnp.where(forced, jnp.inf, imp), -jnp.inf)
    top, idx = lax.top_k(score, min(TOPK, n_sel))
    return idx, top > -jnp.inf


def _nsa_core(q, qpos, kc, vc, cend, n_sel, fetch_sel, kw, vw, kwpos, gates):
    qf = q * HEAD_DIM ** -0.5
    s = jnp.einsum('bqgrd,bcgd->bqgrc', qf, kc)
    pc = _masked_softmax(s, (cend[None, :] <= qpos[:, None])[None, :, None, None, :])
    o_c = jnp.einsum('bqgrc,bcgd->bqgrd', pc, vc)
    idx, ok = _select_blocks(pc.sum(3), qpos, n_sel)
    ks, vs = fetch_sel(idx)
    b_, q_, g_, k_ = idx.shape
    kpos = idx[..., None] * SEL_BLK + jnp.arange(SEL_BLK)
    ms = (ok[..., None] & (kpos <= qpos[None, :, None, None, None])).reshape(b_, q_, g_, 1, k_ * SEL_BLK)
    ks = ks.reshape(b_, q_, g_, k_ * SEL_BLK, HEAD_DIM)
    vs = vs.reshape(b_, q_, g_, k_ * SEL_BLK, HEAD_DIM)
    ps = _masked_softmax(jnp.einsum('bqgrd,bqgkd->bqgrk', qf, ks), ms)
    o_s = jnp.einsum('bqgrk,bqgkd->bqgrd', ps, vs)
    dpos = qpos[:, None] - kwpos[None, :]
    mw = ((dpos >= 0) & (dpos < WINDOW) & (kwpos >= 0)[None, :])[None, :, None, None, :]
    pw = _masked_softmax(jnp.einsum('bqgrd,bkgd->bqgrk', qf, kw), mw)
    o_w = jnp.einsum('bqgrk,bkgd->bqgrd', pw, vw)
    return gates[..., 0:1] * o_c + gates[..., 1:2] * o_s + gates[..., 2:3] * o_w


def _nsa_project(h, w_in, pos):
    b_, t_, _ = h.shape
    qd, kd = N_HEADS * HEAD_DIM, N_KV * HEAD_DIM
    proj = h @ w_in
    q = _rope(proj[..., :qd].reshape(b_, t_, N_HEADS, HEAD_DIM), pos).reshape(b_, t_, N_KV, Q_PER_KV, HEAD_DIM)
    kv = proj[..., qd:qd + 6 * kd].reshape(b_, t_, 6, N_KV, HEAD_DIM)
    rows = jnp.stack([kv[:, :, 0], kv[:, :, 1], _rope(kv[:, :, 2], pos), kv[:, :, 3]], axis=2)
    win = jnp.stack([_rope(kv[:, :, 4], pos), kv[:, :, 5]], axis=2)
    gates = jax.nn.sigmoid(proj[..., qd + 6 * kd:]).reshape(b_, t_, N_KV, Q_PER_KV, 3)
    return q, rows, win, gates


def _nsa_prompt(h, w_in, w_o, pe, cw1, cb1, cw2, cb2):
    b_, t_, _ = h.shape
    q, rows, win, gates = _nsa_project(h, w_in, jnp.arange(t_))
    kc, vc, cend = _compressed_kv(rows[:, :, 0], rows[:, :, 1], pe, cw1, cb1, cw2, cb2)
    n_sel = t_ // SEL_BLK
    k_blk = rows[:, :, 2].reshape(b_, n_sel, SEL_BLK, N_KV, HEAD_DIM)
    v_blk = rows[:, :, 3].reshape(b_, n_sel, SEL_BLK, N_KV, HEAD_DIM)
    bi = jnp.arange(b_)[:, None, None, None]
    gi = jnp.arange(N_KV)[None, None, :, None]

    def fetch(idx):
        return k_blk[bi, idx, :, gi], v_blk[bi, idx, :, gi]

    w_pad = jnp.pad(win, ((0, 0), (WINDOW, 0), (0, 0), (0, 0), (0, 0)))

    def block(i):
        s0 = i * Q_BLOCK
        qb = lax.dynamic_slice_in_dim(q, s0, Q_BLOCK, axis=1)
        gb = lax.dynamic_slice_in_dim(gates, s0, Q_BLOCK, axis=1)
        wb = lax.dynamic_slice_in_dim(w_pad, s0, Q_BLOCK + WINDOW, axis=1)
        qpos = s0 + jnp.arange(Q_BLOCK)
        kwpos = s0 - WINDOW + jnp.arange(Q_BLOCK + WINDOW)
        return _nsa_core(qb, qpos, kc, vc, cend, n_sel, fetch, wb[:, :, 0], wb[:, :, 1], kwpos, gb)

    o = lax.map(block, jnp.arange(t_ // Q_BLOCK))
    o = jnp.moveaxis(o, 0, 1).reshape(b_, t_, N_HEADS * HEAD_DIM)
    return o @ w_o, (rows, win[:, t_ - min(WINDOW, t_):])


def _nsa_sample(h, past_len, cache_kv, page_table, li, win_buf, w_in, w_o, pe, cw1, cb1, cw2, cb2):
    b_, s_, _ = h.shape
    pos = past_len + jnp.arange(s_)
    q, rows, win, gates = _nsa_project(h, w_in, pos)
    cmp_pages = cache_kv[:, li, :, :2].reshape(cache_kv.shape[0], -1)
    past = jnp.take(cmp_pages, page_table.reshape(-1), axis=0).reshape(b_, -1, 2, N_KV, HEAD_DIM)
    k_raw = jnp.concatenate([past[:, :, 0], rows[:, :, 0]], axis=1)
    v_raw = jnp.concatenate([past[:, :, 1], rows[:, :, 1]], axis=1)
    kc, vc, cend = _compressed_kv(k_raw, v_raw, pe, cw1, cb1, cw2, cb2)
    n_past_blk = past_len // SEL_BLK
    n_new_blk = -(-s_ // SEL_BLK)
    n_sel = n_past_blk + n_new_blk
    bpp = PAGE_SIZE // SEL_BLK
    new_sel = jnp.pad(rows[:, :, 2:4], ((0, 0), (0, n_new_blk * SEL_BLK - s_), (0, 0), (0, 0), (0, 0)))
    new_k = new_sel[:, :, 0].reshape(b_, n_new_blk, SEL_BLK, N_KV, HEAD_DIM)
    new_v = new_sel[:, :, 1].reshape(b_, n_new_blk, SEL_BLK, N_KV, HEAD_DIM)
    bi = jnp.arange(b_)[:, None, None, None]
    gi = jnp.arange(N_KV)[None, None, :, None]

    def fetch(idx):
        jp = jnp.minimum(idx, n_past_blk - 1)
        phys = page_table[bi, jp // bpp][..., None]
        rr = (jp % bpp)[..., None] * SEL_BLK + jnp.arange(SEL_BLK)
        pk = cache_kv[phys, li, rr, 2, gi[..., None]]
        pv = cache_kv[phys, li, rr, 3, gi[..., None]]
        jn = jnp.clip(idx - n_past_blk, 0, n_new_blk - 1)
        is_past = (idx < n_past_blk)[..., None, None]
        return (jnp.where(is_past, pk, new_k[bi, jn, :, gi]), jnp.where(is_past, pv, new_v[bi, jn, :, gi]))

    w_buf = win_buf.shape[1]
    w_all = jnp.concatenate([win_buf, win], axis=1)
    kwpos = past_len - w_buf + jnp.arange(w_buf + s_)
    o = _nsa_core(q, pos, kc, vc, cend, n_sel, fetch, w_all[:, :, 0], w_all[:, :, 1], kwpos, gates)
    o = o.reshape(b_, s_, N_HEADS * HEAD_DIM)
    return o @ w_o, (rows, w_all[:, -w_buf:])


def _segsum(a):
    t_ = a.shape[-1]
    ii = jnp.arange(t_)
    x = jnp.where(ii[:, None] > ii[None, :], jnp.broadcast_to(a[..., :, None], a.shape + (t_,)), 0.0)
    return jnp.where(ii[:, None] >= ii[None, :], jnp.cumsum(x, axis=-2), -jnp.inf)


def _ssd_scan(x, dt, a_head, bm, cm, h0):
    b_, t_, g_, r_, p_ = x.shape
    n_ = bm.shape[-1]
    lc = min(SSM_CHUNK, t_)
    nc = -(-t_ // lc)
    pad = nc * lc - t_

    def padt(z):
        return jnp.pad(z, [(0, 0), (0, pad)] + [(0, 0)] * (z.ndim - 2))

    xd = padt(x * dt[..., None]).reshape(b_, nc, lc, g_, r_, p_)
    a = jnp.moveaxis(padt(dt * a_head).reshape(b_, nc, lc, g_, r_), (1, 2), (3, 4))
    bc = padt(bm).reshape(b_, nc, lc, g_, n_)
    cc = padt(cm).reshape(b_, nc, lc, g_, n_)
    a_cs = jnp.cumsum(a, axis=-1)
    lmat = jnp.exp(_segsum(a))
    cb = jnp.einsum('bclgn,bcsgn->bgcls', cc, bc)
    y_diag = jnp.einsum('bgrcls,bcsgrp->bclgrp', cb[:, :, None] * lmat, xd)
    decay = jnp.exp(a_cs[..., -1:] - a_cs)
    states = jnp.einsum('bclgn,bgrcl,bclgrp->bcgrpn', bc, decay, xd)
    states = jnp.concatenate([h0[:, None], states], axis=1)
    chunk_a = jnp.pad(a_cs[..., -1], [(0, 0)] * 3 + [(1, 0)])
    states = jnp.einsum('bgrzc,bcgrpn->bzgrpn', jnp.exp(_segsum(chunk_a)), states)
    y_off = jnp.einsum('bclgn,bcgrpn,bgrcl->bclgrp', cc, states[:, :-1], jnp.exp(a_cs))
    y = (y_diag + y_off).reshape(b_, nc * lc, g_, r_, p_)[:, :t_]
    return y, states[:, -1]


def _ssd_mixer(h, conv_state, ssm_state, w_in, conv_w, conv_b, dt_bias, a_log, d_skip, norm_g, w_out):
    b_, t_, _ = h.shape
    hpg = SSM_HEADS // SSM_GROUPS
    gn = SSM_GROUPS * D_STATE
    proj = h @ w_in
    z = proj[..., :D_INNER]
    xbc = proj[..., D_INNER:D_INNER + CONV_DIM]
    dt_raw = proj[..., D_INNER + CONV_DIM:]
    full = jnp.concatenate([conv_state, xbc], axis=1)
    xbc = jax.nn.silu(sum(full[:, k:k + t_] * conv_w[k] for k in range(CONV_W)) + conv_b)
    xs = xbc[..., :D_INNER].reshape(b_, t_, SSM_GROUPS, hpg, SSM_HEAD_DIM)
    bm = xbc[..., D_INNER:D_INNER + gn].reshape(b_, t_, SSM_GROUPS, D_STATE)
    cm = xbc[..., D_INNER + gn:].reshape(b_, t_, SSM_GROUPS, D_STATE)
    dt = jax.nn.softplus(dt_raw + dt_bias).reshape(b_, t_, SSM_GROUPS, hpg)
    a_head = -jnp.exp(a_log).reshape(SSM_GROUPS, hpg)
    h0 = ssm_state.reshape(b_, SSM_GROUPS, hpg, SSM_HEAD_DIM, D_STATE)
    y, h_t = _ssd_scan(xs, dt, a_head, bm, cm, h0)
    y = y + d_skip.reshape(SSM_GROUPS, hpg, 1) * xs
    y = y.reshape(b_, t_, SSM_GROUPS, D_INNER // SSM_GROUPS) * jax.nn.silu(z).reshape(b_, t_, SSM_GROUPS, D_INNER // SSM_GROUPS)
    y = y * lax.rsqrt(jnp.mean(y * y, axis=-1, keepdims=True) + LN_EPS)
    y = y.reshape(b_, t_, D_INNER) * norm_g
    new_ssm = h_t.reshape(b_, SSM_HEADS, SSM_HEAD_DIM, D_STATE)
    return y @ w_out, (full[:, t_:], new_ssm)


def _trunk_layer(x, c, i, mixer, ada_w, ada_b, ln_g, ln_b, ffn_a, ffn_b, fused_mixer=False):
    b_, t_, d_ = x.shape
    m = (jax.nn.silu(c) @ ada_w[i] + ada_b[i]).reshape(b_, 3, 3, 1, D_MODEL)

    def ffn(v, s, w):
        mods = [m[:, s, k] for k in range(3)]
        if t_ < 8:
            mods = [jnp.broadcast_to(z, (b_, t_, d_)).reshape(1, b_ * t_, d_) for z in mods]
            return ffn_sublayer(v.reshape(1, b_ * t_, d_), *mods, *w, ln_g[i, s], ln_b[i, s]).reshape(b_, t_, d_)
        return ffn_sublayer(v, *mods, *w, ln_g[i, s], ln_b[i, s])

    x = ffn(x, 0, ffn_a)
    if fused_mixer:
        x, st = mixer(x, m[:, 1, 0], m[:, 1, 1], m[:, 1, 2], ln_g[i, 1], ln_b[i, 1])
    else:
        y, st = mixer(x * (1 + m[:, 1, 1]) + m[:, 1, 0])
        x = _layer_norm(ALPHA * x + (1 + m[:, 1, 2]) * y, ln_g[i, 1], ln_b[i, 1])
    x = ffn(x, 2, ffn_b)
    return x, st


def kernel(x_prompt, x_sample, cache_kv, cache_win, state_conv, state_ssm, page_table, c_prompt, c_sample, ada_w, ada_b, ln_g, ln_b, ffn_w1, ffn_w3, ffn_w2, nsa_w_in, nsa_w_o, nsa_cmp_pe, nsa_cmp_w1, nsa_cmp_b1, nsa_cmp_w2, nsa_cmp_b2, ssm_w_in, ssm_conv_w, ssm_conv_b, ssm_dt_bias, ssm_a_log, ssm_d, ssm_norm_g, ssm_w_out):
    past_len = page_table.shape[1] * PAGE_SIZE
    xp, xs = x_prompt, x_sample
    kv_p, kv_s, win_p, win_s, conv_p, conv_s, ssm_p, ssm_s = [], [], [], [], [], [], [], []
    for i in range(DEPTH):
        j = i // N_MIXERS
        ffn_a = _chunk_ffn_weights(ffn_w1[i, 0], ffn_w3[i, 0], ffn_w2[i, 0])
        ffn_b = _chunk_ffn_weights(ffn_w1[i, 1], ffn_w3[i, 1], ffn_w2[i, 1])
        lw = (ada_w, ada_b, ln_g, ln_b, ffn_a, ffn_b)
        if i % N_MIXERS == 0:
            nw = (nsa_w_in[j], nsa_w_o[j], nsa_cmp_pe[j], nsa_cmp_w1[j], nsa_cmp_b1[j], nsa_cmp_w2[j], nsa_cmp_b2[j])
            def prompt_mixer(x, sh, sc, gt, lg, lb):
                x_new, rows, win = nsa_prompt_layer(x, sh, sc, gt, *nw, lg, lb)
                return x_new, (rows, win[:, x.shape[1] - min(WINDOW, x.shape[1]):])

            xp, (r_p, w_p) = _trunk_layer(xp, c_prompt, i, prompt_mixer, *lw, fused_mixer=True)
            xs, (r_s, w_s) = _trunk_layer(xs, c_sample, i, lambda h: _nsa_sample(h, past_len, cache_kv, page_table, j, cache_win[:, j], *nw), *lw)
            kv_p.append(r_p); kv_s.append(r_s); win_p.append(w_p); win_s.append(w_s)
        else:
            sw = (ssm_w_in[j], ssm_conv_w[j], ssm_conv_b[j], ssm_dt_bias[j], ssm_a_log[j], ssm_d[j], ssm_norm_g[j], ssm_w_out[j])
            zc = jnp.zeros((xp.shape[0], CONV_W - 1, CONV_DIM), f32)
            zs = jnp.zeros((xp.shape[0], SSM_HEADS, SSM_HEAD_DIM, D_STATE), f32)
            xp, (cv_p, st_p) = _trunk_layer(xp, c_prompt, i, lambda h: _ssd_mixer(h, zc, zs, *sw), *lw)
            xs, (cv_s, st_s) = _trunk_layer(xs, c_sample, i, lambda h: _ssd_mixer(h, state_conv[:, j], state_ssm[:, j], *sw), *lw)
            conv_p.append(cv_p); conv_s.append(cv_s); ssm_p.append(st_p); ssm_s.append(st_s)
    return (xp, xs, jnp.stack(kv_p, 1), jnp.stack(kv_s, 1), jnp.stack(win_p, 1), jnp.stack(win_s, 1),
            jnp.stack(conv_p, 1), jnp.stack(conv_s, 1), jnp.stack(ssm_p, 1), jnp.stack(ssm_s, 1))
```

```python
import functools
import math

import jax
import jax.numpy as jnp
from jax import lax
from jax.experimental import pallas as pl
from jax.experimental.pallas import tpu as pltpu

f32 = jnp.float32
bf16 = jnp.bfloat16

D_MODEL = 1024
DEPTH = 4
PAGE_SIZE = 128
N_HEADS = 16
HEAD_DIM = 64
N_KV = 4
Q_PER_KV = N_HEADS // N_KV
CMP_BLK = 32
CMP_STRIDE = 16
CMP_HIDDEN = 2 * HEAD_DIM
SEL_BLK = 64
TOPK = 16
WINDOW = 512
Q_BLOCK = 128
ROPE_THETA = 10000.0
D_INNER = 2 * D_MODEL
SSM_HEAD_DIM = 64
SSM_HEADS = D_INNER // SSM_HEAD_DIM
SSM_GROUPS = 4
D_STATE = 128
CONV_W = 4
CONV_DIM = D_INNER + 2 * SSM_GROUPS * D_STATE
SSM_CHUNK = 128
D_FF = 256 * ((8 * D_MODEL // 3 + 255) // 256)
N_MIXERS = 2
ALPHA = (2 * DEPTH) ** 0.25
N_ADA = 9
LN_EPS = 1e-5

V7X_VMEM_LIMIT_BYTES = 56 * 1024 * 1024
FF_CHUNK = 256
FFN_ROWS = 512


def _ffn_kernel(x_ref, sh_ref, sc_ref, gt_ref, w1_ref, w3_ref, w2_ref, lg_ref, lb_ref, o_ref, acc_ref):
    x = x_ref[0]
    hb = (x * (1.0 + sc_ref[0]) + sh_ref[0]).astype(bf16)
    acc_ref[...] = jnp.zeros_like(acc_ref)

    def chunk(c, carry):
        a = jnp.dot(hb, w1_ref[c], preferred_element_type=f32)
        b = jnp.dot(hb, w3_ref[c], preferred_element_type=f32)
        g = (a * jax.nn.sigmoid(a) * b).astype(bf16)
        acc_ref[...] += jnp.dot(g, w2_ref[c], preferred_element_type=f32)
        return carry

    lax.fori_loop(0, w1_ref.shape[0], chunk, 0)
    y = ALPHA * x + (1.0 + gt_ref[0]) * (0.5 * acc_ref[...])
    mu = jnp.mean(y, axis=-1, keepdims=True)
    yc = y - mu
    var = jnp.mean(yc * yc, axis=-1, keepdims=True)
    o_ref[0] = yc * lax.rsqrt(var + LN_EPS) * lg_ref[...] + lb_ref[...]


def _mod_spec(m, rows):
    if m.shape[1] == 1:
        return pl.BlockSpec((1, 1, m.shape[2]), lambda b, t: (b, 0, 0))
    return pl.BlockSpec((1, rows, m.shape[2]), lambda b, t: (b, t, 0))


def ffn_sublayer(x, shift, scale, gate, w1c, w3c, w2c, ln_g, ln_b):
    b_, t_, d_ = x.shape
    rows = min(FFN_ROWS, t_)
    mod_spec = _mod_spec(shift, rows)
    const3 = lambda b, t: (0, 0, 0)
    wspec = lambda w: pl.BlockSpec(w.shape, const3, pipeline_mode=pl.Buffered(1))
    vec = pl.BlockSpec((1, d_), lambda b, t: (0, 0))
    return pl.pallas_call(
        _ffn_kernel,
        grid=(b_, t_ // rows),
        in_specs=[pl.BlockSpec((1, rows, d_), lambda b, t: (b, t, 0)), mod_spec, mod_spec, mod_spec,
                  wspec(w1c), wspec(w3c), wspec(w2c), vec, vec],
        out_specs=pl.BlockSpec((1, rows, d_), lambda b, t: (b, t, 0)),
        out_shape=jax.ShapeDtypeStruct(x.shape, f32),
        scratch_shapes=[pltpu.VMEM((rows, d_), f32)],
        compiler_params=pltpu.CompilerParams(dimension_semantics=("arbitrary", "arbitrary"),
                                             vmem_limit_bytes=V7X_VMEM_LIMIT_BYTES),
        name="ffn_sublayer",
    )(x, shift, scale, gate, w1c, w3c, w2c, ln_g.reshape(1, d_), ln_b.reshape(1, d_))


def _chunk_ffn_weights(w1, w3, w2):
    n = D_FF // FF_CHUNK
    w1c = w1.astype(bf16).reshape(D_MODEL, n, FF_CHUNK).transpose(1, 0, 2)
    w3c = w3.astype(bf16).reshape(D_MODEL, n, FF_CHUNK).transpose(1, 0, 2)
    w2c = w2.astype(bf16).reshape(n, FF_CHUNK, D_MODEL)
    return w1c, w3c, w2c


NEG = -1e30
PROJ_ROWS = 512
SEL_TK = 512
KD = N_KV * HEAD_DIM
_C_Q, _C_QR, _C_CMP, _C_SEL, _C_WIN, _C_G, _C_END = 0, 1024, 2048, 2560, 3328, 4096, 4224
N_GATE = 3 * N_HEADS


def _rot_cols(w):
    w4 = w.reshape(w.shape[0], -1, 2, HEAD_DIM // 2)
    return jnp.stack([-w4[:, :, 1], w4[:, :, 0]], axis=2).reshape(w.shape)


def _nsa_proj_weights(w_in):
    qd = N_HEADS * HEAD_DIM
    wq = w_in[:, :qd] * HEAD_DIM ** -0.5
    kc, vc, ks, vs, kw, vw = [w_in[:, qd + i * KD: qd + (i + 1) * KD] for i in range(6)]
    wg = jnp.pad(w_in[:, qd + 6 * KD:], ((0, 0), (0, 128 - N_GATE)))
    return jnp.concatenate([wq, _rot_cols(wq), kc, vc, ks, _rot_cols(ks), vs, kw, _rot_cols(kw), vw, wg], axis=1).astype(bf16)


def _rope_tables(pos, width):
    half = HEAD_DIM // 2
    inv = ROPE_THETA ** (-jnp.arange(half, dtype=f32) / half)
    ang = jnp.tile(pos.astype(f32)[:, None] * inv, (1, width // half))
    return jnp.cos(ang), jnp.sin(ang)


def _nsa_proj_kernel(x_ref, sh_ref, sc_ref, w_ref, cos_ref, sin_ref, q_ref, rows_ref, win_ref, gate_ref, kvb_ref):
    hb = (x_ref[0] * (1.0 + sc_ref[0]) + sh_ref[0]).astype(bf16)
    cos, sin = cos_ref[...], sin_ref[...]

    def mm(lo, hi):
        return jnp.dot(hb, w_ref[:, lo:hi], preferred_element_type=f32)

    def rope(a, b, c):
        return a[:, c * 128:(c + 1) * 128] * cos + b[:, c * 128:(c + 1) * 128] * sin

    qa, qb = mm(_C_Q, _C_QR), mm(_C_QR, _C_CMP)
    for c in range(N_HEADS * HEAD_DIM // 128):
        q_ref[0, :, c * 128:(c + 1) * 128] = rope(qa, qb, c).astype(bf16)
    rows_ref[0, :, 0:2 * KD] = mm(_C_CMP, _C_SEL)
    sel = mm(_C_SEL, _C_WIN)
    win = mm(_C_WIN, _C_G)
    for c in range(KD // 128):
        ks = rope(sel[:, 0:KD], sel[:, KD:2 * KD], c)
        kw = rope(win[:, 0:KD], win[:, KD:2 * KD], c)
        rows_ref[0, :, 2 * KD + c * 128:2 * KD + (c + 1) * 128] = ks
        win_ref[0, :, c * 128:(c + 1) * 128] = kw
        kvb_ref[0, :, c * 128:(c + 1) * 128] = ks.astype(bf16)
        kvb_ref[0, :, 2 * KD + c * 128:2 * KD + (c + 1) * 128] = kw.astype(bf16)
    rows_ref[0, :, 3 * KD:4 * KD] = sel[:, 2 * KD:3 * KD]
    win_ref[0, :, KD:2 * KD] = win[:, 2 * KD:3 * KD]
    kvb_ref[0, :, KD:2 * KD] = sel[:, 2 * KD:3 * KD].astype(bf16)
    kvb_ref[0, :, 3 * KD:4 * KD] = win[:, 2 * KD:3 * KD].astype(bf16)
    gate_ref[0] = jax.nn.sigmoid(mm(_C_G, _C_END))


def nsa_project(x, shift, scale, w_all, cos, sin):
    b_, t_, d_ = x.shape
    rows = min(PROJ_ROWS, t_)
    tile = lambda n: pl.BlockSpec((1, rows, n), lambda b, t: (b, t, 0))
    mod = _mod_spec(shift, rows)
    tab = pl.BlockSpec((rows, 128), lambda b, t: (t, 0))
    return pl.pallas_call(
        _nsa_proj_kernel,
        grid=(b_, t_ // rows),
        in_specs=[tile(d_), mod, mod, pl.BlockSpec(w_all.shape, lambda b, t: (0, 0), pipeline_mode=pl.Buffered(1)), tab, tab],
        out_specs=[tile(4 * KD), tile(4 * KD), tile(2 * KD), tile(128), tile(4 * KD)],
        out_shape=[jax.ShapeDtypeStruct((b_, t_, 4 * KD), bf16), jax.ShapeDtypeStruct((b_, t_, 4 * KD), f32),
                   jax.ShapeDtypeStruct((b_, t_, 2 * KD), f32), jax.ShapeDtypeStruct((b_, t_, 128), f32),
                   jax.ShapeDtypeStruct((b_, t_, 4 * KD), bf16)],
        compiler_params=pltpu.CompilerParams(dimension_semantics=("arbitrary", "arbitrary"),
                                             vmem_limit_bytes=V7X_VMEM_LIMIT_BYTES),
        name="nsa_project",
    )(x, shift, scale, w_all, cos, sin)


def _gelu_tanh(x):
    return 0.5 * x * (1.0 + jnp.tanh(math.sqrt(2.0 / math.pi) * (x + 0.044715 * (x * x * x))))


def _compress_kernel(xa_ref, xb_ref, pe_ref, w1_ref, b1_ref, w2_ref, w2r_ref, b2_ref, b2r_ref, w2t_ref, w2rt_ref,
                     b2t_ref, b2rt_ref, cos_ref, sin_ref, cost_ref, sint_ref, on_ref, ot_ref):
    w1 = w1_ref[0]
    half = w1.shape[0] // 2
    bias = jnp.dot(pe_ref[0], w1, preferred_element_type=f32)[0:1] + b1_ref[0]
    pre = (jnp.dot(xa_ref[0, 0, 0], w1[:half], preferred_element_type=f32)
           + jnp.dot(xb_ref[0, 0, 0], w1[half:], preferred_element_type=f32) + bias)
    hb = _gelu_tanh(pre).astype(bf16)
    y = jnp.dot(hb, w2_ref[0], preferred_element_type=f32) + b2_ref[0]
    yr = jnp.dot(hb, w2r_ref[0], preferred_element_type=f32) + b2r_ref[0]
    on_ref[0, 0, 0] = y * cos_ref[0] + yr * sin_ref[0]
    nt = (((1,), (1,)), ((), ()))
    yt = lax.dot_general(w2t_ref[0], hb, nt, preferred_element_type=f32) + b2t_ref[0]
    yrt = lax.dot_general(w2rt_ref[0], hb, nt, preferred_element_type=f32) + b2rt_ref[0]
    ot_ref[0, 0, 0] = yt * cost_ref[0] + yrt * sint_ref[0]


def compress_kv(chunks, pe, w1, b1, w2, b2):
    b_, _, g_, n_ch, feat = chunks.shape
    nxt = jnp.concatenate([chunks[:, :, :, 1:], jnp.zeros_like(chunks[:, :, :, :1])], axis=3)
    cend = jnp.arange(n_ch) * CMP_STRIDE + CMP_BLK - 1
    cos, sin = _rope_tables(cend, HEAD_DIM)
    cos = jnp.stack([cos, jnp.ones_like(cos)])
    sin = jnp.stack([sin, jnp.zeros_like(sin)])
    w2b = w2.astype(bf16)
    w2r = _rot_cols(w2).astype(bf16)
    b2r = _rot_cols(b2[:, None, :])
    pe8 = jnp.broadcast_to(pe.reshape(2, 1, -1), (2, 8, pe.shape[1] * pe.shape[2])).astype(bf16)
    xspec = pl.BlockSpec((1, 1, 1, n_ch, feat), lambda b, k, g: (b, k, g, 0, 0))
    per_kv = lambda *s: pl.BlockSpec((1,) + s, lambda b, k, g: (k,) + (0,) * len(s))
    return pl.pallas_call(
        _compress_kernel,
        grid=(b_, 2, g_),
        in_specs=[xspec, xspec, per_kv(8, pe8.shape[2]), per_kv(*w1.shape[1:]), per_kv(1, CMP_HIDDEN),
                  per_kv(CMP_HIDDEN, HEAD_DIM), per_kv(CMP_HIDDEN, HEAD_DIM), per_kv(1, HEAD_DIM), per_kv(1, HEAD_DIM),
                  per_kv(HEAD_DIM, CMP_HIDDEN), per_kv(HEAD_DIM, CMP_HIDDEN), per_kv(HEAD_DIM, 1), per_kv(HEAD_DIM, 1),
                  per_kv(n_ch, HEAD_DIM), per_kv(n_ch, HEAD_DIM), per_kv(HEAD_DIM, n_ch), per_kv(HEAD_DIM, n_ch)],
        out_specs=[pl.BlockSpec((1, 1, 1, n_ch, HEAD_DIM), lambda b, k, g: (b, k, g, 0, 0)),
                   pl.BlockSpec((1, 1, 1, HEAD_DIM, n_ch), lambda b, k, g: (b, k, g, 0, 0))],
        out_shape=[jax.ShapeDtypeStruct((b_, 2, g_, n_ch, HEAD_DIM), f32),
                   jax.ShapeDtypeStruct((b_, 2, g_, HEAD_DIM, n_ch), f32)],
        compiler_params=pltpu.CompilerParams(dimension_semantics=("arbitrary",) * 3),
        name="compress_kv",
    )(chunks, nxt, pe8, w1.astype(bf16), b1[:, None, :], w2b, w2r, b2[:, None, :], b2r,
      w2b.transpose(0, 2, 1), w2r.transpose(0, 2, 1), b2[:, :, None], b2r.transpose(0, 2, 1),
      cos, sin, cos.transpose(0, 2, 1), sin.transpose(0, 2, 1))


def _cmp_to_sel_matrix(n_c, n_sel):
    cs = jnp.arange(n_c)[:, None] * CMP_STRIDE
    ss = jnp.arange(n_sel)[None, :] * SEL_BLK
    ov = jnp.minimum(cs + CMP_BLK, ss + SEL_BLK) - jnp.maximum(cs, ss)
    return jnp.clip(ov, 0, None).astype(f32) / CMP_BLK


def _softmax_rows(s, valid):
    m = jnp.max(s, axis=1, keepdims=True)
    e = jnp.where(valid, jnp.exp(s - m), 0.0)
    return e, jnp.sum(e, axis=1, keepdims=True)


def _nsa_attn_kernel(q_ref, kct_ref, vc_ref, ks_ref, vs_ref, kw_ref, vw_ref, gate_ref, msel_ref, o_ref,
                     oc_sc, score_sc, nonsel_sc, m_sc, acc_sc, *, n_cmp, n_sel):
    qt = pl.program_id(1)
    s0 = qt * Q_BLOCK
    rq = Q_PER_KV * Q_BLOCK
    row_q = s0 + (lax.broadcasted_iota(jnp.int32, (rq, 1), 0) & (Q_BLOCK - 1))

    cend = lax.broadcasted_iota(jnp.int32, (1, n_cmp), 1) * CMP_STRIDE + (CMP_BLK - 1)
    cvalid = cend <= row_q
    for g in range(N_KV):
        s = jnp.dot(q_ref[0, g, 0], kct_ref[0, g], preferred_element_type=f32)
        e, l = _softmax_rows(jnp.where(cvalid, s, NEG), cvalid)
        p = e * (1.0 / jnp.maximum(l, 1e-30))
        oc_sc[g] = jnp.dot(p.astype(bf16), vc_ref[0, g], preferred_element_type=f32)
        ps = p[0:Q_BLOCK]
        for r in range(1, Q_PER_KV):
            ps = ps + p[r * Q_BLOCK:(r + 1) * Q_BLOCK]
        hi = ps.astype(bf16)
        lo = (ps - hi.astype(f32)).astype(bf16)
        imp = (jnp.dot(hi, msel_ref[...], preferred_element_type=f32)
               + jnp.dot(lo, msel_ref[...], preferred_element_type=f32))
        score_sc[:, g * Q_BLOCK:(g + 1) * Q_BLOCK] = imp.T

    jj = lax.broadcasted_iota(jnp.int32, (n_sel, N_KV * Q_BLOCK), 0)
    qpos = s0 + (lax.broadcasted_iota(jnp.int32, (n_sel, N_KV * Q_BLOCK), 1) & (Q_BLOCK - 1))
    cur = qpos // SEL_BLK
    forced = (jj == 0) | (jj == cur) | (jj == cur - 1)
    sc0 = jnp.where(jj <= cur, jnp.where(forced, -NEG, score_sc[...]), NEG)
    jjf = jj.astype(f32)

    def pick(_, carry):
        sc, nonsel = carry
        m = jnp.max(sc, axis=0, keepdims=True)
        idx = jnp.min(jnp.where(sc == m, jjf, float(n_sel)), axis=0, keepdims=True)
        hit = (jjf == idx) & (m > NEG)
        return jnp.where(hit, NEG, sc), jnp.where(hit, 0.0, nonsel)

    _, nonsel = lax.fori_loop(0, min(TOPK, n_sel), pick, (sc0, jnp.ones((n_sel, N_KV * Q_BLOCK), f32)))
    for g in range(N_KV):
        nonsel_sc[g] = nonsel[:, g * Q_BLOCK:(g + 1) * Q_BLOCK].T.astype(bf16)

    n_tiles = (s0 + Q_BLOCK + SEL_TK - 1) // SEL_TK
    key_l = lax.broadcasted_iota(jnp.int32, (1, SEL_TK), 1)
    blk_j = lax.broadcasted_iota(jnp.int32, (n_sel, SEL_TK), 0)
    blk_l = lax.broadcasted_iota(jnp.int32, (n_sel, SEL_TK), 1)
    wkey_l = lax.broadcasted_iota(jnp.int32, (1, Q_BLOCK), 1)
    n_win = WINDOW // Q_BLOCK + 1
    for g in range(N_KV):
        q = q_ref[0, g, 0]
        m_sc[...] = jnp.full_like(m_sc, NEG)
        acc_sc[...] = jnp.zeros_like(acc_sc)

        def sel_tile(kt, causal):
            s = jnp.dot(q, ks_ref[0, g, kt], preferred_element_type=f32)
            expand = jnp.where(blk_j == (kt * SEL_TK + blk_l) // SEL_BLK, NEG, 0.0).astype(bf16)
            bias = jnp.dot(nonsel_sc[g], expand, preferred_element_type=f32)
            s = s + jnp.concatenate([bias] * Q_PER_KV, axis=0)
            if causal:
                s = jnp.where(kt * SEL_TK + key_l <= row_q, s, NEG)
            m_old = m_sc[...]
            m_new = jnp.maximum(m_old, jnp.max(s, axis=1, keepdims=True))
            p = jnp.exp(s - m_new).astype(bf16)
            acc_sc[...] = jnp.exp(m_old - m_new) * acc_sc[...] + jnp.dot(p, vs_ref[0, g, kt], preferred_element_type=f32)
            m_sc[...] = m_new

        def body(kt, carry):
            sel_tile(kt, False)
            return carry

        lax.fori_loop(0, n_tiles - 1, body, 0)
        sel_tile(n_tiles - 1, True)
        acc = acc_sc[...]
        o_s = acc[:, 0:HEAD_DIM] * (1.0 / jnp.maximum(acc[:, HEAD_DIM:HEAD_DIM + 1], 1e-30))

        s_parts, w_tiles = [], []
        for i in range(n_win):
            kt = qt - (n_win - 1) + i
            ktc = jnp.maximum(kt, 0)
            kpos = kt * Q_BLOCK + wkey_l
            dpos = row_q - kpos
            ok = (kpos >= 0) & (dpos >= 0) & (dpos < WINDOW)
            s_parts.append(jnp.where(ok, jnp.dot(q, kw_ref[0, g, ktc], preferred_element_type=f32), NEG))
            w_tiles.append(ktc)
        s_w = jnp.concatenate(s_parts, axis=1)
        e_w = jnp.exp(s_w - jnp.max(s_w, axis=1, keepdims=True)).astype(bf16)
        acc_w = jnp.dot(e_w[:, 0:Q_BLOCK], vw_ref[0, g, w_tiles[0]], preferred_element_type=f32)
        for i in range(1, n_win):
            acc_w = acc_w + jnp.dot(e_w[:, i * Q_BLOCK:(i + 1) * Q_BLOCK], vw_ref[0, g, w_tiles[i]], preferred_element_type=f32)
        o_w = acc_w[:, 0:HEAD_DIM] * (1.0 / jnp.maximum(acc_w[:, HEAD_DIM:HEAD_DIM + 1], 1e-30))

        o_c = oc_sc[g]
        gates = gate_ref[0]
        for r in range(Q_PER_KV):
            c = (g * Q_PER_KV + r) * 3
            rows = slice(r * Q_BLOCK, (r + 1) * Q_BLOCK)
            o = (gates[:, c:c + 1] * o_c[rows] + gates[:, c + 1:c + 2] * o_s[rows] + gates[:, c + 2:c + 3] * o_w[rows])
            o_ref[0, g, 0, rows, :] = o.astype(bf16)


def _with_ones(v):
    pad = jnp.zeros(v.shape[:-1] + (128 - v.shape[-1] - 1,), v.dtype)
    return jnp.concatenate([v, jnp.ones(v.shape[:-1] + (1,), v.dtype), pad], axis=-1)


def nsa_attention(q, kvb, gates, kct, vc):
    b_, t_, _ = q.shape
    n_qt, n_st, n_sel = t_ // Q_BLOCK, t_ // SEL_TK, t_ // SEL_BLK
    n_cmp = kct.shape[-1]
    rq = Q_PER_KV * Q_BLOCK
    q5 = q.reshape(b_, n_qt, Q_BLOCK, N_KV, Q_PER_KV, HEAD_DIM).transpose(0, 3, 1, 4, 2, 5).reshape(b_, N_KV, n_qt, rq, HEAD_DIM)

    def tiles(x, tk):
        return x.reshape(b_, t_ // tk, tk, N_KV, HEAD_DIM).transpose(0, 3, 1, 2, 4)

    ks = tiles(kvb[..., 0:KD], SEL_TK).transpose(0, 1, 2, 4, 3)
    vs = _with_ones(tiles(kvb[..., KD:2 * KD], SEL_TK))
    kw = tiles(kvb[..., 2 * KD:3 * KD], Q_BLOCK).transpose(0, 1, 2, 4, 3)
    vw = _with_ones(tiles(kvb[..., 3 * KD:4 * KD], Q_BLOCK))
    msel = _cmp_to_sel_matrix(n_cmp, n_sel).astype(bf16)
    per_b = lambda x: pl.BlockSpec((1,) + x.shape[1:], lambda b, t: (b,) + (0,) * (x.ndim - 1), pipeline_mode=pl.Buffered(1))
    qspec = pl.BlockSpec((1, N_KV, 1, rq, HEAD_DIM), lambda b, t: (b, 0, t, 0, 0))
    o5 = pl.pallas_call(
        functools.partial(_nsa_attn_kernel, n_cmp=n_cmp, n_sel=n_sel),
        grid=(b_, n_qt),
        in_specs=[qspec, per_b(kct), per_b(vc), per_b(ks), per_b(vs), per_b(kw), per_b(vw),
                  pl.BlockSpec((1, Q_BLOCK, 128), lambda b, t: (b, t, 0)),
                  pl.BlockSpec(msel.shape, lambda b, t: (0, 0))],
        out_specs=qspec,
        out_shape=jax.ShapeDtypeStruct(q5.shape, bf16),
        scratch_shapes=[pltpu.VMEM((N_KV, rq, HEAD_DIM), f32), pltpu.VMEM((n_sel, N_KV * Q_BLOCK), f32),
                        pltpu.VMEM((N_KV, Q_BLOCK, n_sel), bf16), pltpu.VMEM((rq, 1), f32), pltpu.VMEM((rq, 128), f32)],
        compiler_params=pltpu.CompilerParams(dimension_semantics=("arbitrary", "arbitrary"),
                                             vmem_limit_bytes=V7X_VMEM_LIMIT_BYTES),
        name="nsa_attention",
    )(q5, kct, vc, ks, vs, kw, vw, gates, msel)
    return o5.reshape(b_, N_KV, n_qt, Q_PER_KV, Q_BLOCK, HEAD_DIM).transpose(0, 2, 4, 1, 3, 5).reshape(b_, t_, N_HEADS * HEAD_DIM)


def _mixer_out_kernel(y_ref, x_ref, gt_ref, w_ref, lg_ref, lb_ref, o_ref):
    f = jnp.dot(y_ref[0], w_ref[...], preferred_element_type=f32)
    y = ALPHA * x_ref[0] + (1.0 + gt_ref[0]) * f
    mu = jnp.mean(y, axis=-1, keepdims=True)
    yc = y - mu
    var = jnp.mean(yc * yc, axis=-1, keepdims=True)
    o_ref[0] = yc * lax.rsqrt(var + LN_EPS) * lg_ref[...] + lb_ref[...]


def mixer_out(y, x, gate, w, ln_g, ln_b):
    b_, t_, d_ = x.shape
    rows = min(PROJ_ROWS, t_)
    vec = pl.BlockSpec((1, d_), lambda b, t: (0, 0))
    return pl.pallas_call(
        _mixer_out_kernel,
        grid=(b_, t_ // rows),
        in_specs=[pl.BlockSpec((1, rows, y.shape[2]), lambda b, t: (b, t, 0)), pl.BlockSpec((1, rows, d_), lambda b, t: (b, t, 0)),
                  _mod_spec(gate, rows), pl.BlockSpec(w.shape, lambda b, t: (0, 0)), vec, vec],
        out_specs=pl.BlockSpec((1, rows, d_), lambda b, t: (b, t, 0)),
        out_shape=jax.ShapeDtypeStruct(x.shape, f32),
        compiler_params=pltpu.CompilerParams(dimension_semantics=("arbitrary", "arbitrary"),
                                             vmem_limit_bytes=V7X_VMEM_LIMIT_BYTES),
        name="mixer_out",
    )(y, x, gate, w, ln_g.reshape(1, d_), ln_b.reshape(1, d_))


def nsa_prompt_layer(x, shift, scale, gate, w_in, w_o, pe, cw1, cb1, cw2, cb2, ln_g, ln_b):
    b_, t_, _ = x.shape
    cos, sin = _rope_tables(jnp.arange(t_), 128)
    q, rows, win, gates, kvb = nsa_project(x, shift, scale, _nsa_proj_weights(w_in), cos, sin)
    n_ch = t_ // CMP_STRIDE
    chunks = rows[..., 0:2 * KD].astype(bf16).reshape(b_, n_ch, CMP_STRIDE, 2, N_KV, HEAD_DIM)
    chunks = chunks.transpose(0, 3, 4, 1, 2, 5).reshape(b_, 2, N_KV, n_ch, CMP_STRIDE * HEAD_DIM)
    cmp_n, cmp_t = compress_kv(chunks, pe, cw1, cb1, cw2, cb2)
    o = nsa_attention(q, kvb, gates, cmp_t[:, 0].astype(bf16), cmp_n[:, 1].astype(bf16))
    x_new = mixer_out(o, x, gate, w_o.astype(bf16), ln_g, ln_b)
    return x_new, rows.reshape(b_, t_, 4, N_KV, HEAD_DIM), win.reshape(b_, t_, 2, N_KV, HEAD_DIM)


HPG = SSM_HEADS // SSM_GROUPS
GN = SSM_GROUPS * D_STATE
GW = D_INNER // SSM_GROUPS
DT_PAD = 128
CONV_KEEP = 8


def _ssd_proj_kernel(x_ref, sh_ref, sc_ref, w_ref, z_ref, xbc_ref, dt_ref):
    hb = (x_ref[0] * (1.0 + sc_ref[0]) + sh_ref[0]).astype(bf16)
    z_ref[0] = jnp.dot(hb, w_ref[:, 0:D_INNER], preferred_element_type=f32)
    xbc_ref[0] = jnp.dot(hb, w_ref[:, D_INNER:D_INNER + CONV_DIM], preferred_element_type=f32)
    dt_ref[0] = jnp.dot(hb, w_ref[:, D_INNER + CONV_DIM:], preferred_element_type=f32)


def ssd_project(x, shift, scale, w_in):
    b_, t_, d_ = x.shape
    rows = min(PROJ_ROWS, t_)
    w_all = jnp.pad(w_in, ((0, 0), (0, DT_PAD - SSM_HEADS))).astype(bf16)
    tile = lambda n: pl.BlockSpec((1, rows, n), lambda b, t: (b, t, 0))
    mod = _mod_spec(shift, rows)
    return pl.pallas_call(
        _ssd_proj_kernel,
        grid=(b_, t_ // rows),
        in_specs=[tile(d_), mod, mod, pl.BlockSpec(w_all.shape, lambda b, t: (0, 0), pipeline_mode=pl.Buffered(1))],
        out_specs=[tile(D_INNER), tile(CONV_DIM), tile(DT_PAD)],
        out_shape=[jax.ShapeDtypeStruct((b_, t_, D_INNER), f32), jax.ShapeDtypeStruct((b_, t_, CONV_DIM), f32),
                   jax.ShapeDtypeStruct((b_, t_, DT_PAD), f32)],
        compiler_params=pltpu.CompilerParams(dimension_semantics=("arbitrary", "arbitrary"),
                                             vmem_limit_bytes=V7X_VMEM_LIMIT_BYTES),
        name="ssd_project",
    )(x, shift, scale, w_all)


def _split3(v):
    p1 = v.astype(bf16)
    r1 = v - p1.astype(f32)
    p2 = r1.astype(bf16)
    p3 = (r1 - p2.astype(f32)).astype(bf16)
    return p1, p2, p3


def _dot3(parts, m, left):
    out = None
    for p in parts:
        t = jnp.dot(m, p, preferred_element_type=f32) if left else jnp.dot(p, m, preferred_element_type=f32)
        out = t if out is None else out + t
    return out


def _ssd_scan_kernel(xbc_ref, dt_ref, z_ref, cst_ref, h0_ref, cw_ref, cb_ref, dtb_ref, a_ref, d_ref, ng_ref, ex_ref,
                     y_ref, ht_ref, win_sc, st_sc, *, n_valid):
    c = pl.program_id(1)
    L = SSM_CHUNK

    @pl.when(c == 0)
    def _():
        win_sc[0:CONV_KEEP, :] = cst_ref[0]
        st_sc[...] = h0_ref[0]

    win_sc[CONV_KEEP:CONV_KEEP + L, :] = xbc_ref[0]
    acc = cb_ref[...] + win_sc[pl.ds(CONV_KEEP, L), :] * cw_ref[CONV_W - 1:CONV_W, :]
    for k in range(CONV_W - 1):
        acc = acc + win_sc[pl.ds(CONV_KEEP - (CONV_W - 1) + k, L), :] * cw_ref[k:k + 1, :]
    win_sc[0:CONV_KEEP, :] = win_sc[L:L + CONV_KEEP, :]
    xbc = acc * jax.nn.sigmoid(acc)
    xs = xbc[:, 0:D_INNER]

    t_row = lax.broadcasted_iota(jnp.int32, (L, 1), 0)
    dt_in = dt_ref[0] + dtb_ref[...]
    dt = jnp.maximum(dt_in, 0.0) + jnp.log1p(jnp.exp(-jnp.abs(dt_in)))
    dt = jnp.where(t_row < n_valid, dt, 0.0)
    a = dt * a_ref[...]
    ii = lax.broadcasted_iota(jnp.int32, (L, L), 0)
    jj = lax.broadcasted_iota(jnp.int32, (L, L), 1)
    lower = ii >= jj
    tri = jnp.where(lower, 1.0, 0.0).astype(bf16)
    a_cs = _dot3(_split3(a), tri, left=True)
    a_cs_t = a_cs.T
    ex = ex_ref[...]
    dt_e = _dot3(_split3(dt), ex, left=False)
    acs_e = _dot3(_split3(a_cs), ex, left=False)
    a_tot_e = acs_e[L - 1:L, :]
    xd = xs * dt_e
    xdd = (xd * jnp.exp(a_tot_e - acs_e)).astype(bf16)
    xdb = xd.astype(bf16)
    grow = jnp.exp(acs_e)
    lane_lo = lax.broadcasted_iota(jnp.int32, (L, 2 * SSM_HEAD_DIM), 1) < SSM_HEAD_DIM
    nt = (((1,), (1,)), ((), ()))

    for g in range(SSM_GROUPS):
        gl = slice(g * GW, (g + 1) * GW)
        bm = xbc[:, D_INNER + g * D_STATE:D_INNER + (g + 1) * D_STATE]
        cm = xbc[:, D_INNER + GN + g * D_STATE:D_INNER + GN + (g + 1) * D_STATE].astype(bf16)
        cb = lax.dot_general(cm, bm.astype(bf16), nt, preferred_element_type=f32)
        st_old = st_sc[:, gl]
        y_g = jnp.dot(cm, st_old.astype(bf16), preferred_element_type=f32) * grow[:, gl]
        st_sc[:, gl] = jnp.exp(a_tot_e[:, gl]) * st_old + jnp.dot(bm.T.astype(bf16), xdd[:, gl], preferred_element_type=f32)
        pairs = []
        for k in range(HPG // 2):
            h0 = g * HPG + 2 * k
            ms = []
            for h in (h0, h0 + 1):
                seg = a_cs[:, h:h + 1] - a_cs_t[h:h + 1, :]
                ms.append((cb * jnp.where(lower, jnp.exp(seg), 0.0)).astype(bf16))
            xp = xdb[:, h0 * SSM_HEAD_DIM:(h0 + 2) * SSM_HEAD_DIM]
            pairs.append(jnp.where(lane_lo, jnp.dot(ms[0], xp, preferred_element_type=f32),
                                   jnp.dot(ms[1], xp, preferred_element_type=f32)))
        y_g = y_g + jnp.concatenate(pairs, axis=1) + d_ref[:, gl] * xs[:, gl]
        zg = z_ref[0, :, gl]
        y_g = y_g * (zg * jax.nn.sigmoid(zg))
        y_g = y_g * lax.rsqrt(jnp.mean(y_g * y_g, axis=-1, keepdims=True) + LN_EPS)
        y_ref[0, :, gl] = (y_g * ng_ref[:, gl]).astype(bf16)

    @pl.when(c == pl.num_programs(1) - 1)
    def _():
        ht_ref[0] = st_sc[...]


def ssd_scan(xbc, dt_raw, z, conv_state, h0, conv_w, conv_b, dt_bias, a_log, d_skip, norm_g, n_valid):
    b_, t_, _ = xbc.shape
    L = SSM_CHUNK
    cst = jnp.pad(conv_state, ((0, 0), (CONV_KEEP - (CONV_W - 1), 0), (0, 0)))
    h0t = h0.transpose(0, 3, 1, 2).reshape(b_, D_STATE, D_INNER)
    vec = lambda v: jnp.pad(v, (0, DT_PAD - SSM_HEADS)).reshape(1, DT_PAD)
    ex = (jnp.arange(DT_PAD)[:, None] == (jnp.arange(D_INNER) // SSM_HEAD_DIM)[None, :]).astype(bf16)
    tile = lambda n: pl.BlockSpec((1, L, n), lambda b, c: (b, c, 0))
    per_b = lambda r, n: pl.BlockSpec((1, r, n), lambda b, c: (b, 0, 0))
    const = lambda r, n: pl.BlockSpec((r, n), lambda b, c: (0, 0))
    y, ht = pl.pallas_call(
        functools.partial(_ssd_scan_kernel, n_valid=n_valid),
        grid=(b_, t_ // L),
        in_specs=[tile(CONV_DIM), tile(DT_PAD), tile(D_INNER), per_b(CONV_KEEP, CONV_DIM), per_b(D_STATE, D_INNER),
                  const(CONV_W, CONV_DIM), const(1, CONV_DIM), const(1, DT_PAD), const(1, DT_PAD), const(1, D_INNER),
                  const(1, D_INNER), const(DT_PAD, D_INNER)],
        out_specs=[tile(D_INNER), per_b(D_STATE, D_INNER)],
        out_shape=[jax.ShapeDtypeStruct((b_, t_, D_INNER), bf16), jax.ShapeDtypeStruct((b_, D_STATE, D_INNER), f32)],
        scratch_shapes=[pltpu.VMEM((L + CONV_KEEP, CONV_DIM), f32), pltpu.VMEM((D_STATE, D_INNER), f32)],
        compiler_params=pltpu.CompilerParams(dimension_semantics=("arbitrary", "arbitrary"),
                                             vmem_limit_bytes=V7X_VMEM_LIMIT_BYTES),
        name="ssd_scan",
    )(xbc, dt_raw, z, cst, h0t, conv_w, conv_b.reshape(1, CONV_DIM), vec(dt_bias), vec(-jnp.exp(a_log)),
      jnp.repeat(d_skip, SSM_HEAD_DIM).reshape(1, D_INNER), norm_g.reshape(1, D_INNER), ex)
    return y, ht.reshape(b_, D_STATE, SSM_HEADS, SSM_HEAD_DIM).transpose(0, 2, 3, 1)


def ssd_layer(x, shift, scale, gate, conv_state, ssm_state, w_in, conv_w, conv_b, dt_bias, a_log, d_skip, norm_g, w_out,
              ln_g, ln_b):
    b_, t_, d_ = x.shape
    fold = t_ < SSM_CHUNK
    if fold:
        flat = lambda m: jnp.broadcast_to(m, (b_, t_, d_)).reshape(1, b_ * t_, d_)
        xf, shift, scale, gate = x.reshape(1, b_ * t_, d_), flat(shift), flat(scale), flat(gate)
    else:
        xf = x
    z, xbc, dt_raw = ssd_project(xf, shift, scale, w_in)
    if fold:
        padt = lambda v: jnp.pad(v.reshape(b_, t_, -1), ((0, 0), (0, SSM_CHUNK - t_), (0, 0)))
        z, xbc, dt_raw = padt(z), padt(xbc), padt(dt_raw)
    y, new_ssm = ssd_scan(xbc, dt_raw, z, conv_state, ssm_state, conv_w, conv_b, dt_bias, a_log, d_skip, norm_g, t_)
    new_conv = jnp.concatenate([conv_state, xbc[:, :t_]], axis=1)[:, t_:]
    if fold:
        y = y[:, :t_].reshape(1, b_ * t_, D_INNER)
    x_new = mixer_out(y, xf, gate, w_out.astype(bf16), ln_g, ln_b).reshape(b_, t_, d_)
    return x_new, new_conv, new_ssm


def _layer_norm(x, g, b):
    mu = x.mean(-1, keepdims=True)
    var = jnp.square(x - mu).mean(-1, keepdims=True)
    return (x - mu) * lax.rsqrt(var + LN_EPS) * g + b


def _rope(x, pos):
    half = HEAD_DIM // 2
    inv = ROPE_THETA ** (-jnp.arange(half, dtype=f32) / half)
    ang = pos.astype(f32)[:, None] * inv
    cos, sin = jnp.cos(ang)[:, None, :], jnp.sin(ang)[:, None, :]
    x1, x2 = x[..., :half], x[..., half:]
    return jnp.concatenate([x1 * cos - x2 * sin, x2 * cos + x1 * sin], -1)


def _masked_softmax(s, mask):
    s = jnp.where(mask, s, -jnp.inf)
    m = jnp.max(s, axis=-1, keepdims=True)
    e = jnp.exp(s - jnp.where(jnp.isfinite(m), m, 0.0))
    return e / jnp.maximum(jnp.sum(e, axis=-1, keepdims=True), 1e-30)


def _compress(kv, pe, w1, b1, w2, b2):
    b_, l_, g_, d_ = kv.shape
    n_sub = CMP_BLK // CMP_STRIDE
    n_c = (l_ - CMP_BLK) // CMP_STRIDE + 1
    n_ch = n_c + n_sub - 1
    ch = kv[:, :n_ch * CMP_STRIDE].reshape(b_, n_ch, CMP_STRIDE, g_, d_)
    ch = jnp.moveaxis(ch, 3, 2).reshape(b_, n_ch, g_, CMP_STRIDE * d_)
    w1s = w1.reshape(n_sub, CMP_STRIDE * d_, CMP_HIDDEN)
    proj = jnp.einsum('bngf,jfh->jbngh', ch, w1s)
    pre = sum(proj[j][:, j:j + n_c] for j in range(n_sub)) + pe.reshape(-1) @ w1 + b1
    return jax.nn.gelu(pre) @ w2 + b2


def _compressed_kv(k_raw, v_raw, pe, w1, b1, w2, b2):
    kc = _compress(k_raw, pe[0], w1[0], b1[0], w2[0], b2[0])
    vc = _compress(v_raw, pe[1], w1[1], b1[1], w2[1], b2[1])
    cend = jnp.arange(kc.shape[1]) * CMP_STRIDE + CMP_BLK - 1
    return _rope(kc, cend), vc, cend


def _cmp_to_sel(n_c, n_sel):
    cs = jnp.arange(n_c)[:, None] * CMP_STRIDE
    ss = jnp.arange(n_sel)[None, :] * SEL_BLK
    ov = jnp.minimum(cs + CMP_BLK, ss + SEL_BLK) - jnp.maximum(cs, ss)
    return jnp.clip(ov, 0, None).astype(f32) / CMP_BLK


def _select_blocks(p_cmp, qpos, n_sel):
    imp = jnp.einsum('bqgc,cs->bqgs', p_cmp, _cmp_to_sel(p_cmp.shape[-1], n_sel))
    j = jnp.arange(n_sel)[None, :]
    cur = (qpos // SEL_BLK)[:, None]
    valid = (j * SEL_BLK <= qpos[:, None])[:, None, :]
    forced = ((j == 0) | (j == cur) | (j == cur - 1))[:, None, :]
    score = jnp.where(valid, jnp.where(forced, jnp.inf, imp), -jnp.inf)
    top, idx = lax.top_k(score, min(TOPK, n_sel))
    return idx, top > -jnp.inf


def _nsa_core(q, qpos, kc, vc, cend, n_sel, fetch_sel, kw, vw, kwpos, gates):
    qf = q * HEAD_DIM ** -0.5
    s = jnp.einsum('bqgrd,bcgd->bqgrc', qf, kc)
    pc = _masked_softmax(s, (cend[None, :] <= qpos[:, None])[None, :, None, None, :])
    o_c = jnp.einsum('bqgrc,bcgd->bqgrd', pc, vc)
    idx, ok = _select_blocks(pc.sum(3), qpos, n_sel)
    ks, vs = fetch_sel(idx)
    b_, q_, g_, k_ = idx.shape
    kpos = idx[..., None] * SEL_BLK + jnp.arange(SEL_BLK)
    ms = (ok[..., None] & (kpos <= qpos[None, :, None, None, None])).reshape(b_, q_, g_, 1, k_ * SEL_BLK)
    ks = ks.reshape(b_, q_, g_, k_ * SEL_BLK, HEAD_DIM)
    vs = vs.reshape(b_, q_, g_, k_ * SEL_BLK, HEAD_DIM)
    ps = _masked_softmax(jnp.einsum('bqgrd,bqgkd->bqgrk', qf, ks), ms)
    o_s = jnp.einsum('bqgrk,bqgkd->bqgrd', ps, vs)
    dpos = qpos[:, None] - kwpos[None, :]
    mw = ((dpos >= 0) & (dpos < WINDOW) & (kwpos >= 0)[None, :])[None, :, None, None, :]
    pw = _masked_softmax(jnp.einsum('bqgrd,bkgd->bqgrk', qf, kw), mw)
    o_w = jnp.einsum('bqgrk,bkgd->bqgrd', pw, vw)
    return gates[..., 0:1] * o_c + gates[..., 1:2] * o_s + gates[..., 2:3] * o_w


def _nsa_project(h, w_in, pos):
    b_, t_, _ = h.shape
    qd, kd = N_HEADS * HEAD_DIM, N_KV * HEAD_DIM
    proj = h @ w_in
    q = _rope(proj[..., :qd].reshape(b_, t_, N_HEADS, HEAD_DIM), pos).reshape(b_, t_, N_KV, Q_PER_KV, HEAD_DIM)
    kv = proj[..., qd:qd + 6 * kd].reshape(b_, t_, 6, N_KV, HEAD_DIM)
    rows = jnp.stack([kv[:, :, 0], kv[:, :, 1], _rope(kv[:, :, 2], pos), kv[:, :, 3]], axis=2)
    win = jnp.stack([_rope(kv[:, :, 4], pos), kv[:, :, 5]], axis=2)
    gates = jax.nn.sigmoid(proj[..., qd + 6 * kd:]).reshape(b_, t_, N_KV, Q_PER_KV, 3)
    return q, rows, win, gates


def _nsa_prompt(h, w_in, w_o, pe, cw1, cb1, cw2, cb2):
    b_, t_, _ = h.shape
    q, rows, win, gates = _nsa_project(h, w_in, jnp.arange(t_))
    kc, vc, cend = _compressed_kv(rows[:, :, 0], rows[:, :, 1], pe, cw1, cb1, cw2, cb2)
    n_sel = t_ // SEL_BLK
    k_blk = rows[:, :, 2].reshape(b_, n_sel, SEL_BLK, N_KV, HEAD_DIM)
    v_blk = rows[:, :, 3].reshape(b_, n_sel, SEL_BLK, N_KV, HEAD_DIM)
    bi = jnp.arange(b_)[:, None, None, None]
    gi = jnp.arange(N_KV)[None, None, :, None]

    def fetch(idx):
        return k_blk[bi, idx, :, gi], v_blk[bi, idx, :, gi]

    w_pad = jnp.pad(win, ((0, 0), (WINDOW, 0), (0, 0), (0, 0), (0, 0)))

    def block(i):
        s0 = i * Q_BLOCK
        qb = lax.dynamic_slice_in_dim(q, s0, Q_BLOCK, axis=1)
        gb = lax.dynamic_slice_in_dim(gates, s0, Q_BLOCK, axis=1)
        wb = lax.dynamic_slice_in_dim(w_pad, s0, Q_BLOCK + WINDOW, axis=1)
        qpos = s0 + jnp.arange(Q_BLOCK)
        kwpos = s0 - WINDOW + jnp.arange(Q_BLOCK + WINDOW)
        return _nsa_core(qb, qpos, kc, vc, cend, n_sel, fetch, wb[:, :, 0], wb[:, :, 1], kwpos, gb)

    o = lax.map(block, jnp.arange(t_ // Q_BLOCK))
    o = jnp.moveaxis(o, 0, 1).reshape(b_, t_, N_HEADS * HEAD_DIM)
    return o @ w_o, (rows, win[:, t_ - min(WINDOW, t_):])


def _nsa_sample(h, past_len, cache_kv, page_table, li, win_buf, w_in, w_o, pe, cw1, cb1, cw2, cb2):
    b_, s_, _ = h.shape
    pos = past_len + jnp.arange(s_)
    q, rows, win, gates = _nsa_project(h, w_in, pos)
    cmp_pages = cache_kv[:, li, :, :2].reshape(cache_kv.shape[0], -1)
    past = jnp.take(cmp_pages, page_table.reshape(-1), axis=0).reshape(b_, -1, 2, N_KV, HEAD_DIM)
    k_raw = jnp.concatenate([past[:, :, 0], rows[:, :, 0]], axis=1)
    v_raw = jnp.concatenate([past[:, :, 1], rows[:, :, 1]], axis=1)
    kc, vc, cend = _compressed_kv(k_raw, v_raw, pe, cw1, cb1, cw2, cb2)
    n_past_blk = past_len // SEL_BLK
    n_new_blk = -(-s_ // SEL_BLK)
    n_sel = n_past_blk + n_new_blk
    bpp = PAGE_SIZE // SEL_BLK
    new_sel = jnp.pad(rows[:, :, 2:4], ((0, 0), (0, n_new_blk * SEL_BLK - s_), (0, 0), (0, 0), (0, 0)))
    new_k = new_sel[:, :, 0].reshape(b_, n_new_blk, SEL_BLK, N_KV, HEAD_DIM)
    new_v = new_sel[:, :, 1].reshape(b_, n_new_blk, SEL_BLK, N_KV, HEAD_DIM)
    bi = jnp.arange(b_)[:, None, None, None]
    gi = jnp.arange(N_KV)[None, None, :, None]

    def fetch(idx):
        jp = jnp.minimum(idx, n_past_blk - 1)
        phys = page_table[bi, jp // bpp][..., None]
        rr = (jp % bpp)[..., None] * SEL_BLK + jnp.arange(SEL_BLK)
        pk = cache_kv[phys, li, rr, 2, gi[..., None]]
        pv = cache_kv[phys, li, rr, 3, gi[..., None]]
        jn = jnp.clip(idx - n_past_blk, 0, n_new_blk - 1)
        is_past = (idx < n_past_blk)[..., None, None]
        return (jnp.where(is_past, pk, new_k[bi, jn, :, gi]), jnp.where(is_past, pv, new_v[bi, jn, :, gi]))

    w_buf = win_buf.shape[1]
    w_all = jnp.concatenate([win_buf, win], axis=1)
    kwpos = past_len - w_buf + jnp.arange(w_buf + s_)
    o = _nsa_core(q, pos, kc, vc, cend, n_sel, fetch, w_all[:, :, 0], w_all[:, :, 1], kwpos, gates)
    o = o.reshape(b_, s_, N_HEADS * HEAD_DIM)
    return o @ w_o, (rows, w_all[:, -w_buf:])


def _segsum(a):
    t_ = a.shape[-1]
    ii = jnp.arange(t_)
    x = jnp.where(ii[:, None] > ii[None, :], jnp.broadcast_to(a[..., :, None], a.shape + (t_,)), 0.0)
    return jnp.where(ii[:, None] >= ii[None, :], jnp.cumsum(x, axis=-2), -jnp.inf)


def _ssd_scan(x, dt, a_head, bm, cm, h0):
    b_, t_, g_, r_, p_ = x.shape
    n_ = bm.shape[-1]
    lc = min(SSM_CHUNK, t_)
    nc = -(-t_ // lc)
    pad = nc * lc - t_

    def padt(z):
        return jnp.pad(z, [(0, 0), (0, pad)] + [(0, 0)] * (z.ndim - 2))

    xd = padt(x * dt[..., None]).reshape(b_, nc, lc, g_, r_, p_)
    a = jnp.moveaxis(padt(dt * a_head).reshape(b_, nc, lc, g_, r_), (1, 2), (3, 4))
    bc = padt(bm).reshape(b_, nc, lc, g_, n_)
    cc = padt(cm).reshape(b_, nc, lc, g_, n_)
    a_cs = jnp.cumsum(a, axis=-1)
    lmat = jnp.exp(_segsum(a))
    cb = jnp.einsum('bclgn,bcsgn->bgcls', cc, bc)
    y_diag = jnp.einsum('bgrcls,bcsgrp->bclgrp', cb[:, :, None] * lmat, xd)
    decay = jnp.exp(a_cs[..., -1:] - a_cs)
    states = jnp.einsum('bclgn,bgrcl,bclgrp->bcgrpn', bc, decay, xd)
    states = jnp.concatenate([h0[:, None], states], axis=1)
    chunk_a = jnp.pad(a_cs[..., -1], [(0, 0)] * 3 + [(1, 0)])
    states = jnp.einsum('bgrzc,bcgrpn->bzgrpn', jnp.exp(_segsum(chunk_a)), states)
    y_off = jnp.einsum('bclgn,bcgrpn,bgrcl->bclgrp', cc, states[:, :-1], jnp.exp(a_cs))
    y = (y_diag + y_off).reshape(b_, nc * lc, g_, r_, p_)[:, :t_]
    return y, states[:, -1]


def _ssd_mixer(h, conv_state, ssm_state, w_in, conv_w, conv_b, dt_bias, a_log, d_skip, norm_g, w_out):
    b_, t_, _ = h.shape
    hpg = SSM_HEADS // SSM_GROUPS
    gn = SSM_GROUPS * D_STATE
    proj = h @ w_in
    z = proj[..., :D_INNER]
    xbc = proj[..., D_INNER:D_INNER + CONV_DIM]
    dt_raw = proj[..., D_INNER + CONV_DIM:]
    full = jnp.concatenate([conv_state, xbc], axis=1)
    xbc = jax.nn.silu(sum(full[:, k:k + t_] * conv_w[k] for k in range(CONV_W)) + conv_b)
    xs = xbc[..., :D_INNER].reshape(b_, t_, SSM_GROUPS, hpg, SSM_HEAD_DIM)
    bm = xbc[..., D_INNER:D_INNER + gn].reshape(b_, t_, SSM_GROUPS, D_STATE)
    cm = xbc[..., D_INNER + gn:].reshape(b_, t_, SSM_GROUPS, D_STATE)
    dt = jax.nn.softplus(dt_raw + dt_bias).reshape(b_, t_, SSM_GROUPS, hpg)
    a_head = -jnp.exp(a_log).reshape(SSM_GROUPS, hpg)
    h0 = ssm_state.reshape(b_, SSM_GROUPS, hpg, SSM_HEAD_DIM, D_STATE)
    y, h_t = _ssd_scan(xs, dt, a_head, bm, cm, h0)
    y = y + d_skip.reshape(SSM_GROUPS, hpg, 1) * xs
    y = y.reshape(b_, t_, SSM_GROUPS, D_INNER // SSM_GROUPS) * jax.nn.silu(z).reshape(b_, t_, SSM_GROUPS, D_INNER // SSM_GROUPS)
    y = y * lax.rsqrt(jnp.mean(y * y, axis=-1, keepdims=True) + LN_EPS)
    y = y.reshape(b_, t_, D_INNER) * norm_g
    new_ssm = h_t.reshape(b_, SSM_HEADS, SSM_HEAD_DIM, D_STATE)
    return y @ w_out, (full[:, t_:], new_ssm)


def _trunk_layer(x, c, i, mixer, ada_w, ada_b, ln_g, ln_b, ffn_a, ffn_b, fused_mixer=False):
    b_, t_, d_ = x.shape
    m = (jax.nn.silu(c) @ ada_w[i] + ada_b[i]).reshape(b_, 3, 3, 1, D_MODEL)

    def ffn(v, s, w):
        mods = [m[:, s, k] for k in range(3)]
        if t_ < 8:
            mods = [jnp.broadcast_to(z, (b_, t_, d_)).reshape(1, b_ * t_, d_) for z in mods]
            return ffn_sublayer(v.reshape(1, b_ * t_, d_), *mods, *w, ln_g[i, s], ln_b[i, s]).reshape(b_, t_, d_)
        return ffn_sublayer(v, *mods, *w, ln_g[i, s], ln_b[i, s])

    x = ffn(x, 0, ffn_a)
    if fused_mixer:
        x, st = mixer(x, m[:, 1, 0], m[:, 1, 1], m[:, 1, 2], ln_g[i, 1], ln_b[i, 1])
    else:
        y, st = mixer(x * (1 + m[:, 1, 1]) + m[:, 1, 0])
        x = _layer_norm(ALPHA * x + (1 + m[:, 1, 2]) * y, ln_g[i, 1], ln_b[i, 1])
    x = ffn(x, 2, ffn_b)
    return x, st


def kernel(x_prompt, x_sample, cache_kv, cache_win, state_conv, state_ssm, page_table, c_prompt, c_sample, ada_w, ada_b, ln_g, ln_b, ffn_w1, ffn_w3, ffn_w2, nsa_w_in, nsa_w_o, nsa_cmp_pe, nsa_cmp_w1, nsa_cmp_b1, nsa_cmp_w2, nsa_cmp_b2, ssm_w_in, ssm_conv_w, ssm_conv_b, ssm_dt_bias, ssm_a_log, ssm_d, ssm_norm_g, ssm_w_out):
    past_len = page_table.shape[1] * PAGE_SIZE
    xp, xs = x_prompt, x_sample
    kv_p, kv_s, win_p, win_s, conv_p, conv_s, ssm_p, ssm_s = [], [], [], [], [], [], [], []
    for i in range(DEPTH):
        j = i // N_MIXERS
        ffn_a = _chunk_ffn_weights(ffn_w1[i, 0], ffn_w3[i, 0], ffn_w2[i, 0])
        ffn_b = _chunk_ffn_weights(ffn_w1[i, 1], ffn_w3[i, 1], ffn_w2[i, 1])
        lw = (ada_w, ada_b, ln_g, ln_b, ffn_a, ffn_b)
        if i % N_MIXERS == 0:
            nw = (nsa_w_in[j], nsa_w_o[j], nsa_cmp_pe[j], nsa_cmp_w1[j], nsa_cmp_b1[j], nsa_cmp_w2[j], nsa_cmp_b2[j])
            def prompt_mixer(x, sh, sc, gt, lg, lb):
                x_new, rows, win = nsa_prompt_layer(x, sh, sc, gt, *nw, lg, lb)
                return x_new, (rows, win[:, x.shape[1] - min(WINDOW, x.shape[1]):])

            xp, (r_p, w_p) = _trunk_layer(xp, c_prompt, i, prompt_mixer, *lw, fused_mixer=True)
            xs, (r_s, w_s) = _trunk_layer(xs, c_sample, i, lambda h: _nsa_sample(h, past_len, cache_kv, page_table, j, cache_win[:, j], *nw), *lw)
            kv_p.append(r_p); kv_s.append(r_s); win_p.append(w_p); win_s.append(w_s)
        else:
            sw = (ssm_w_in[j], ssm_conv_w[j], ssm_conv_b[j], ssm_dt_bias[j], ssm_a_log[j], ssm_d[j], ssm_norm_g[j], ssm_w_out[j])
            zc = jnp.zeros((xp.shape[0], CONV_W - 1, CONV_DIM), f32)
            zs = jnp.zeros((xp.shape[0], SSM_HEADS, SSM_HEAD_DIM, D_STATE), f32)
            def ssd_mixer(conv0, ssm0):
                def run(x, sh, sc, gt, lg, lb):
                    x_new, cv, st = ssd_layer(x, sh, sc, gt, conv0, ssm0, *sw, lg, lb)
                    return x_new, (cv, st)
                return run

            xp, (cv_p, st_p) = _trunk_layer(xp, c_prompt, i, ssd_mixer(zc, zs), *lw, fused_mixer=True)
            xs, (cv_s, st_s) = _trunk_layer(xs, c_sample, i, ssd_mixer(state_conv[:, j], state_ssm[:, j]), *lw, fused_mixer=True)
            conv_p.append(cv_p); conv_s.append(cv_s); ssm_p.append(st_p); ssm_s.append(st_s)
    return (xp, xs, jnp.stack(kv_p, 1), jnp.stack(kv_s, 1), jnp.stack(win_p, 1), jnp.stack(win_s, 1),
            jnp.stack(conv_p, 1), jnp.stack(conv_s, 1), jnp.stack(ssm_p, 1), jnp.stack(ssm_s, 1))
```

```python
import functools
import math

import jax
import jax.numpy as jnp
from jax import lax
from jax.experimental import pallas as pl
from jax.experimental.pallas import tpu as pltpu

f32 = jnp.float32
bf16 = jnp.bfloat16

D_MODEL = 1024
DEPTH = 4
PAGE_SIZE = 128
N_HEADS = 16
HEAD_DIM = 64
N_KV = 4
Q_PER_KV = N_HEADS // N_KV
CMP_BLK = 32
CMP_STRIDE = 16
CMP_HIDDEN = 2 * HEAD_DIM
SEL_BLK = 64
TOPK = 16
WINDOW = 512
Q_BLOCK = 128
ROPE_THETA = 10000.0
D_INNER = 2 * D_MODEL
SSM_HEAD_DIM = 64
SSM_HEADS = D_INNER // SSM_HEAD_DIM
SSM_GROUPS = 4
D_STATE = 128
CONV_W = 4
CONV_DIM = D_INNER + 2 * SSM_GROUPS * D_STATE
SSM_CHUNK = 128
D_FF = 256 * ((8 * D_MODEL // 3 + 255) // 256)
N_MIXERS = 2
ALPHA = (2 * DEPTH) ** 0.25
N_ADA = 9
LN_EPS = 1e-5

V7X_VMEM_LIMIT_BYTES = 56 * 1024 * 1024
FF_CHUNK = 256
FFN_ROWS = 512


def _ffn_kernel(x_ref, sh_ref, sc_ref, gt_ref, w1_ref, w3_ref, w2_ref, lg_ref, lb_ref, o_ref, acc_ref):
    x = x_ref[0]
    hb = (x * (1.0 + sc_ref[0]) + sh_ref[0]).astype(bf16)
    acc_ref[...] = jnp.zeros_like(acc_ref)

    def chunk(c, carry):
        a = jnp.dot(hb, w1_ref[c], preferred_element_type=f32)
        b = jnp.dot(hb, w3_ref[c], preferred_element_type=f32)
        g = (a * jax.nn.sigmoid(a) * b).astype(bf16)
        acc_ref[...] += jnp.dot(g, w2_ref[c], preferred_element_type=f32)
        return carry

    lax.fori_loop(0, w1_ref.shape[0], chunk, 0)
    y = ALPHA * x + (1.0 + gt_ref[0]) * (0.5 * acc_ref[...])
    mu = jnp.mean(y, axis=-1, keepdims=True)
    yc = y - mu
    var = jnp.mean(yc * yc, axis=-1, keepdims=True)
    o_ref[0] = yc * lax.rsqrt(var + LN_EPS) * lg_ref[...] + lb_ref[...]


def _mod_spec(m, rows):
    if m.shape[1] == 1:
        return pl.BlockSpec((1, 1, m.shape[2]), lambda b, t: (b, 0, 0))
    return pl.BlockSpec((1, rows, m.shape[2]), lambda b, t: (b, t, 0))


def ffn_sublayer(x, shift, scale, gate, w1c, w3c, w2c, ln_g, ln_b):
    b_, t_, d_ = x.shape
    rows = min(FFN_ROWS, t_)
    mod_spec = _mod_spec(shift, rows)
    const3 = lambda b, t: (0, 0, 0)
    wspec = lambda w: pl.BlockSpec(w.shape, const3, pipeline_mode=pl.Buffered(1))
    vec = pl.BlockSpec((1, d_), lambda b, t: (0, 0))
    return pl.pallas_call(
        _ffn_kernel,
        grid=(b_, t_ // rows),
        in_specs=[pl.BlockSpec((1, rows, d_), lambda b, t: (b, t, 0)), mod_spec, mod_spec, mod_spec,
                  wspec(w1c), wspec(w3c), wspec(w2c), vec, vec],
        out_specs=pl.BlockSpec((1, rows, d_), lambda b, t: (b, t, 0)),
        out_shape=jax.ShapeDtypeStruct(x.shape, f32),
        scratch_shapes=[pltpu.VMEM((rows, d_), f32)],
        compiler_params=pltpu.CompilerParams(dimension_semantics=("arbitrary", "arbitrary"),
                                             vmem_limit_bytes=V7X_VMEM_LIMIT_BYTES),
        name="ffn_sublayer",
    )(x, shift, scale, gate, w1c, w3c, w2c, ln_g.reshape(1, d_), ln_b.reshape(1, d_))


def _chunk_ffn_weights(w1, w3, w2):
    n = D_FF // FF_CHUNK
    w1c = w1.astype(bf16).reshape(D_MODEL, n, FF_CHUNK).transpose(1, 0, 2)
    w3c = w3.astype(bf16).reshape(D_MODEL, n, FF_CHUNK).transpose(1, 0, 2)
    w2c = w2.astype(bf16).reshape(n, FF_CHUNK, D_MODEL)
    return w1c, w3c, w2c


NEG = -1e30
PROJ_ROWS = 512
SEL_TK = 512
KD = N_KV * HEAD_DIM
_C_Q, _C_QR, _C_CMP, _C_SEL, _C_WIN, _C_G, _C_END = 0, 1024, 2048, 2560, 3328, 4096, 4224
N_GATE = 3 * N_HEADS


def _rot_cols(w):
    w4 = w.reshape(w.shape[0], -1, 2, HEAD_DIM // 2)
    return jnp.stack([-w4[:, :, 1], w4[:, :, 0]], axis=2).reshape(w.shape)


def _nsa_proj_weights(w_in):
    qd = N_HEADS * HEAD_DIM
    wq = w_in[:, :qd] * HEAD_DIM ** -0.5
    kc, vc, ks, vs, kw, vw = [w_in[:, qd + i * KD: qd + (i + 1) * KD] for i in range(6)]
    wg = jnp.pad(w_in[:, qd + 6 * KD:], ((0, 0), (0, 128 - N_GATE)))
    return jnp.concatenate([wq, _rot_cols(wq), kc, vc, ks, _rot_cols(ks), vs, kw, _rot_cols(kw), vw, wg], axis=1).astype(bf16)


def _rope_tables(pos, width):
    half = HEAD_DIM // 2
    inv = ROPE_THETA ** (-jnp.arange(half, dtype=f32) / half)
    ang = jnp.tile(pos.astype(f32)[:, None] * inv, (1, width // half))
    return jnp.cos(ang), jnp.sin(ang)


def _nsa_proj_kernel(x_ref, sh_ref, sc_ref, w_ref, cos_ref, sin_ref, q_ref, rows_ref, win_ref, gate_ref, kvb_ref):
    hb = (x_ref[0] * (1.0 + sc_ref[0]) + sh_ref[0]).astype(bf16)
    cos, sin = cos_ref[...], sin_ref[...]

    def mm(lo, hi):
        return jnp.dot(hb, w_ref[:, lo:hi], preferred_element_type=f32)

    def rope(a, b, c):
        return a[:, c * 128:(c + 1) * 128] * cos + b[:, c * 128:(c + 1) * 128] * sin

    qa, qb = mm(_C_Q, _C_QR), mm(_C_QR, _C_CMP)
    for c in range(N_HEADS * HEAD_DIM // 128):
        q_ref[0, :, c * 128:(c + 1) * 128] = rope(qa, qb, c).astype(bf16)
    rows_ref[0, :, 0:2 * KD] = mm(_C_CMP, _C_SEL)
    sel = mm(_C_SEL, _C_WIN)
    win = mm(_C_WIN, _C_G)
    for c in range(KD // 128):
        ks = rope(sel[:, 0:KD], sel[:, KD:2 * KD], c)
        kw = rope(win[:, 0:KD], win[:, KD:2 * KD], c)
        rows_ref[0, :, 2 * KD + c * 128:2 * KD + (c + 1) * 128] = ks
        win_ref[0, :, c * 128:(c + 1) * 128] = kw
        kvb_ref[0, :, c * 128:(c + 1) * 128] = ks.astype(bf16)
        kvb_ref[0, :, 2 * KD + c * 128:2 * KD + (c + 1) * 128] = kw.astype(bf16)
    rows_ref[0, :, 3 * KD:4 * KD] = sel[:, 2 * KD:3 * KD]
    win_ref[0, :, KD:2 * KD] = win[:, 2 * KD:3 * KD]
    kvb_ref[0, :, KD:2 * KD] = sel[:, 2 * KD:3 * KD].astype(bf16)
    kvb_ref[0, :, 3 * KD:4 * KD] = win[:, 2 * KD:3 * KD].astype(bf16)
    gate_ref[0] = jax.nn.sigmoid(mm(_C_G, _C_END))


def nsa_project(x, shift, scale, w_all, cos, sin):
    b_, t_, d_ = x.shape
    rows = min(PROJ_ROWS, t_)
    tile = lambda n: pl.BlockSpec((1, rows, n), lambda b, t: (b, t, 0))
    mod = _mod_spec(shift, rows)
    tab = pl.BlockSpec((rows, 128), lambda b, t: (t, 0))
    return pl.pallas_call(
        _nsa_proj_kernel,
        grid=(b_, t_ // rows),
        in_specs=[tile(d_), mod, mod, pl.BlockSpec(w_all.shape, lambda b, t: (0, 0), pipeline_mode=pl.Buffered(1)), tab, tab],
        out_specs=[tile(4 * KD), tile(4 * KD), tile(2 * KD), tile(128), tile(4 * KD)],
        out_shape=[jax.ShapeDtypeStruct((b_, t_, 4 * KD), bf16), jax.ShapeDtypeStruct((b_, t_, 4 * KD), f32),
                   jax.ShapeDtypeStruct((b_, t_, 2 * KD), f32), jax.ShapeDtypeStruct((b_, t_, 128), f32),
                   jax.ShapeDtypeStruct((b_, t_, 4 * KD), bf16)],
        compiler_params=pltpu.CompilerParams(dimension_semantics=("arbitrary", "arbitrary"),
                                             vmem_limit_bytes=V7X_VMEM_LIMIT_BYTES),
        name="nsa_project",
    )(x, shift, scale, w_all, cos, sin)


def _gelu_tanh(x):
    return 0.5 * x * (1.0 + jnp.tanh(math.sqrt(2.0 / math.pi) * (x + 0.044715 * (x * x * x))))


def _compress_kernel(xa_ref, xb_ref, pe_ref, w1_ref, b1_ref, w2_ref, w2r_ref, b2_ref, b2r_ref, w2t_ref, w2rt_ref,
                     b2t_ref, b2rt_ref, cos_ref, sin_ref, cost_ref, sint_ref, on_ref, ot_ref, *, projected):
    w1 = w1_ref[0]
    half = w1.shape[0] // 2
    bias = jnp.dot(pe_ref[0], w1, preferred_element_type=f32)[0:1] + b1_ref[0]
    if projected:
        pre = xa_ref[0, 0, 0] + xb_ref[0, 0, 0] + bias
    else:
        pre = (jnp.dot(xa_ref[0, 0, 0], w1[:half], preferred_element_type=f32)
               + jnp.dot(xb_ref[0, 0, 0], w1[half:], preferred_element_type=f32) + bias)
    hb = _gelu_tanh(pre).astype(bf16)
    y = jnp.dot(hb, w2_ref[0], preferred_element_type=f32) + b2_ref[0]
    yr = jnp.dot(hb, w2r_ref[0], preferred_element_type=f32) + b2r_ref[0]
    on_ref[0, 0, 0] = y * cos_ref[0] + yr * sin_ref[0]
    nt = (((1,), (1,)), ((), ()))
    yt = lax.dot_general(w2t_ref[0], hb, nt, preferred_element_type=f32) + b2t_ref[0]
    yrt = lax.dot_general(w2rt_ref[0], hb, nt, preferred_element_type=f32) + b2rt_ref[0]
    ot_ref[0, 0, 0] = yt * cost_ref[0] + yrt * sint_ref[0]


def compress_kv(chunks, pe, w1, b1, w2, b2, projected=False):
    b_, _, g_, n_ch, feat = chunks.shape
    if projected:
        feat = CMP_HIDDEN
        first, second = chunks[..., :feat], chunks[..., feat:]
    else:
        first = second = chunks
    chunks = first
    nxt = jnp.concatenate([second[:, :, :, 1:], jnp.zeros_like(second[:, :, :, :1])], axis=3)
    cend = jnp.arange(n_ch) * CMP_STRIDE + CMP_BLK - 1
    cos, sin = _rope_tables(cend, HEAD_DIM)
    cos = jnp.stack([cos, jnp.ones_like(cos)])
    sin = jnp.stack([sin, jnp.zeros_like(sin)])
    w2b = w2.astype(bf16)
    w2r = _rot_cols(w2).astype(bf16)
    b2r = _rot_cols(b2[:, None, :])
    pe8 = jnp.broadcast_to(pe.reshape(2, 1, -1), (2, 8, pe.shape[1] * pe.shape[2])).astype(bf16)
    xspec = pl.BlockSpec((1, 1, 1, n_ch, feat), lambda b, k, g: (b, k, g, 0, 0))
    per_kv = lambda *s: pl.BlockSpec((1,) + s, lambda b, k, g: (k,) + (0,) * len(s))
    return pl.pallas_call(
        functools.partial(_compress_kernel, projected=projected),
        grid=(b_, 2, g_),
        in_specs=[xspec, xspec, per_kv(8, pe8.shape[2]), per_kv(*w1.shape[1:]), per_kv(1, CMP_HIDDEN),
                  per_kv(CMP_HIDDEN, HEAD_DIM), per_kv(CMP_HIDDEN, HEAD_DIM), per_kv(1, HEAD_DIM), per_kv(1, HEAD_DIM),
                  per_kv(HEAD_DIM, CMP_HIDDEN), per_kv(HEAD_DIM, CMP_HIDDEN), per_kv(HEAD_DIM, 1), per_kv(HEAD_DIM, 1),
                  per_kv(n_ch, HEAD_DIM), per_kv(n_ch, HEAD_DIM), per_kv(HEAD_DIM, n_ch), per_kv(HEAD_DIM, n_ch)],
        out_specs=[pl.BlockSpec((1, 1, 1, n_ch, HEAD_DIM), lambda b, k, g: (b, k, g, 0, 0)),
                   pl.BlockSpec((1, 1, 1, HEAD_DIM, n_ch), lambda b, k, g: (b, k, g, 0, 0))],
        out_shape=[jax.ShapeDtypeStruct((b_, 2, g_, n_ch, HEAD_DIM), f32),
                   jax.ShapeDtypeStruct((b_, 2, g_, HEAD_DIM, n_ch), f32)],
        compiler_params=pltpu.CompilerParams(dimension_semantics=("arbitrary",) * 3),
        name="compress_kv",
    )(chunks, nxt, pe8, w1.astype(bf16), b1[:, None, :], w2b, w2r, b2[:, None, :], b2r,
      w2b.transpose(0, 2, 1), w2r.transpose(0, 2, 1), b2[:, :, None], b2r.transpose(0, 2, 1),
      cos, sin, cos.transpose(0, 2, 1), sin.transpose(0, 2, 1))


def _cmp_to_sel_matrix(n_c, n_sel):
    cs = jnp.arange(n_c)[:, None] * CMP_STRIDE
    ss = jnp.arange(n_sel)[None, :] * SEL_BLK
    ov = jnp.minimum(cs + CMP_BLK, ss + SEL_BLK) - jnp.maximum(cs, ss)
    return jnp.clip(ov, 0, None).astype(f32) / CMP_BLK


def _softmax_rows(s, valid):
    m = jnp.max(s, axis=1, keepdims=True)
    e = jnp.where(valid, jnp.exp(s - m), 0.0)
    return e, jnp.sum(e, axis=1, keepdims=True)


def _unselected_blocks(imp_t, qpos, n_sel):
    jj = lax.broadcasted_iota(jnp.int32, imp_t.shape, 0)
    cur = qpos // SEL_BLK
    forced = (jj == 0) | (jj == cur) | (jj == cur - 1)
    sc0 = jnp.where((jj <= cur) & (jj < n_sel), jnp.where(forced, -NEG, imp_t), NEG)
    jjf = jj.astype(f32)

    def pick(_, carry):
        sc, nonsel = carry
        m = jnp.max(sc, axis=0, keepdims=True)
        idx = jnp.min(jnp.where(sc == m, jjf, float(imp_t.shape[0])), axis=0, keepdims=True)
        hit = (jjf == idx) & (m > NEG)
        return jnp.where(hit, NEG, sc), jnp.where(hit, 0.0, nonsel)

    return lax.fori_loop(0, min(TOPK, n_sel), pick, (sc0, jnp.ones(imp_t.shape, f32)))[1]


def _nsa_attn_kernel(q_ref, kct_ref, vc_ref, ks_ref, vs_ref, kw_ref, vw_ref, gate_ref, msel_ref, o_ref,
                     oc_sc, score_sc, nonsel_sc, m_sc, acc_sc, *, n_cmp, n_sel):
    qt = pl.program_id(1)
    s0 = qt * Q_BLOCK
    rq = Q_PER_KV * Q_BLOCK
    row_q = s0 + (lax.broadcasted_iota(jnp.int32, (rq, 1), 0) & (Q_BLOCK - 1))

    cend = lax.broadcasted_iota(jnp.int32, (1, n_cmp), 1) * CMP_STRIDE + (CMP_BLK - 1)
    cvalid = cend <= row_q
    for g in range(N_KV):
        s = jnp.dot(q_ref[0, g, 0], kct_ref[0, g], preferred_element_type=f32)
        e, l = _softmax_rows(jnp.where(cvalid, s, NEG), cvalid)
        p = e * (1.0 / jnp.maximum(l, 1e-30))
        oc_sc[g] = jnp.dot(p.astype(bf16), vc_ref[0, g], preferred_element_type=f32)
        ps = p[0:Q_BLOCK]
        for r in range(1, Q_PER_KV):
            ps = ps + p[r * Q_BLOCK:(r + 1) * Q_BLOCK]
        hi = ps.astype(bf16)
        lo = (ps - hi.astype(f32)).astype(bf16)
        imp = (jnp.dot(hi, msel_ref[...], preferred_element_type=f32)
               + jnp.dot(lo, msel_ref[...], preferred_element_type=f32))
        score_sc[:, g * Q_BLOCK:(g + 1) * Q_BLOCK] = imp.T

    qpos = s0 + (lax.broadcasted_iota(jnp.int32, (n_sel, N_KV * Q_BLOCK), 1) & (Q_BLOCK - 1))
    nonsel = _unselected_blocks(score_sc[...], qpos, n_sel)
    for g in range(N_KV):
        nonsel_sc[g] = nonsel[:, g * Q_BLOCK:(g + 1) * Q_BLOCK].T.astype(bf16)

    n_tiles = (s0 + Q_BLOCK + SEL_TK - 1) // SEL_TK
    key_l = lax.broadcasted_iota(jnp.int32, (1, SEL_TK), 1)
    blk_j = lax.broadcasted_iota(jnp.int32, (n_sel, SEL_TK), 0)
    blk_l = lax.broadcasted_iota(jnp.int32, (n_sel, SEL_TK), 1)
    wkey_l = lax.broadcasted_iota(jnp.int32, (1, Q_BLOCK), 1)
    n_win = WINDOW // Q_BLOCK + 1
    for g in range(N_KV):
        q = q_ref[0, g, 0]
        m_sc[...] = jnp.full_like(m_sc, NEG)
        acc_sc[...] = jnp.zeros_like(acc_sc)

        def sel_tile(kt, causal):
            s = jnp.dot(q, ks_ref[0, g, kt], preferred_element_type=f32)
            expand = jnp.where(blk_j == (kt * SEL_TK + blk_l) // SEL_BLK, NEG, 0.0).astype(bf16)
            bias = jnp.dot(nonsel_sc[g], expand, preferred_element_type=f32)
            s = s + jnp.concatenate([bias] * Q_PER_KV, axis=0)
            if causal:
                s = jnp.where(kt * SEL_TK + key_l <= row_q, s, NEG)
            m_old = m_sc[...]
            m_new = jnp.maximum(m_old, jnp.max(s, axis=1, keepdims=True))
            p = jnp.exp(s - m_new).astype(bf16)
            acc_sc[...] = jnp.exp(m_old - m_new) * acc_sc[...] + jnp.dot(p, vs_ref[0, g, kt], preferred_element_type=f32)
            m_sc[...] = m_new

        def body(kt, carry):
            sel_tile(kt, False)
            return carry

        lax.fori_loop(0, n_tiles - 1, body, 0)
        sel_tile(n_tiles - 1, True)
        acc = acc_sc[...]
        o_s = acc[:, 0:HEAD_DIM] * (1.0 / jnp.maximum(acc[:, HEAD_DIM:HEAD_DIM + 1], 1e-30))

        s_parts, w_tiles = [], []
        for i in range(n_win):
            kt = qt - (n_win - 1) + i
            ktc = jnp.maximum(kt, 0)
            kpos = kt * Q_BLOCK + wkey_l
            dpos = row_q - kpos
            ok = (kpos >= 0) & (dpos >= 0) & (dpos < WINDOW)
            s_parts.append(jnp.where(ok, jnp.dot(q, kw_ref[0, g, ktc], preferred_element_type=f32), NEG))
            w_tiles.append(ktc)
        s_w = jnp.concatenate(s_parts, axis=1)
        e_w = jnp.exp(s_w - jnp.max(s_w, axis=1, keepdims=True)).astype(bf16)
        acc_w = jnp.dot(e_w[:, 0:Q_BLOCK], vw_ref[0, g, w_tiles[0]], preferred_element_type=f32)
        for i in range(1, n_win):
            acc_w = acc_w + jnp.dot(e_w[:, i * Q_BLOCK:(i + 1) * Q_BLOCK], vw_ref[0, g, w_tiles[i]], preferred_element_type=f32)
        o_w = acc_w[:, 0:HEAD_DIM] * (1.0 / jnp.maximum(acc_w[:, HEAD_DIM:HEAD_DIM + 1], 1e-30))

        o_c = oc_sc[g]
        gates = gate_ref[0]
        for r in range(Q_PER_KV):
            c = (g * Q_PER_KV + r) * 3
            rows = slice(r * Q_BLOCK, (r + 1) * Q_BLOCK)
            o = (gates[:, c:c + 1] * o_c[rows] + gates[:, c + 1:c + 2] * o_s[rows] + gates[:, c + 2:c + 3] * o_w[rows])
            o_ref[0, g, 0, rows, :] = o.astype(bf16)


def _with_ones(v):
    pad = jnp.zeros(v.shape[:-1] + (128 - v.shape[-1] - 1,), v.dtype)
    return jnp.concatenate([v, jnp.ones(v.shape[:-1] + (1,), v.dtype), pad], axis=-1)


def nsa_attention(q, kvb, gates, kct, vc):
    b_, t_, _ = q.shape
    n_qt, n_st, n_sel = t_ // Q_BLOCK, t_ // SEL_TK, t_ // SEL_BLK
    n_cmp = kct.shape[-1]
    rq = Q_PER_KV * Q_BLOCK
    q5 = q.reshape(b_, n_qt, Q_BLOCK, N_KV, Q_PER_KV, HEAD_DIM).transpose(0, 3, 1, 4, 2, 5).reshape(b_, N_KV, n_qt, rq, HEAD_DIM)

    def tiles(x, tk):
        return x.reshape(b_, t_ // tk, tk, N_KV, HEAD_DIM).transpose(0, 3, 1, 2, 4)

    ks = tiles(kvb[..., 0:KD], SEL_TK).transpose(0, 1, 2, 4, 3)
    vs = _with_ones(tiles(kvb[..., KD:2 * KD], SEL_TK))
    kw = tiles(kvb[..., 2 * KD:3 * KD], Q_BLOCK).transpose(0, 1, 2, 4, 3)
    vw = _with_ones(tiles(kvb[..., 3 * KD:4 * KD], Q_BLOCK))
    msel = _cmp_to_sel_matrix(n_cmp, n_sel).astype(bf16)
    per_b = lambda x: pl.BlockSpec((1,) + x.shape[1:], lambda b, t: (b,) + (0,) * (x.ndim - 1), pipeline_mode=pl.Buffered(1))
    qspec = pl.BlockSpec((1, N_KV, 1, rq, HEAD_DIM), lambda b, t: (b, 0, t, 0, 0))
    o5 = pl.pallas_call(
        functools.partial(_nsa_attn_kernel, n_cmp=n_cmp, n_sel=n_sel),
        grid=(b_, n_qt),
        in_specs=[qspec, per_b(kct), per_b(vc), per_b(ks), per_b(vs), per_b(kw), per_b(vw),
                  pl.BlockSpec((1, Q_BLOCK, 128), lambda b, t: (b, t, 0)),
                  pl.BlockSpec(msel.shape, lambda b, t: (0, 0))],
        out_specs=qspec,
        out_shape=jax.ShapeDtypeStruct(q5.shape, bf16),
        scratch_shapes=[pltpu.VMEM((N_KV, rq, HEAD_DIM), f32), pltpu.VMEM((n_sel, N_KV * Q_BLOCK), f32),
                        pltpu.VMEM((N_KV, Q_BLOCK, n_sel), bf16), pltpu.VMEM((rq, 1), f32), pltpu.VMEM((rq, 128), f32)],
        compiler_params=pltpu.CompilerParams(dimension_semantics=("arbitrary", "arbitrary"),
                                             vmem_limit_bytes=V7X_VMEM_LIMIT_BYTES),
        name="nsa_attention",
    )(q5, kct, vc, ks, vs, kw, vw, gates, msel)
    return o5.reshape(b_, N_KV, n_qt, Q_PER_KV, Q_BLOCK, HEAD_DIM).transpose(0, 2, 4, 1, 3, 5).reshape(b_, t_, N_HEADS * HEAD_DIM)


def _mixer_out_kernel(y_ref, x_ref, gt_ref, w_ref, lg_ref, lb_ref, o_ref):
    f = jnp.dot(y_ref[0], w_ref[...], preferred_element_type=f32)
    y = ALPHA * x_ref[0] + (1.0 + gt_ref[0]) * f
    mu = jnp.mean(y, axis=-1, keepdims=True)
    yc = y - mu
    var = jnp.mean(yc * yc, axis=-1, keepdims=True)
    o_ref[0] = yc * lax.rsqrt(var + LN_EPS) * lg_ref[...] + lb_ref[...]


def mixer_out(y, x, gate, w, ln_g, ln_b):
    b_, t_, d_ = x.shape
    rows = min(PROJ_ROWS, t_)
    vec = pl.BlockSpec((1, d_), lambda b, t: (0, 0))
    return pl.pallas_call(
        _mixer_out_kernel,
        grid=(b_, t_ // rows),
        in_specs=[pl.BlockSpec((1, rows, y.shape[2]), lambda b, t: (b, t, 0)), pl.BlockSpec((1, rows, d_), lambda b, t: (b, t, 0)),
                  _mod_spec(gate, rows), pl.BlockSpec(w.shape, lambda b, t: (0, 0)), vec, vec],
        out_specs=pl.BlockSpec((1, rows, d_), lambda b, t: (b, t, 0)),
        out_shape=jax.ShapeDtypeStruct(x.shape, f32),
        compiler_params=pltpu.CompilerParams(dimension_semantics=("arbitrary", "arbitrary"),
                                             vmem_limit_bytes=V7X_VMEM_LIMIT_BYTES),
        name="mixer_out",
    )(y, x, gate, w, ln_g.reshape(1, d_), ln_b.reshape(1, d_))


def nsa_prompt_layer(x, shift, scale, gate, w_in, w_o, pe, cw1, cb1, cw2, cb2, ln_g, ln_b):
    b_, t_, _ = x.shape
    cos, sin = _rope_tables(jnp.arange(t_), 128)
    q, rows, win, gates, kvb = nsa_project(x, shift, scale, _nsa_proj_weights(w_in), cos, sin)
    n_ch = t_ // CMP_STRIDE
    chunks = rows[..., 0:2 * KD].astype(bf16).reshape(b_, n_ch, CMP_STRIDE, 2, N_KV, HEAD_DIM)
    chunks = chunks.transpose(0, 3, 4, 1, 2, 5).reshape(b_, 2, N_KV, n_ch, CMP_STRIDE * HEAD_DIM)
    cmp_n, cmp_t = compress_kv(chunks, pe, cw1, cb1, cw2, cb2)
    o = nsa_attention(q, kvb, gates, cmp_t[:, 0].astype(bf16), cmp_n[:, 1].astype(bf16))
    x_new = mixer_out(o, x, gate, w_o.astype(bf16), ln_g, ln_b)
    return x_new, rows.reshape(b_, t_, 4, N_KV, HEAD_DIM), win.reshape(b_, t_, 2, N_KV, HEAD_DIM)


PAGES_PER_STEP = 8
S_PAD = 8
CHUNKS_PER_PAGE = PAGE_SIZE // CMP_STRIDE
WIN_KEYS_PAD = 128


def _page_specs(block, tail):
    def spec(i):
        return pl.BlockSpec(block, lambda b, pg, pt: (pt[b, pg * PAGES_PER_STEP + i],) + tail)
    return [spec(i) for i in range(PAGES_PER_STEP)]


def _cmp_pages_kernel(pt_ref, *refs):
    x_refs, w_ref, o_ref = refs[:PAGES_PER_STEP], refs[PAGES_PER_STEP], refs[PAGES_PER_STEP + 1]
    for kv in range(2):
        x = jnp.concatenate([r[0, kv] for r in x_refs], axis=0)
        o_ref[0, 0, kv] = jnp.dot(x, w_ref[kv], preferred_element_type=f32)


def cmp_pages(cmp_src, page_table, w01):
    b_, n_pages = page_table.shape
    n_pg = n_pages // PAGES_PER_STEP
    rpp = cmp_src.shape[2]
    out = pl.pallas_call(
        _cmp_pages_kernel,
        grid_spec=pltpu.PrefetchScalarGridSpec(
            num_scalar_prefetch=1, grid=(b_, n_pg),
            in_specs=_page_specs((1,) + cmp_src.shape[1:], (0, 0, 0)) + [pl.BlockSpec(w01.shape, lambda b, pg, pt: (0, 0, 0))],
            out_specs=pl.BlockSpec((1, 1, 2, PAGES_PER_STEP * rpp, w01.shape[2]), lambda b, pg, pt: (b, pg, 0, 0, 0))),
        out_shape=jax.ShapeDtypeStruct((b_, n_pg, 2, PAGES_PER_STEP * rpp, w01.shape[2]), f32),
        compiler_params=pltpu.CompilerParams(dimension_semantics=("arbitrary", "arbitrary")),
        name="cmp_pages",
    )(page_table, *([cmp_src] * PAGES_PER_STEP), w01)
    out = out.reshape(b_, n_pg, 2, PAGES_PER_STEP, N_KV, CHUNKS_PER_PAGE, w01.shape[2])
    return out.transpose(0, 2, 4, 1, 3, 5, 6).reshape(b_, 2, N_KV, n_pages * CHUNKS_PER_PAGE, w01.shape[2])


def _sample_cmp_win_kernel(q_ref, kct_ref, vc_ref, kw_ref, vw_ref, msel_ref, oc_ref, ow_ref, imp_ref, *, past, w_buf, n_wk):
    rows = Q_PER_KV * S_PAD
    qpos = past + (lax.broadcasted_iota(jnp.int32, (rows, 1), 0) & (S_PAD - 1))
    cend = lax.broadcasted_iota(jnp.int32, (1, kct_ref.shape[3]), 1) * CMP_STRIDE + (CMP_BLK - 1)
    cvalid = cend <= qpos
    widx = lax.broadcasted_iota(jnp.int32, (1, kw_ref.shape[3]), 1)
    kwpos = past - w_buf + widx
    dpos = qpos - kwpos
    wok = (dpos >= 0) & (dpos < WINDOW) & (kwpos >= 0) & (widx < n_wk)
    for g in range(N_KV):
        q = q_ref[0, g]
        s = jnp.dot(q, kct_ref[0, g], preferred_element_type=f32)
        e, l = _softmax_rows(jnp.where(cvalid, s, NEG), cvalid)
        p = e * (1.0 / jnp.maximum(l, 1e-30))
        oc_ref[0, g] = jnp.dot(p.astype(bf16), vc_ref[0, g], preferred_element_type=f32)
        hi = p.astype(bf16)
        lo = (p - hi.astype(f32)).astype(bf16)
        imp = jnp.dot(hi, msel_ref[...], preferred_element_type=f32) + jnp.dot(lo, msel_ref[...], preferred_element_type=f32)
        tot = imp[0:S_PAD]
        for r in range(1, Q_PER_KV):
            tot = tot + imp[r * S_PAD:(r + 1) * S_PAD]
        imp_ref[0, g] = tot
        sw = jnp.where(wok, jnp.dot(q, kw_ref[0, g], preferred_element_type=f32), NEG)
        ew = jnp.exp(sw - jnp.max(sw, axis=1, keepdims=True)).astype(bf16)
        acc = jnp.dot(ew, vw_ref[0, g], preferred_element_type=f32)
        ow_ref[0, g] = acc[:, 0:HEAD_DIM] * (1.0 / jnp.maximum(acc[:, HEAD_DIM:HEAD_DIM + 1], 1e-30))


def _topk_kernel(imp_ref, o_ref, *, past, n_sel):
    qpos = past + (lax.broadcasted_iota(jnp.int32, imp_ref.shape, 1) & (S_PAD - 1))
    o_ref[...] = _unselected_blocks(imp_ref[...], qpos, n_sel)


def _sample_sel_kernel(pt_ref, *refs, past, n_new):
    page_refs = refs[:PAGES_PER_STEP]
    qbd_ref, nonsel_ref, knew_ref, vnew_ref, os_ref, m_sc, l_sc, acc_sc = refs[PAGES_PER_STEP:]
    pg = pl.program_id(1)
    nt = (((1,), (1,)), ((), ()))

    @pl.when(pg == 0)
    def _():
        m_sc[...] = jnp.full_like(m_sc, NEG)
        l_sc[...] = jnp.zeros_like(l_sc)
        acc_sc[...] = jnp.zeros_like(acc_sc)

    def accumulate(s, v):
        m_old = m_sc[...]
        m_new = jnp.maximum(m_old, jnp.max(s, axis=1, keepdims=True))
        p = jnp.exp(s - m_new)
        alpha = jnp.exp(m_old - m_new)
        l_sc[...] = alpha * l_sc[...] + jnp.sum(p, axis=1, keepdims=True)
        acc_sc[...] = alpha * acc_sc[...] + jnp.dot(p.astype(bf16), v, preferred_element_type=f32)
        m_sc[...] = m_new

    x = jnp.concatenate([r[0, 0] for r in page_refs], axis=0)
    nk = x.shape[0]
    qbd = qbd_ref[0]
    nonsel = nonsel_ref[0]
    s = lax.dot_general(qbd, x[:, 0:KD].astype(bf16), nt, preferred_element_type=f32)
    blk_j = lax.broadcasted_iota(jnp.int32, (nonsel.shape[1], nk), 0)
    key = pg * nk + lax.broadcasted_iota(jnp.int32, (nonsel.shape[1], nk), 1)
    expand = jnp.where(blk_j == key // SEL_BLK, NEG, 0.0).astype(bf16)
    accumulate(s + jnp.dot(nonsel, expand, preferred_element_type=f32), x[:, KD:2 * KD].astype(bf16))

    @pl.when(pg == pl.num_programs(1) - 1)
    def _():
        sn = lax.dot_general(qbd, knew_ref[0], nt, preferred_element_type=f32)
        t_new = lax.broadcasted_iota(jnp.int32, (1, sn.shape[1]), 1)
        s_row = lax.broadcasted_iota(jnp.int32, (sn.shape[0], 1), 0) & (S_PAD - 1)
        new_blk = past // SEL_BLK
        bias_new = nonsel[:, new_blk:new_blk + 1].astype(f32) * NEG
        accumulate(jnp.where((t_new < n_new) & (t_new <= s_row), sn + bias_new, NEG), vnew_ref[0])
        o = acc_sc[...] * (1.0 / jnp.maximum(l_sc[...], 1e-30))
        rows = Q_PER_KV * S_PAD
        for g in range(N_KV):
            os_ref[0, g] = o[g * rows:(g + 1) * rows, g * HEAD_DIM:(g + 1) * HEAD_DIM]


def _gated_out_kernel(oc_ref, os_ref, ow_ref, gc_ref, gs_ref, gw_ref, x_ref, gt_ref, w_ref, lg_ref, lb_ref, o_ref):
    o = gc_ref[0] * oc_ref[0] + gs_ref[0] * os_ref[0] + gw_ref[0] * ow_ref[0]
    f = jnp.dot(o.astype(bf16), w_ref[...], preferred_element_type=f32)
    y = ALPHA * x_ref[0] + (1.0 + gt_ref[0]) * f
    mu = jnp.mean(y, axis=-1, keepdims=True)
    yc = y - mu
    var = jnp.mean(yc * yc, axis=-1, keepdims=True)
    o_ref[0] = yc * lax.rsqrt(var + LN_EPS) * lg_ref[...] + lb_ref[...]


def nsa_sample_layer(x, shift, scale, gate, cache_kv, page_table, li, win_buf, w_in, w_o, pe, cw1, cb1, cw2, cb2, ln_g, ln_b):
    b_, s_, d_ = x.shape
    n_pool, n_pages = cache_kv.shape[0], page_table.shape[1]
    past = n_pages * PAGE_SIZE
    w_buf = win_buf.shape[1]
    n_ch = past // CMP_STRIDE
    n_sel = past // SEL_BLK + 1
    assert past % SEL_BLK == 0 and s_ <= S_PAD and n_pages % PAGES_PER_STEP == 0
    assert (past + s_ - CMP_BLK) // CMP_STRIDE + 2 == n_ch
    rows_n = b_ * s_
    rq = Q_PER_KV * S_PAD
    flat = lambda m: jnp.broadcast_to(m, (b_, s_, d_)).reshape(1, rows_n, d_)
    xf = x.reshape(1, rows_n, d_)
    cos, sin = _rope_tables(past + (jnp.arange(rows_n) % s_), 128)
    q, rows, win, gates, kvb = nsa_project(xf, flat(shift), flat(scale), _nsa_proj_weights(w_in), cos, sin)

    cmp_src = cache_kv[:, li, :, 0:2].astype(bf16).reshape(n_pool, CHUNKS_PER_PAGE, CMP_STRIDE, 2, N_KV, HEAD_DIM)
    cmp_src = cmp_src.transpose(0, 3, 4, 1, 2, 5).reshape(n_pool, 2, N_KV * CHUNKS_PER_PAGE, CMP_STRIDE * HEAD_DIM)
    half = cw1.shape[1] // 2
    w01 = jnp.concatenate([cw1[:, :half], cw1[:, half:]], axis=2).astype(bf16)
    cmp_n, cmp_t = compress_kv(cmp_pages(cmp_src, page_table, w01), pe, cw1, cb1, cw2, cb2, projected=True)

    q32 = jnp.pad(q.reshape(b_, s_, N_KV, Q_PER_KV, HEAD_DIM), ((0, 0), (0, S_PAD - s_), (0, 0), (0, 0), (0, 0)))
    q32 = q32.transpose(0, 2, 3, 1, 4).reshape(b_, N_KV, rq, HEAD_DIM)
    qbd = (q32[:, :, :, None, :] * jnp.eye(N_KV, dtype=bf16)[None, :, None, :, None]).reshape(b_, N_KV * rq, KD)

    w_all = jnp.concatenate([win_buf, win.reshape(b_, s_, 2, N_KV, HEAD_DIM)], axis=1)
    n_wk = w_buf + s_
    wk_pad = -(-n_wk // WIN_KEYS_PAD) * WIN_KEYS_PAD
    w_pad = jnp.pad(w_all, ((0, 0), (0, wk_pad - n_wk), (0, 0), (0, 0), (0, 0))).astype(bf16)
    kw = w_pad[:, :, 0].transpose(0, 2, 3, 1)
    vw = _with_ones(w_pad[:, :, 1].transpose(0, 2, 1, 3))
    n_sel_pad = 2 * 128
    msel = jnp.pad(_cmp_to_sel_matrix(n_ch, n_sel), ((0, 0), (0, n_sel_pad - n_sel))).astype(bf16)
    per_b = lambda a: pl.BlockSpec((1,) + a.shape[1:], lambda b: (b,) + (0,) * (a.ndim - 1))
    kct, vc = cmp_t[:, 0].astype(bf16), cmp_n[:, 1].astype(bf16)
    o_shape = jax.ShapeDtypeStruct((b_, N_KV, rq, HEAD_DIM), f32)
    o_spec = pl.BlockSpec((1, N_KV, rq, HEAD_DIM), lambda b: (b, 0, 0, 0))
    oc, ow, imp = pl.pallas_call(
        functools.partial(_sample_cmp_win_kernel, past=past, w_buf=w_buf, n_wk=n_wk),
        grid=(b_,),
        in_specs=[per_b(q32), per_b(kct), per_b(vc), per_b(kw), per_b(vw), pl.BlockSpec(msel.shape, lambda b: (0, 0))],
        out_specs=[o_spec, o_spec, pl.BlockSpec((1, N_KV, S_PAD, n_sel_pad), lambda b: (b, 0, 0, 0))],
        out_shape=[o_shape, o_shape, jax.ShapeDtypeStruct((b_, N_KV, S_PAD, n_sel_pad), f32)],
        compiler_params=pltpu.CompilerParams(dimension_semantics=("arbitrary",)),
        name="sample_cmp_win",
    )(q32, kct, vc, kw, vw, msel)

    sel_rows = -(-n_sel // 8) * 8
    imp_t = imp[..., :sel_rows].transpose(3, 0, 1, 2).reshape(sel_rows, b_ * N_KV * S_PAD)
    nonsel_t = pl.pallas_call(
        functools.partial(_topk_kernel, past=past, n_sel=n_sel),
        out_shape=jax.ShapeDtypeStruct(imp_t.shape, f32),
        name="sample_topk",
    )(imp_t)
    nonsel = nonsel_t.reshape(sel_rows, b_, N_KV, 1, S_PAD).transpose(1, 2, 3, 4, 0)
    nonsel = jnp.broadcast_to(nonsel, (b_, N_KV, Q_PER_KV, S_PAD, sel_rows)).reshape(b_, N_KV * rq, sel_rows)
    nonsel = jnp.pad(nonsel, ((0, 0), (0, 0), (0, n_sel_pad - sel_rows)), constant_values=1.0).astype(bf16)

    new_pad = lambda a: jnp.pad(a.reshape(b_, s_, KD), ((0, 0), (0, 128 - s_), (0, 0)))
    knew, vnew = new_pad(kvb[..., 0:KD]), new_pad(kvb[..., KD:2 * KD])
    pages = cache_kv.reshape(n_pool, cache_kv.shape[1], PAGE_SIZE, 4 * KD)
    rows_all = N_KV * rq
    per_b2 = lambda a: pl.BlockSpec((1,) + a.shape[1:], lambda b, pg, pt: (b,) + (0,) * (a.ndim - 1))
    o_s = pl.pallas_call(
        functools.partial(_sample_sel_kernel, past=past, n_new=s_),
        grid_spec=pltpu.PrefetchScalarGridSpec(
            num_scalar_prefetch=1, grid=(b_, n_pages // PAGES_PER_STEP),
            in_specs=_page_specs((1, 1, PAGE_SIZE, 2 * KD), (li, 0, 1)) + [per_b2(qbd), per_b2(nonsel), per_b2(knew), per_b2(vnew)],
            out_specs=pl.BlockSpec((1, N_KV, rq, HEAD_DIM), lambda b, pg, pt: (b, 0, 0, 0)),
            scratch_shapes=[pltpu.VMEM((rows_all, 1), f32), pltpu.VMEM((rows_all, 1), f32), pltpu.VMEM((rows_all, KD), f32)]),
        out_shape=o_shape,
        compiler_params=pltpu.CompilerParams(dimension_semantics=("arbitrary", "arbitrary"),
                                             vmem_limit_bytes=V7X_VMEM_LIMIT_BYTES),
        name="sample_sel",
    )(page_table, *([pages] * PAGES_PER_STEP), qbd, nonsel, knew, vnew)

    tok = lambda o: o.reshape(b_, N_KV, Q_PER_KV, S_PAD, HEAD_DIM)[:, :, :, :s_].transpose(0, 3, 1, 2, 4).reshape(1, rows_n, N_HEADS * HEAD_DIM)
    gexp = lambda br: jnp.repeat(gates[..., br:N_GATE:3], HEAD_DIM, axis=-1)
    full = lambda n: pl.BlockSpec((1, rows_n, n), lambda i: (0, 0, 0))
    vec = pl.BlockSpec((1, d_), lambda i: (0, 0))
    x_new = pl.pallas_call(
        _gated_out_kernel,
        grid=(1,),
        in_specs=[full(N_HEADS * HEAD_DIM)] * 6 + [full(d_), full(d_), pl.BlockSpec(w_o.shape, lambda i: (0, 0)), vec, vec],
        out_specs=full(d_),
        out_shape=jax.ShapeDtypeStruct((1, rows_n, d_), f32),
        name="sample_gated_out",
    )(tok(oc), tok(o_s), tok(ow), gexp(0), gexp(1), gexp(2), xf, flat(gate), w_o.astype(bf16), ln_g.reshape(1, d_), ln_b.reshape(1, d_))
    return x_new.reshape(b_, s_, d_), rows.reshape(b_, s_, 4, N_KV, HEAD_DIM), w_all[:, n_wk - w_buf:]


HPG = SSM_HEADS // SSM_GROUPS
GN = SSM_GROUPS * D_STATE
GW = D_INNER // SSM_GROUPS
DT_PAD = 128
CONV_KEEP = 8


def _ssd_proj_kernel(x_ref, sh_ref, sc_ref, w_ref, z_ref, xbc_ref, dt_ref):
    hb = (x_ref[0] * (1.0 + sc_ref[0]) + sh_ref[0]).astype(bf16)
    z_ref[0] = jnp.dot(hb, w_ref[:, 0:D_INNER], preferred_element_type=f32)
    xbc_ref[0] = jnp.dot(hb, w_ref[:, D_INNER:D_INNER + CONV_DIM], preferred_element_type=f32)
    dt_ref[0] = jnp.dot(hb, w_ref[:, D_INNER + CONV_DIM:], preferred_element_type=f32)


def ssd_project(x, shift, scale, w_in):
    b_, t_, d_ = x.shape
    rows = min(PROJ_ROWS, t_)
    w_all = jnp.pad(w_in, ((0, 0), (0, DT_PAD - SSM_HEADS))).astype(bf16)
    tile = lambda n: pl.BlockSpec((1, rows, n), lambda b, t: (b, t, 0))
    mod = _mod_spec(shift, rows)
    return pl.pallas_call(
        _ssd_proj_kernel,
        grid=(b_, t_ // rows),
        in_specs=[tile(d_), mod, mod, pl.BlockSpec(w_all.shape, lambda b, t: (0, 0), pipeline_mode=pl.Buffered(1))],
        out_specs=[tile(D_INNER), tile(CONV_DIM), tile(DT_PAD)],
        out_shape=[jax.ShapeDtypeStruct((b_, t_, D_INNER), f32), jax.ShapeDtypeStruct((b_, t_, CONV_DIM), f32),
                   jax.ShapeDtypeStruct((b_, t_, DT_PAD), f32)],
        compiler_params=pltpu.CompilerParams(dimension_semantics=("arbitrary", "arbitrary"),
                                             vmem_limit_bytes=V7X_VMEM_LIMIT_BYTES),
        name="ssd_project",
    )(x, shift, scale, w_all)


def _split3(v):
    p1 = v.astype(bf16)
    r1 = v - p1.astype(f32)
    p2 = r1.astype(bf16)
    p3 = (r1 - p2.astype(f32)).astype(bf16)
    return p1, p2, p3


def _dot3(parts, m, left):
    out = None
    for p in parts:
        t = jnp.dot(m, p, preferred_element_type=f32) if left else jnp.dot(p, m, preferred_element_type=f32)
        out = t if out is None else out + t
    return out


def _ssd_scan_kernel(xbc_ref, dt_ref, z_ref, cst_ref, h0_ref, cw_ref, cb_ref, dtb_ref, a_ref, d_ref, ng_ref, ex_ref,
                     y_ref, ht_ref, win_sc, st_sc, *, n_valid):
    c = pl.program_id(1)
    L = SSM_CHUNK

    @pl.when(c == 0)
    def _():
        win_sc[0:CONV_KEEP, :] = cst_ref[0]
        st_sc[...] = h0_ref[0]

    win_sc[CONV_KEEP:CONV_KEEP + L, :] = xbc_ref[0]
    acc = cb_ref[...] + win_sc[pl.ds(CONV_KEEP, L), :] * cw_ref[CONV_W - 1:CONV_W, :]
    for k in range(CONV_W - 1):
        acc = acc + win_sc[pl.ds(CONV_KEEP - (CONV_W - 1) + k, L), :] * cw_ref[k:k + 1, :]
    win_sc[0:CONV_KEEP, :] = win_sc[L:L + CONV_KEEP, :]
    xbc = acc * jax.nn.sigmoid(acc)
    xs = xbc[:, 0:D_INNER]

    t_row = lax.broadcasted_iota(jnp.int32, (L, 1), 0)
    dt_in = dt_ref[0] + dtb_ref[...]
    dt = jnp.maximum(dt_in, 0.0) + jnp.log1p(jnp.exp(-jnp.abs(dt_in)))
    dt = jnp.where(t_row < n_valid, dt, 0.0)
    a = dt * a_ref[...]
    ii = lax.broadcasted_iota(jnp.int32, (L, L), 0)
    jj = lax.broadcasted_iota(jnp.int32, (L, L), 1)
    lower = ii >= jj
    tri = jnp.where(lower, 1.0, 0.0).astype(bf16)
    a_cs = _dot3(_split3(a), tri, left=True)
    a_cs_t = a_cs.T
    ex = ex_ref[...]
    dt_e = _dot3(_split3(dt), ex, left=False)
    acs_e = _dot3(_split3(a_cs), ex, left=False)
    a_tot_e = acs_e[L - 1:L, :]
    xd = xs * dt_e
    xdd = (xd * jnp.exp(a_tot_e - acs_e)).astype(bf16)
    xdb = xd.astype(bf16)
    grow = jnp.exp(acs_e)
    lane_lo = lax.broadcasted_iota(jnp.int32, (L, 2 * SSM_HEAD_DIM), 1) < SSM_HEAD_DIM
    nt = (((1,), (1,)), ((), ()))

    for g in range(SSM_GROUPS):
        gl = slice(g * GW, (g + 1) * GW)
        bm = xbc[:, D_INNER + g * D_STATE:D_INNER + (g + 1) * D_STATE]
        cm = xbc[:, D_INNER + GN + g * D_STATE:D_INNER + GN + (g + 1) * D_STATE].astype(bf16)
        cb = lax.dot_general(cm, bm.astype(bf16), nt, preferred_element_type=f32)
        st_old = st_sc[:, gl]
        y_g = jnp.dot(cm, st_old.astype(bf16), preferred_element_type=f32) * grow[:, gl]
        st_sc[:, gl] = jnp.exp(a_tot_e[:, gl]) * st_old + jnp.dot(bm.T.astype(bf16), xdd[:, gl], preferred_element_type=f32)
        pairs = []
        for k in range(HPG // 2):
            h0 = g * HPG + 2 * k
            ms = []
            for h in (h0, h0 + 1):
                seg = a_cs[:, h:h + 1] - a_cs_t[h:h + 1, :]
                ms.append((cb * jnp.where(lower, jnp.exp(seg), 0.0)).astype(bf16))
            xp = xdb[:, h0 * SSM_HEAD_DIM:(h0 + 2) * SSM_HEAD_DIM]
            pairs.append(jnp.where(lane_lo, jnp.dot(ms[0], xp, preferred_element_type=f32),
                                   jnp.dot(ms[1], xp, preferred_element_type=f32)))
        y_g = y_g + jnp.concatenate(pairs, axis=1) + d_ref[:, gl] * xs[:, gl]
        zg = z_ref[0, :, gl]
        y_g = y_g * (zg * jax.nn.sigmoid(zg))
        y_g = y_g * lax.rsqrt(jnp.mean(y_g * y_g, axis=-1, keepdims=True) + LN_EPS)
        y_ref[0, :, gl] = (y_g * ng_ref[:, gl]).astype(bf16)

    @pl.when(c == pl.num_programs(1) - 1)
    def _():
        ht_ref[0] = st_sc[...]


def ssd_scan(xbc, dt_raw, z, conv_state, h0, conv_w, conv_b, dt_bias, a_log, d_skip, norm_g, n_valid):
    b_, t_, _ = xbc.shape
    L = SSM_CHUNK
    cst = jnp.pad(conv_state, ((0, 0), (CONV_KEEP - (CONV_W - 1), 0), (0, 0)))
    h0t = h0.transpose(0, 3, 1, 2).reshape(b_, D_STATE, D_INNER)
    vec = lambda v: jnp.pad(v, (0, DT_PAD - SSM_HEADS)).reshape(1, DT_PAD)
    ex = (jnp.arange(DT_PAD)[:, None] == (jnp.arange(D_INNER) // SSM_HEAD_DIM)[None, :]).astype(bf16)
    tile = lambda n: pl.BlockSpec((1, L, n), lambda b, c: (b, c, 0))
    per_b = lambda r, n: pl.BlockSpec((1, r, n), lambda b, c: (b, 0, 0))
    const = lambda r, n: pl.BlockSpec((r, n), lambda b, c: (0, 0))
    y, ht = pl.pallas_call(
        functools.partial(_ssd_scan_kernel, n_valid=n_valid),
        grid=(b_, t_ // L),
        in_specs=[tile(CONV_DIM), tile(DT_PAD), tile(D_INNER), per_b(CONV_KEEP, CONV_DIM), per_b(D_STATE, D_INNER),
                  const(CONV_W, CONV_DIM), const(1, CONV_DIM), const(1, DT_PAD), const(1, DT_PAD), const(1, D_INNER),
                  const(1, D_INNER), const(DT_PAD, D_INNER)],
        out_specs=[tile(D_INNER), per_b(D_STATE, D_INNER)],
        out_shape=[jax.ShapeDtypeStruct((b_, t_, D_INNER), bf16), jax.ShapeDtypeStruct((b_, D_STATE, D_INNER), f32)],
        scratch_shapes=[pltpu.VMEM((L + CONV_KEEP, CONV_DIM), f32), pltpu.VMEM((D_STATE, D_INNER), f32)],
        compiler_params=pltpu.CompilerParams(dimension_semantics=("arbitrary", "arbitrary"),
                                             vmem_limit_bytes=V7X_VMEM_LIMIT_BYTES),
        name="ssd_scan",
    )(xbc, dt_raw, z, cst, h0t, conv_w, conv_b.reshape(1, CONV_DIM), vec(dt_bias), vec(-jnp.exp(a_log)),
      jnp.repeat(d_skip, SSM_HEAD_DIM).reshape(1, D_INNER), norm_g.reshape(1, D_INNER), ex)
    return y, ht.reshape(b_, D_STATE, SSM_HEADS, SSM_HEAD_DIM).transpose(0, 2, 3, 1)


def ssd_layer(x, shift, scale, gate, conv_state, ssm_state, w_in, conv_w, conv_b, dt_bias, a_log, d_skip, norm_g, w_out,
              ln_g, ln_b):
    b_, t_, d_ = x.shape
    fold = t_ < SSM_CHUNK
    if fold:
        flat = lambda m: jnp.broadcast_to(m, (b_, t_, d_)).reshape(1, b_ * t_, d_)
        xf, shift, scale, gate = x.reshape(1, b_ * t_, d_), flat(shift), flat(scale), flat(gate)
    else:
        xf = x
    z, xbc, dt_raw = ssd_project(xf, shift, scale, w_in)
    if fold:
        padt = lambda v: jnp.pad(v.reshape(b_, t_, -1), ((0, 0), (0, SSM_CHUNK - t_), (0, 0)))
        z, xbc, dt_raw = padt(z), padt(xbc), padt(dt_raw)
    y, new_ssm = ssd_scan(xbc, dt_raw, z, conv_state, ssm_state, conv_w, conv_b, dt_bias, a_log, d_skip, norm_g, t_)
    new_conv = jnp.concatenate([conv_state, xbc[:, :t_]], axis=1)[:, t_:]
    if fold:
        y = y[:, :t_].reshape(1, b_ * t_, D_INNER)
    x_new = mixer_out(y, xf, gate, w_out.astype(bf16), ln_g, ln_b).reshape(b_, t_, d_)
    return x_new, new_conv, new_ssm


def _layer_norm(x, g, b):
    mu = x.mean(-1, keepdims=True)
    var = jnp.square(x - mu).mean(-1, keepdims=True)
    return (x - mu) * lax.rsqrt(var + LN_EPS) * g + b


def _rope(x, pos):
    half = HEAD_DIM // 2
    inv = ROPE_THETA ** (-jnp.arange(half, dtype=f32) / half)
    ang = pos.astype(f32)[:, None] * inv
    cos, sin = jnp.cos(ang)[:, None, :], jnp.sin(ang)[:, None, :]
    x1, x2 = x[..., :half], x[..., half:]
    return jnp.concatenate([x1 * cos - x2 * sin, x2 * cos + x1 * sin], -1)


def _masked_softmax(s, mask):
    s = jnp.where(mask, s, -jnp.inf)
    m = jnp.max(s, axis=-1, keepdims=True)
    e = jnp.exp(s - jnp.where(jnp.isfinite(m), m, 0.0))
    return e / jnp.maximum(jnp.sum(e, axis=-1, keepdims=True), 1e-30)


def _compress(kv, pe, w1, b1, w2, b2):
    b_, l_, g_, d_ = kv.shape
    n_sub = CMP_BLK // CMP_STRIDE
    n_c = (l_ - CMP_BLK) // CMP_STRIDE + 1
    n_ch = n_c + n_sub - 1
    ch = kv[:, :n_ch * CMP_STRIDE].reshape(b_, n_ch, CMP_STRIDE, g_, d_)
    ch = jnp.moveaxis(ch, 3, 2).reshape(b_, n_ch, g_, CMP_STRIDE * d_)
    w1s = w1.reshape(n_sub, CMP_STRIDE * d_, CMP_HIDDEN)
    proj = jnp.einsum('bngf,jfh->jbngh', ch, w1s)
    pre = sum(proj[j][:, j:j + n_c] for j in range(n_sub)) + pe.reshape(-1) @ w1 + b1
    return jax.nn.gelu(pre) @ w2 + b2


def _compressed_kv(k_raw, v_raw, pe, w1, b1, w2, b2):
    kc = _compress(k_raw, pe[0], w1[0], b1[0], w2[0], b2[0])
    vc = _compress(v_raw, pe[1], w1[1], b1[1], w2[1], b2[1])
    cend = jnp.arange(kc.shape[1]) * CMP_STRIDE + CMP_BLK - 1
    return _rope(kc, cend), vc, cend


def _cmp_to_sel(n_c, n_sel):
    cs = jnp.arange(n_c)[:, None] * CMP_STRIDE
    ss = jnp.arange(n_sel)[None, :] * SEL_BLK
    ov = jnp.minimum(cs + CMP_BLK, ss + SEL_BLK) - jnp.maximum(cs, ss)
    return jnp.clip(ov, 0, None).astype(f32) / CMP_BLK


def _select_blocks(p_cmp, qpos, n_sel):
    imp = jnp.einsum('bqgc,cs->bqgs', p_cmp, _cmp_to_sel(p_cmp.shape[-1], n_sel))
    j = jnp.arange(n_sel)[None, :]
    cur = (qpos // SEL_BLK)[:, None]
    valid = (j * SEL_BLK <= qpos[:, None])[:, None, :]
    forced = ((j == 0) | (j == cur) | (j == cur - 1))[:, None, :]
    score = jnp.where(valid, jnp.where(forced, jnp.inf, imp), -jnp.inf)
    top, idx = lax.top_k(score, min(TOPK, n_sel))
    return idx, top > -jnp.inf


def _nsa_core(q, qpos, kc, vc, cend, n_sel, fetch_sel, kw, vw, kwpos, gates):
    qf = q * HEAD_DIM ** -0.5
    s = jnp.einsum('bqgrd,bcgd->bqgrc', qf, kc)
    pc = _masked_softmax(s, (cend[None, :] <= qpos[:, None])[None, :, None, None, :])
    o_c = jnp.einsum('bqgrc,bcgd->bqgrd', pc, vc)
    idx, ok = _select_blocks(pc.sum(3), qpos, n_sel)
    ks, vs = fetch_sel(idx)
    b_, q_, g_, k_ = idx.shape
    kpos = idx[..., None] * SEL_BLK + jnp.arange(SEL_BLK)
    ms = (ok[..., None] & (kpos <= qpos[None, :, None, None, None])).reshape(b_, q_, g_, 1, k_ * SEL_BLK)
    ks = ks.reshape(b_, q_, g_, k_ * SEL_BLK, HEAD_DIM)
    vs = vs.reshape(b_, q_, g_, k_ * SEL_BLK, HEAD_DIM)
    ps = _masked_softmax(jnp.einsum('bqgrd,bqgkd->bqgrk', qf, ks), ms)
    o_s = jnp.einsum('bqgrk,bqgkd->bqgrd', ps, vs)
    dpos = qpos[:, None] - kwpos[None, :]
    mw = ((dpos >= 0) & (dpos < WINDOW) & (kwpos >= 0)[None, :])[None, :, None, None, :]
    pw = _masked_softmax(jnp.einsum('bqgrd,bkgd->bqgrk', qf, kw), mw)
    o_w = jnp.einsum('bqgrk,bkgd->bqgrd', pw, vw)
    return gates[..., 0:1] * o_c + gates[..., 1:2] * o_s + gates[..., 2:3] * o_w


def _nsa_project(h, w_in, pos):
    b_, t_, _ = h.shape
    qd, kd = N_HEADS * HEAD_DIM, N_KV * HEAD_DIM
    proj = h @ w_in
    q = _rope(proj[..., :qd].reshape(b_, t_, N_HEADS, HEAD_DIM), pos).reshape(b_, t_, N_KV, Q_PER_KV, HEAD_DIM)
    kv = proj[..., qd:qd + 6 * kd].reshape(b_, t_, 6, N_KV, HEAD_DIM)
    rows = jnp.stack([kv[:, :, 0], kv[:, :, 1], _rope(kv[:, :, 2], pos), kv[:, :, 3]], axis=2)
    win = jnp.stack([_rope(kv[:, :, 4], pos), kv[:, :, 5]], axis=2)
    gates = jax.nn.sigmoid(proj[..., qd + 6 * kd:]).reshape(b_, t_, N_KV, Q_PER_KV, 3)
    return q, rows, win, gates


def _nsa_prompt(h, w_in, w_o, pe, cw1, cb1, cw2, cb2):
    b_, t_, _ = h.shape
    q, rows, win, gates = _nsa_project(h, w_in, jnp.arange(t_))
    kc, vc, cend = _compressed_kv(rows[:, :, 0], rows[:, :, 1], pe, cw1, cb1, cw2, cb2)
    n_sel = t_ // SEL_BLK
    k_blk = rows[:, :, 2].reshape(b_, n_sel, SEL_BLK, N_KV, HEAD_DIM)
    v_blk = rows[:, :, 3].reshape(b_, n_sel, SEL_BLK, N_KV, HEAD_DIM)
    bi = jnp.arange(b_)[:, None, None, None]
    gi = jnp.arange(N_KV)[None, None, :, None]

    def fetch(idx):
        return k_blk[bi, idx, :, gi], v_blk[bi, idx, :, gi]

    w_pad = jnp.pad(win, ((0, 0), (WINDOW, 0), (0, 0), (0, 0), (0, 0)))

    def block(i):
        s0 = i * Q_BLOCK
        qb = lax.dynamic_slice_in_dim(q, s0, Q_BLOCK, axis=1)
        gb = lax.dynamic_slice_in_dim(gates, s0, Q_BLOCK, axis=1)
        wb = lax.dynamic_slice_in_dim(w_pad, s0, Q_BLOCK + WINDOW, axis=1)
        qpos = s0 + jnp.arange(Q_BLOCK)
        kwpos = s0 - WINDOW + jnp.arange(Q_BLOCK + WINDOW)
        return _nsa_core(qb, qpos, kc, vc, cend, n_sel, fetch, wb[:, :, 0], wb[:, :, 1], kwpos, gb)

    o = lax.map(block, jnp.arange(t_ // Q_BLOCK))
    o = jnp.moveaxis(o, 0, 1).reshape(b_, t_, N_HEADS * HEAD_DIM)
    return o @ w_o, (rows, win[:, t_ - min(WINDOW, t_):])


def _nsa_sample(h, past_len, cache_kv, page_table, li, win_buf, w_in, w_o, pe, cw1, cb1, cw2, cb2):
    b_, s_, _ = h.shape
    pos = past_len + jnp.arange(s_)
    q, rows, win, gates = _nsa_project(h, w_in, pos)
    cmp_pages = cache_kv[:, li, :, :2].reshape(cache_kv.shape[0], -1)
    past = jnp.take(cmp_pages, page_table.reshape(-1), axis=0).reshape(b_, -1, 2, N_KV, HEAD_DIM)
    k_raw = jnp.concatenate([past[:, :, 0], rows[:, :, 0]], axis=1)
    v_raw = jnp.concatenate([past[:, :, 1], rows[:, :, 1]], axis=1)
    kc, vc, cend = _compressed_kv(k_raw, v_raw, pe, cw1, cb1, cw2, cb2)
    n_past_blk = past_len // SEL_BLK
    n_new_blk = -(-s_ // SEL_BLK)
    n_sel = n_past_blk + n_new_blk
    bpp = PAGE_SIZE // SEL_BLK
    new_sel = jnp.pad(rows[:, :, 2:4], ((0, 0), (0, n_new_blk * SEL_BLK - s_), (0, 0), (0, 0), (0, 0)))
    new_k = new_sel[:, :, 0].reshape(b_, n_new_blk, SEL_BLK, N_KV, HEAD_DIM)
    new_v = new_sel[:, :, 1].reshape(b_, n_new_blk, SEL_BLK, N_KV, HEAD_DIM)
    bi = jnp.arange(b_)[:, None, None, None]
    gi = jnp.arange(N_KV)[None, None, :, None]

    def fetch(idx):
        jp = jnp.minimum(idx, n_past_blk - 1)
        phys = page_table[bi, jp // bpp][..., None]
        rr = (jp % bpp)[..., None] * SEL_BLK + jnp.arange(SEL_BLK)
        pk = cache_kv[phys, li, rr, 2, gi[..., None]]
        pv = cache_kv[phys, li, rr, 3, gi[..., None]]
        jn = jnp.clip(idx - n_past_blk, 0, n_new_blk - 1)
        is_past = (idx < n_past_blk)[..., None, None]
        return (jnp.where(is_past, pk, new_k[bi, jn, :, gi]), jnp.where(is_past, pv, new_v[bi, jn, :, gi]))

    w_buf = win_buf.shape[1]
    w_all = jnp.concatenate([win_buf, win], axis=1)
    kwpos = past_len - w_buf + jnp.arange(w_buf + s_)
    o = _nsa_core(q, pos, kc, vc, cend, n_sel, fetch, w_all[:, :, 0], w_all[:, :, 1], kwpos, gates)
    o = o.reshape(b_, s_, N_HEADS * HEAD_DIM)
    return o @ w_o, (rows, w_all[:, -w_buf:])


def _segsum(a):
    t_ = a.shape[-1]
    ii = jnp.arange(t_)
    x = jnp.where(ii[:, None] > ii[None, :], jnp.broadcast_to(a[..., :, None], a.shape + (t_,)), 0.0)
    return jnp.where(ii[:, None] >= ii[None, :], jnp.cumsum(x, axis=-2), -jnp.inf)


def _ssd_scan(x, dt, a_head, bm, cm, h0):
    b_, t_, g_, r_, p_ = x.shape
    n_ = bm.shape[-1]
    lc = min(SSM_CHUNK, t_)
    nc = -(-t_ // lc)
    pad = nc * lc - t_

    def padt(z):
        return jnp.pad(z, [(0, 0), (0, pad)] + [(0, 0)] * (z.ndim - 2))

    xd = padt(x * dt[..., None]).reshape(b_, nc, lc, g_, r_, p_)
    a = jnp.moveaxis(padt(dt * a_head).reshape(b_, nc, lc, g_, r_), (1, 2), (3, 4))
    bc = padt(bm).reshape(b_, nc, lc, g_, n_)
    cc = padt(cm).reshape(b_, nc, lc, g_, n_)
    a_cs = jnp.cumsum(a, axis=-1)
    lmat = jnp.exp(_segsum(a))
    cb = jnp.einsum('bclgn,bcsgn->bgcls', cc, bc)
    y_diag = jnp.einsum('bgrcls,bcsgrp->bclgrp', cb[:, :, None] * lmat, xd)
    decay = jnp.exp(a_cs[..., -1:] - a_cs)
    states = jnp.einsum('bclgn,bgrcl,bclgrp->bcgrpn', bc, decay, xd)
    states = jnp.concatenate([h0[:, None], states], axis=1)
    chunk_a = jnp.pad(a_cs[..., -1], [(0, 0)] * 3 + [(1, 0)])
    states = jnp.einsum('bgrzc,bcgrpn->bzgrpn', jnp.exp(_segsum(chunk_a)), states)
    y_off = jnp.einsum('bclgn,bcgrpn,bgrcl->bclgrp', cc, states[:, :-1], jnp.exp(a_cs))
    y = (y_diag + y_off).reshape(b_, nc * lc, g_, r_, p_)[:, :t_]
    return y, states[:, -1]


def _ssd_mixer(h, conv_state, ssm_state, w_in, conv_w, conv_b, dt_bias, a_log, d_skip, norm_g, w_out):
    b_, t_, _ = h.shape
    hpg = SSM_HEADS // SSM_GROUPS
    gn = SSM_GROUPS * D_STATE
    proj = h @ w_in
    z = proj[..., :D_INNER]
    xbc = proj[..., D_INNER:D_INNER + CONV_DIM]
    dt_raw = proj[..., D_INNER + CONV_DIM:]
    full = jnp.concatenate([conv_state, xbc], axis=1)
    xbc = jax.nn.silu(sum(full[:, k:k + t_] * conv_w[k] for k in range(CONV_W)) + conv_b)
    xs = xbc[..., :D_INNER].reshape(b_, t_, SSM_GROUPS, hpg, SSM_HEAD_DIM)
    bm = xbc[..., D_INNER:D_INNER + gn].reshape(b_, t_, SSM_GROUPS, D_STATE)
    cm = xbc[..., D_INNER + gn:].reshape(b_, t_, SSM_GROUPS, D_STATE)
    dt = jax.nn.softplus(dt_raw + dt_bias).reshape(b_, t_, SSM_GROUPS, hpg)
    a_head = -jnp.exp(a_log).reshape(SSM_GROUPS, hpg)
    h0 = ssm_state.reshape(b_, SSM_GROUPS, hpg, SSM_HEAD_DIM, D_STATE)
    y, h_t = _ssd_scan(xs, dt, a_head, bm, cm, h0)
    y = y + d_skip.reshape(SSM_GROUPS, hpg, 1) * xs
    y = y.reshape(b_, t_, SSM_GROUPS, D_INNER // SSM_GROUPS) * jax.nn.silu(z).reshape(b_, t_, SSM_GROUPS, D_INNER // SSM_GROUPS)
    y = y * lax.rsqrt(jnp.mean(y * y, axis=-1, keepdims=True) + LN_EPS)
    y = y.reshape(b_, t_, D_INNER) * norm_g
    new_ssm = h_t.reshape(b_, SSM_HEADS, SSM_HEAD_DIM, D_STATE)
    return y @ w_out, (full[:, t_:], new_ssm)


ADA_COLS = 1152


def _adaln_kernel(c_ref, w_ref, b_ref, o_ref):
    c = c_ref[...]
    h = (c * jax.nn.sigmoid(c)).astype(bf16)
    o_ref[...] = jnp.dot(h, w_ref[...].astype(bf16), preferred_element_type=f32) + b_ref[...]


def adaln(c, w, b):
    r_, d_ = c.shape
    n_ = w.shape[1]
    return pl.pallas_call(
        _adaln_kernel,
        grid=(n_ // ADA_COLS,),
        in_specs=[pl.BlockSpec((r_, d_), lambda n: (0, 0)), pl.BlockSpec((d_, ADA_COLS), lambda n: (0, n)),
                  pl.BlockSpec((1, ADA_COLS), lambda n: (0, n))],
        out_specs=pl.BlockSpec((r_, ADA_COLS), lambda n: (0, n)),
        out_shape=jax.ShapeDtypeStruct((r_, n_), f32),
        compiler_params=pltpu.CompilerParams(dimension_semantics=("arbitrary",)),
        name="adaln",
    )(c, w, b.reshape(1, n_))


def _trunk_layer(x, m, i, mixer, ln_g, ln_b, ffn_a, ffn_b):
    b_, t_, d_ = x.shape
    m = m.reshape(b_, 3, 3, 1, D_MODEL)

    def ffn(v, s, w):
        mods = [m[:, s, k] for k in range(3)]
        if t_ < 8:
            mods = [jnp.broadcast_to(z, (b_, t_, d_)).reshape(1, b_ * t_, d_) for z in mods]
            return ffn_sublayer(v.reshape(1, b_ * t_, d_), *mods, *w, ln_g[i, s], ln_b[i, s]).reshape(b_, t_, d_)
        return ffn_sublayer(v, *mods, *w, ln_g[i, s], ln_b[i, s])

    x = ffn(x, 0, ffn_a)
    x, st = mixer(x, m[:, 1, 0], m[:, 1, 1], m[:, 1, 2], ln_g[i, 1], ln_b[i, 1])
    x = ffn(x, 2, ffn_b)
    return x, st


def kernel(x_prompt, x_sample, cache_kv, cache_win, state_conv, state_ssm, page_table, c_prompt, c_sample, ada_w, ada_b, ln_g, ln_b, ffn_w1, ffn_w3, ffn_w2, nsa_w_in, nsa_w_o, nsa_cmp_pe, nsa_cmp_w1, nsa_cmp_b1, nsa_cmp_w2, nsa_cmp_b2, ssm_w_in, ssm_conv_w, ssm_conv_b, ssm_dt_bias, ssm_a_log, ssm_d, ssm_norm_g, ssm_w_out):
    xp, xs = x_prompt, x_sample
    nb_p, nb_s = c_prompt.shape[0], c_sample.shape[0]
    c_all = jnp.concatenate([c_prompt, c_sample], axis=0)
    c_all = jnp.pad(c_all, ((0, -(nb_p + nb_s) % 8), (0, 0)))
    kv_p, kv_s, win_p, win_s, conv_p, conv_s, ssm_p, ssm_s = [], [], [], [], [], [], [], []
    for i in range(DEPTH):
        j = i // N_MIXERS
        ffn_a = _chunk_ffn_weights(ffn_w1[i, 0], ffn_w3[i, 0], ffn_w2[i, 0])
        ffn_b = _chunk_ffn_weights(ffn_w1[i, 1], ffn_w3[i, 1], ffn_w2[i, 1])
        m_all = adaln(c_all, ada_w[i], ada_b[i])
        m_p, m_s = m_all[:nb_p], m_all[nb_p:nb_p + nb_s]
        lw = (ln_g, ln_b, ffn_a, ffn_b)
        if i % N_MIXERS == 0:
            nw = (nsa_w_in[j], nsa_w_o[j], nsa_cmp_pe[j], nsa_cmp_w1[j], nsa_cmp_b1[j], nsa_cmp_w2[j], nsa_cmp_b2[j])
            def prompt_mixer(x, sh, sc, gt, lg, lb):
                x_new, rows, win = nsa_prompt_layer(x, sh, sc, gt, *nw, lg, lb)
                return x_new, (rows, win[:, x.shape[1] - min(WINDOW, x.shape[1]):])

            xp, (r_p, w_p) = _trunk_layer(xp, m_p, i, prompt_mixer, *lw)
            def sample_mixer(x, sh, sc, gt, lg, lb):
                x_new, rows, win = nsa_sample_layer(x, sh, sc, gt, cache_kv, page_table, j, cache_win[:, j], *nw, lg, lb)
                return x_new, (rows, win)

            xs, (r_s, w_s) = _trunk_layer(xs, m_s, i, sample_mixer, *lw)
            kv_p.append(r_p); kv_s.append(r_s); win_p.append(w_p); win_s.append(w_s)
        else:
            sw = (ssm_w_in[j], ssm_conv_w[j], ssm_conv_b[j], ssm_dt_bias[j], ssm_a_log[j], ssm_d[j], ssm_norm_g[j], ssm_w_out[j])
            zc = jnp.zeros((xp.shape[0], CONV_W - 1, CONV_DIM), f32)
            zs = jnp.zeros((xp.shape[0], SSM_HEADS, SSM_HEAD_DIM, D_STATE), f32)
            def ssd_mixer(conv0, ssm0):
                def run(x, sh, sc, gt, lg, lb):
                    x_new, cv, st = ssd_layer(x, sh, sc, gt, conv0, ssm0, *sw, lg, lb)
                    return x_new, (cv, st)
                return run

            xp, (cv_p, st_p) = _trunk_layer(xp, m_p, i, ssd_mixer(zc, zs), *lw)
            xs, (cv_s, st_s) = _trunk_layer(xs, m_s, i, ssd_mixer(state_conv[:, j], state_ssm[:, j]), *lw)
            conv_p.append(cv_p); conv_s.append(cv_s); ssm_p.append(st_p); ssm_s.append(st_s)
    return (xp, xs, jnp.stack(kv_p, 1), jnp.stack(kv_s, 1), jnp.stack(win_p, 1), jnp.stack(win_s, 1),
            jnp.stack(conv_p, 1), jnp.stack(conv_s, 1), jnp.stack(ssm_p, 1), jnp.stack(ssm_s, 1))
```

```python
import functools
import math

import jax
import jax.numpy as jnp
from jax import lax
from jax.experimental import pallas as pl
from jax.experimental.pallas import tpu as pltpu

f32 = jnp.float32
bf16 = jnp.bfloat16

D_MODEL = 1024
DEPTH = 4
PAGE_SIZE = 128
N_HEADS = 16
HEAD_DIM = 64
N_KV = 4
Q_PER_KV = N_HEADS // N_KV
CMP_BLK = 32
CMP_STRIDE = 16
CMP_HIDDEN = 2 * HEAD_DIM
SEL_BLK = 64
TOPK = 16
WINDOW = 512
Q_BLOCK = 128
ROPE_THETA = 10000.0
D_INNER = 2 * D_MODEL
SSM_HEAD_DIM = 64
SSM_HEADS = D_INNER // SSM_HEAD_DIM
SSM_GROUPS = 4
D_STATE = 128
CONV_W = 4
CONV_DIM = D_INNER + 2 * SSM_GROUPS * D_STATE
SSM_CHUNK = 128
D_FF = 256 * ((8 * D_MODEL // 3 + 255) // 256)
N_MIXERS = 2
ALPHA = (2 * DEPTH) ** 0.25
N_ADA = 9
LN_EPS = 1e-5

V7X_VMEM_LIMIT_BYTES = 56 * 1024 * 1024
FF_CHUNK = 256
FFN_ROWS = 512


def _ffn_kernel(x_ref, sh_ref, sc_ref, gt_ref, w1_ref, w3_ref, w2_ref, lg_ref, lb_ref, o_ref, acc_ref):
    x = x_ref[0]
    hb = (x * (1.0 + sc_ref[0]) + sh_ref[0]).astype(bf16)
    acc_ref[...] = jnp.zeros_like(acc_ref)

    def chunk(c, carry):
        a = jnp.dot(hb, w1_ref[c], preferred_element_type=f32)
        b = jnp.dot(hb, w3_ref[c], preferred_element_type=f32)
        g = (a * jax.nn.sigmoid(a) * b).astype(bf16)
        acc_ref[...] += jnp.dot(g, w2_ref[c], preferred_element_type=f32)
        return carry

    lax.fori_loop(0, w1_ref.shape[0], chunk, 0)
    y = ALPHA * x + (1.0 + gt_ref[0]) * (0.5 * acc_ref[...])
    mu = jnp.mean(y, axis=-1, keepdims=True)
    yc = y - mu
    var = jnp.mean(yc * yc, axis=-1, keepdims=True)
    o_ref[0] = yc * lax.rsqrt(var + LN_EPS) * lg_ref[...] + lb_ref[...]


def _mod_spec(m, rows):
    if m.shape[1] == 1:
        return pl.BlockSpec((1, 1, m.shape[2]), lambda b, t: (b, 0, 0))
    return pl.BlockSpec((1, rows, m.shape[2]), lambda b, t: (b, t, 0))


def ffn_sublayer(x, shift, scale, gate, w1c, w3c, w2c, ln_g, ln_b):
    b_, t_, d_ = x.shape
    rows = min(FFN_ROWS, t_)
    mod_spec = _mod_spec(shift, rows)
    const3 = lambda b, t: (0, 0, 0)
    wspec = lambda w: pl.BlockSpec(w.shape, const3, pipeline_mode=pl.Buffered(1))
    vec = pl.BlockSpec((1, d_), lambda b, t: (0, 0))
    return pl.pallas_call(
        _ffn_kernel,
        grid=(b_, t_ // rows),
        in_specs=[pl.BlockSpec((1, rows, d_), lambda b, t: (b, t, 0)), mod_spec, mod_spec, mod_spec,
                  wspec(w1c), wspec(w3c), wspec(w2c), vec, vec],
        out_specs=pl.BlockSpec((1, rows, d_), lambda b, t: (b, t, 0)),
        out_shape=jax.ShapeDtypeStruct(x.shape, f32),
        scratch_shapes=[pltpu.VMEM((rows, d_), f32)],
        compiler_params=pltpu.CompilerParams(dimension_semantics=("arbitrary", "arbitrary"),
                                             vmem_limit_bytes=V7X_VMEM_LIMIT_BYTES),
        name="ffn_sublayer",
    )(x, shift, scale, gate, w1c, w3c, w2c, ln_g.reshape(1, d_), ln_b.reshape(1, d_))


def _chunk_ffn_weights(w1, w3, w2):
    n = D_FF // FF_CHUNK
    w1c = w1.astype(bf16).reshape(D_MODEL, n, FF_CHUNK).transpose(1, 0, 2)
    w3c = w3.astype(bf16).reshape(D_MODEL, n, FF_CHUNK).transpose(1, 0, 2)
    w2c = w2.astype(bf16).reshape(n, FF_CHUNK, D_MODEL)
    return w1c, w3c, w2c


NEG = -1e30
PROJ_ROWS = 512
SEL_TK = 512
KD = N_KV * HEAD_DIM
_C_Q, _C_QR, _C_CMP, _C_SEL, _C_WIN, _C_G, _C_END = 0, 1024, 2048, 2560, 3328, 4096, 4224
N_GATE = 3 * N_HEADS


def _rot_cols(w):
    w4 = w.reshape(w.shape[0], -1, 2, HEAD_DIM // 2)
    return jnp.stack([-w4[:, :, 1], w4[:, :, 0]], axis=2).reshape(w.shape)


def _nsa_proj_weights(w_in):
    qd = N_HEADS * HEAD_DIM
    wq = w_in[:, :qd] * HEAD_DIM ** -0.5
    kc, vc, ks, vs, kw, vw = [w_in[:, qd + i * KD: qd + (i + 1) * KD] for i in range(6)]
    wg = jnp.pad(w_in[:, qd + 6 * KD:], ((0, 0), (0, 128 - N_GATE)))
    return jnp.concatenate([wq, _rot_cols(wq), kc, vc, ks, _rot_cols(ks), vs, kw, _rot_cols(kw), vw, wg], axis=1).astype(bf16)


def _rope_tables(pos, width):
    half = HEAD_DIM // 2
    inv = ROPE_THETA ** (-jnp.arange(half, dtype=f32) / half)
    ang = jnp.tile(pos.astype(f32)[:, None] * inv, (1, width // half))
    return jnp.cos(ang), jnp.sin(ang)


def _nsa_proj_kernel(x_ref, sh_ref, sc_ref, w_ref, cos_ref, sin_ref, q_ref, rows_ref, win_ref, gate_ref, kvb_ref):
    hb = (x_ref[0] * (1.0 + sc_ref[0]) + sh_ref[0]).astype(bf16)
    cos, sin = cos_ref[...], sin_ref[...]

    def mm(lo, hi):
        return jnp.dot(hb, w_ref[:, lo:hi], preferred_element_type=f32)

    def rope(a, b, c):
        return a[:, c * 128:(c + 1) * 128] * cos + b[:, c * 128:(c + 1) * 128] * sin

    qa, qb = mm(_C_Q, _C_QR), mm(_C_QR, _C_CMP)
    for c in range(N_HEADS * HEAD_DIM // 128):
        q_ref[0, :, c * 128:(c + 1) * 128] = rope(qa, qb, c).astype(bf16)
    rows_ref[0, :, 0:2 * KD] = mm(_C_CMP, _C_SEL)
    sel = mm(_C_SEL, _C_WIN)
    win = mm(_C_WIN, _C_G)
    for c in range(KD // 128):
        ks = rope(sel[:, 0:KD], sel[:, KD:2 * KD], c)
        kw = rope(win[:, 0:KD], win[:, KD:2 * KD], c)
        rows_ref[0, :, 2 * KD + c * 128:2 * KD + (c + 1) * 128] = ks
        win_ref[0, :, c * 128:(c + 1) * 128] = kw
        kvb_ref[0, :, c * 128:(c + 1) * 128] = ks.astype(bf16)
        kvb_ref[0, :, 2 * KD + c * 128:2 * KD + (c + 1) * 128] = kw.astype(bf16)
    rows_ref[0, :, 3 * KD:4 * KD] = sel[:, 2 * KD:3 * KD]
    win_ref[0, :, KD:2 * KD] = win[:, 2 * KD:3 * KD]
    kvb_ref[0, :, KD:2 * KD] = sel[:, 2 * KD:3 * KD].astype(bf16)
    kvb_ref[0, :, 3 * KD:4 * KD] = win[:, 2 * KD:3 * KD].astype(bf16)
    gate_ref[0] = jax.nn.sigmoid(mm(_C_G, _C_END))


def nsa_project(x, shift, scale, w_all, cos, sin):
    b_, t_, d_ = x.shape
    rows = min(PROJ_ROWS, t_)
    tile = lambda n: pl.BlockSpec((1, rows, n), lambda b, t: (b, t, 0))
    mod = _mod_spec(shift, rows)
    tab = pl.BlockSpec((rows, 128), lambda b, t: (t, 0))
    return pl.pallas_call(
        _nsa_proj_kernel,
        grid=(b_, t_ // rows),
        in_specs=[tile(d_), mod, mod, pl.BlockSpec(w_all.shape, lambda b, t: (0, 0), pipeline_mode=pl.Buffered(1)), tab, tab],
        out_specs=[tile(4 * KD), tile(4 * KD), tile(2 * KD), tile(128), tile(4 * KD)],
        out_shape=[jax.ShapeDtypeStruct((b_, t_, 4 * KD), bf16), jax.ShapeDtypeStruct((b_, t_, 4 * KD), f32),
                   jax.ShapeDtypeStruct((b_, t_, 2 * KD), f32), jax.ShapeDtypeStruct((b_, t_, 128), f32),
                   jax.ShapeDtypeStruct((b_, t_, 4 * KD), bf16)],
        compiler_params=pltpu.CompilerParams(dimension_semantics=("arbitrary", "arbitrary"),
                                             vmem_limit_bytes=V7X_VMEM_LIMIT_BYTES),
        name="nsa_project",
    )(x, shift, scale, w_all, cos, sin)


def _gelu_tanh(x):
    return 0.5 * x * (1.0 + jnp.tanh(math.sqrt(2.0 / math.pi) * (x + 0.044715 * (x * x * x))))


def _compress_kernel(xa_ref, xb_ref, pe_ref, w1_ref, b1_ref, w2_ref, w2r_ref, b2_ref, b2r_ref, w2t_ref, w2rt_ref,
                     b2t_ref, b2rt_ref, cos_ref, sin_ref, cost_ref, sint_ref, on_ref, ot_ref, *, projected):
    w1 = w1_ref[0]
    half = w1.shape[0] // 2
    bias = jnp.dot(pe_ref[0], w1, preferred_element_type=f32)[0:1] + b1_ref[0]
    if projected:
        pre = xa_ref[0, 0, 0] + xb_ref[0, 0, 0] + bias
    else:
        pre = (jnp.dot(xa_ref[0, 0, 0], w1[:half], preferred_element_type=f32)
               + jnp.dot(xb_ref[0, 0, 0], w1[half:], preferred_element_type=f32) + bias)
    hb = _gelu_tanh(pre).astype(bf16)
    y = jnp.dot(hb, w2_ref[0], preferred_element_type=f32) + b2_ref[0]
    yr = jnp.dot(hb, w2r_ref[0], preferred_element_type=f32) + b2r_ref[0]
    on_ref[0, 0, 0] = y * cos_ref[0] + yr * sin_ref[0]
    nt = (((1,), (1,)), ((), ()))
    yt = lax.dot_general(w2t_ref[0], hb, nt, preferred_element_type=f32) + b2t_ref[0]
    yrt = lax.dot_general(w2rt_ref[0], hb, nt, preferred_element_type=f32) + b2rt_ref[0]
    ot_ref[0, 0, 0] = yt * cost_ref[0] + yrt * sint_ref[0]


def compress_kv(chunks, pe, w1, b1, w2, b2, projected=False):
    b_, _, g_, n_ch, feat = chunks.shape
    if projected:
        feat = CMP_HIDDEN
        first, second = chunks[..., :feat], chunks[..., feat:]
    else:
        first = second = chunks
    chunks = first
    nxt = jnp.concatenate([second[:, :, :, 1:], jnp.zeros_like(second[:, :, :, :1])], axis=3)
    cend = jnp.arange(n_ch) * CMP_STRIDE + CMP_BLK - 1
    cos, sin = _rope_tables(cend, HEAD_DIM)
    cos = jnp.stack([cos, jnp.ones_like(cos)])
    sin = jnp.stack([sin, jnp.zeros_like(sin)])
    w2b = w2.astype(bf16)
    w2r = _rot_cols(w2).astype(bf16)
    b2r = _rot_cols(b2[:, None, :])
    pe8 = jnp.broadcast_to(pe.reshape(2, 1, -1), (2, 8, pe.shape[1] * pe.shape[2])).astype(bf16)
    xspec = pl.BlockSpec((1, 1, 1, n_ch, feat), lambda b, k, g: (b, k, g, 0, 0))
    per_kv = lambda *s: pl.BlockSpec((1,) + s, lambda b, k, g: (k,) + (0,) * len(s))
    return pl.pallas_call(
        functools.partial(_compress_kernel, projected=projected),
        grid=(b_, 2, g_),
        in_specs=[xspec, xspec, per_kv(8, pe8.shape[2]), per_kv(*w1.shape[1:]), per_kv(1, CMP_HIDDEN),
                  per_kv(CMP_HIDDEN, HEAD_DIM), per_kv(CMP_HIDDEN, HEAD_DIM), per_kv(1, HEAD_DIM), per_kv(1, HEAD_DIM),
                  per_kv(HEAD_DIM, CMP_HIDDEN), per_kv(HEAD_DIM, CMP_HIDDEN), per_kv(HEAD_DIM, 1), per_kv(HEAD_DIM, 1),
                  per_kv(n_ch, HEAD_DIM), per_kv(n_ch, HEAD_DIM), per_kv(HEAD_DIM, n_ch), per_kv(HEAD_DIM, n_ch)],
        out_specs=[pl.BlockSpec((1, 1, 1, n_ch, HEAD_DIM), lambda b, k, g: (b, k, g, 0, 0)),
                   pl.BlockSpec((1, 1, 1, HEAD_DIM, n_ch), lambda b, k, g: (b, k, g, 0, 0))],
        out_shape=[jax.ShapeDtypeStruct((b_, 2, g_, n_ch, HEAD_DIM), f32),
                   jax.ShapeDtypeStruct((b_, 2, g_, HEAD_DIM, n_ch), f32)],
        compiler_params=pltpu.CompilerParams(dimension_semantics=("arbitrary",) * 3),
        name="compress_kv",
    )(chunks, nxt, pe8, w1.astype(bf16), b1[:, None, :], w2b, w2r, b2[:, None, :], b2r,
      w2b.transpose(0, 2, 1), w2r.transpose(0, 2, 1), b2[:, :, None], b2r.transpose(0, 2, 1),
      cos, sin, cos.transpose(0, 2, 1), sin.transpose(0, 2, 1))


def _cmp_to_sel_matrix(n_c, n_sel):
    cs = jnp.arange(n_c)[:, None] * CMP_STRIDE
    ss = jnp.arange(n_sel)[None, :] * SEL_BLK
    ov = jnp.minimum(cs + CMP_BLK, ss + SEL_BLK) - jnp.maximum(cs, ss)
    return jnp.clip(ov, 0, None).astype(f32) / CMP_BLK


def _lanes(col, n):
    if n % 128:
        return jnp.broadcast_to(col, (col.shape[0], n))
    tile = jnp.broadcast_to(col, (col.shape[0], 128))
    return tile if n == 128 else jnp.concatenate([tile] * (n // 128), axis=1)


def _softmax_rows(s, valid):
    n = s.shape[1]
    e = jnp.where(valid, jnp.exp(s - _lanes(jnp.max(s, axis=1, keepdims=True), n)), 0.0)
    l = jnp.sum(e, axis=1, keepdims=True)
    return e * _lanes(1.0 / jnp.maximum(l, 1e-30), n)


def _unselected_blocks(imp_t, qpos, n_sel):
    jj = lax.broadcasted_iota(jnp.int32, imp_t.shape, 0)
    cur = qpos // SEL_BLK
    forced = (jj == 0) | (jj == cur) | (jj == cur - 1)
    sc0 = jnp.where((jj <= cur) & (jj < n_sel), jnp.where(forced, -NEG, imp_t), NEG)
    jjf = jj.astype(f32)

    def pick(_, carry):
        sc, nonsel = carry
        m = jnp.max(sc, axis=0, keepdims=True)
        idx = jnp.min(jnp.where(sc == m, jjf, float(imp_t.shape[0])), axis=0, keepdims=True)
        hit = (jjf == idx) & (m > NEG)
        return jnp.where(hit, NEG, sc), jnp.where(hit, 0.0, nonsel)

    return lax.fori_loop(0, min(TOPK, n_sel), pick, (sc0, jnp.ones(imp_t.shape, f32)))[1]


def _nsa_attn_kernel(q_ref, kct_ref, vc_ref, ks_ref, vs_ref, kw_ref, vw_ref, gate_ref, msel_ref, o_ref,
                     oc_sc, score_sc, nonsel_sc, m_sc, acc_sc, *, n_cmp, n_sel):
    qt = pl.program_id(1)
    s0 = qt * Q_BLOCK
    rq = Q_PER_KV * Q_BLOCK
    row_q = s0 + (lax.broadcasted_iota(jnp.int32, (rq, 1), 0) & (Q_BLOCK - 1))

    cend = lax.broadcasted_iota(jnp.int32, (1, n_cmp), 1) * CMP_STRIDE + (CMP_BLK - 1)
    cvalid = cend <= row_q
    for g in range(N_KV):
        s = jnp.dot(q_ref[0, g, 0], kct_ref[0, g], preferred_element_type=f32)
        p = _softmax_rows(jnp.where(cvalid, s, NEG), cvalid)
        oc_sc[g] = jnp.dot(p.astype(bf16), vc_ref[0, g], preferred_element_type=f32)
        ps = p[0:Q_BLOCK]
        for r in range(1, Q_PER_KV):
            ps = ps + p[r * Q_BLOCK:(r + 1) * Q_BLOCK]
        hi = ps.astype(bf16)
        lo = (ps - hi.astype(f32)).astype(bf16)
        imp = (jnp.dot(hi, msel_ref[...], preferred_element_type=f32)
               + jnp.dot(lo, msel_ref[...], preferred_element_type=f32))
        score_sc[:, g * Q_BLOCK:(g + 1) * Q_BLOCK] = imp.T

    qpos = s0 + (lax.broadcasted_iota(jnp.int32, (n_sel, N_KV * Q_BLOCK), 1) & (Q_BLOCK - 1))
    nonsel = _unselected_blocks(score_sc[...], qpos, n_sel)
    for g in range(N_KV):
        nonsel_sc[g] = nonsel[:, g * Q_BLOCK:(g + 1) * Q_BLOCK].T.astype(bf16)

    n_tiles = (s0 + Q_BLOCK + SEL_TK - 1) // SEL_TK
    key_l = lax.broadcasted_iota(jnp.int32, (1, SEL_TK), 1)
    blk_j = lax.broadcasted_iota(jnp.int32, (n_sel, SEL_TK), 0)
    blk_l = lax.broadcasted_iota(jnp.int32, (n_sel, SEL_TK), 1)
    wkey_l = lax.broadcasted_iota(jnp.int32, (1, Q_BLOCK), 1)
    n_win = WINDOW // Q_BLOCK + 1
    m_sc[...] = jnp.full_like(m_sc, NEG)
    acc_sc[...] = jnp.zeros_like(acc_sc)

    def sel_tile(kt, causal):
        expand = jnp.where(blk_j == (kt * SEL_TK + blk_l) // SEL_BLK, NEG, 0.0).astype(bf16)
        ss = []
        for g in range(N_KV):
            s = jnp.dot(q_ref[0, g, 0], ks_ref[0, g, kt], preferred_element_type=f32)
            bias = jnp.dot(nonsel_sc[g], expand, preferred_element_type=f32)
            s = s + jnp.concatenate([bias] * Q_PER_KV, axis=0)
            if causal:
                s = jnp.where(kt * SEL_TK + key_l <= row_q, s, NEG)
            ss.append(s)
        for g in range(N_KV):
            m_old = m_sc[g]
            m_new = jnp.maximum(m_old, jnp.broadcast_to(jnp.max(ss[g], axis=1, keepdims=True), m_old.shape))
            p = jnp.exp(ss[g] - jnp.concatenate([m_new] * (SEL_TK // 128), axis=1)).astype(bf16)
            acc_sc[g] = jnp.exp(m_old - m_new) * acc_sc[g] + jnp.dot(p, vs_ref[0, g, kt], preferred_element_type=f32)
            m_sc[g] = m_new

    def body(kt, carry):
        sel_tile(kt, False)
        return carry

    lax.fori_loop(0, n_tiles - 1, body, 0)
    sel_tile(n_tiles - 1, True)

    for g in range(N_KV):
        q = q_ref[0, g, 0]
        acc = acc_sc[g]
        o_s = acc[:, 0:HEAD_DIM] * (1.0 / jnp.maximum(acc[:, HEAD_DIM:HEAD_DIM + 1], 1e-30))

        s_parts, w_tiles = [], []
        for i in range(n_win):
            kt = qt - (n_win - 1) + i
            ktc = jnp.maximum(kt, 0)
            kpos = kt * Q_BLOCK + wkey_l
            dpos = row_q - kpos
            ok = (kpos >= 0) & (dpos >= 0) & (dpos < WINDOW)
            s_parts.append(jnp.where(ok, jnp.dot(q, kw_ref[0, g, ktc], preferred_element_type=f32), NEG))
            w_tiles.append(ktc)
        s_w = jnp.concatenate(s_parts, axis=1)
        e_w = jnp.exp(s_w - _lanes(jnp.max(s_w, axis=1, keepdims=True), s_w.shape[1])).astype(bf16)
        acc_w = jnp.dot(e_w[:, 0:Q_BLOCK], vw_ref[0, g, w_tiles[0]], preferred_element_type=f32)
        for i in range(1, n_win):
            acc_w = acc_w + jnp.dot(e_w[:, i * Q_BLOCK:(i + 1) * Q_BLOCK], vw_ref[0, g, w_tiles[i]], preferred_element_type=f32)
        o_w = acc_w[:, 0:HEAD_DIM] * (1.0 / jnp.maximum(acc_w[:, HEAD_DIM:HEAD_DIM + 1], 1e-30))

        o_c = oc_sc[g]
        gates = gate_ref[0]
        for r in range(Q_PER_KV):
            c = (g * Q_PER_KV + r) * 3
            rows = slice(r * Q_BLOCK, (r + 1) * Q_BLOCK)
            o = (gates[:, c:c + 1] * o_c[rows] + gates[:, c + 1:c + 2] * o_s[rows] + gates[:, c + 2:c + 3] * o_w[rows])
            o_ref[0, g, 0, rows, :] = o.astype(bf16)


def _with_ones(v):
    pad = jnp.zeros(v.shape[:-1] + (128 - v.shape[-1] - 1,), v.dtype)
    return jnp.concatenate([v, jnp.ones(v.shape[:-1] + (1,), v.dtype), pad], axis=-1)


def nsa_attention(q, kvb, gates, kct, vc):
    b_, t_, _ = q.shape
    n_qt, n_st, n_sel = t_ // Q_BLOCK, t_ // SEL_TK, t_ // SEL_BLK
    n_cmp = kct.shape[-1]
    rq = Q_PER_KV * Q_BLOCK
    q5 = q.reshape(b_, n_qt, Q_BLOCK, N_KV, Q_PER_KV, HEAD_DIM).transpose(0, 3, 1, 4, 2, 5).reshape(b_, N_KV, n_qt, rq, HEAD_DIM)

    def tiles(x, tk):
        return x.reshape(b_, t_ // tk, tk, N_KV, HEAD_DIM).transpose(0, 3, 1, 2, 4)

    ks = tiles(kvb[..., 0:KD], SEL_TK).transpose(0, 1, 2, 4, 3)
    vs = _with_ones(tiles(kvb[..., KD:2 * KD], SEL_TK))
    kw = tiles(kvb[..., 2 * KD:3 * KD], Q_BLOCK).transpose(0, 1, 2, 4, 3)
    vw = _with_ones(tiles(kvb[..., 3 * KD:4 * KD], Q_BLOCK))
    msel = _cmp_to_sel_matrix(n_cmp, n_sel).astype(bf16)
    per_b = lambda x: pl.BlockSpec((1,) + x.shape[1:], lambda b, t: (b,) + (0,) * (x.ndim - 1), pipeline_mode=pl.Buffered(1))
    qspec = pl.BlockSpec((1, N_KV, 1, rq, HEAD_DIM), lambda b, t: (b, 0, t, 0, 0))
    o5 = pl.pallas_call(
        functools.partial(_nsa_attn_kernel, n_cmp=n_cmp, n_sel=n_sel),
        grid=(b_, n_qt),
        in_specs=[qspec, per_b(kct), per_b(vc), per_b(ks), per_b(vs), per_b(kw), per_b(vw),
                  pl.BlockSpec((1, Q_BLOCK, 128), lambda b, t: (b, t, 0)),
                  pl.BlockSpec(msel.shape, lambda b, t: (0, 0))],
        out_specs=qspec,
        out_shape=jax.ShapeDtypeStruct(q5.shape, bf16),
        scratch_shapes=[pltpu.VMEM((N_KV, rq, HEAD_DIM), f32), pltpu.VMEM((n_sel, N_KV * Q_BLOCK), f32),
                        pltpu.VMEM((N_KV, Q_BLOCK, n_sel), bf16), pltpu.VMEM((N_KV, rq, 128), f32), pltpu.VMEM((N_KV, rq, 128), f32)],
        compiler_params=pltpu.CompilerParams(dimension_semantics=("arbitrary", "arbitrary"),
                                             vmem_limit_bytes=V7X_VMEM_LIMIT_BYTES),
        name="nsa_attention",
    )(q5, kct, vc, ks, vs, kw, vw, gates, msel)
    return o5.reshape(b_, N_KV, n_qt, Q_PER_KV, Q_BLOCK, HEAD_DIM).transpose(0, 2, 4, 1, 3, 5).reshape(b_, t_, N_HEADS * HEAD_DIM)


def _mixer_out_kernel(y_ref, x_ref, gt_ref, w_ref, lg_ref, lb_ref, o_ref):
    f = jnp.dot(y_ref[0], w_ref[...], preferred_element_type=f32)
    y = ALPHA * x_ref[0] + (1.0 + gt_ref[0]) * f
    mu = jnp.mean(y, axis=-1, keepdims=True)
    yc = y - mu
    var = jnp.mean(yc * yc, axis=-1, keepdims=True)
    o_ref[0] = yc * lax.rsqrt(var + LN_EPS) * lg_ref[...] + lb_ref[...]


def mixer_out(y, x, gate, w, ln_g, ln_b):
    b_, t_, d_ = x.shape
    rows = min(PROJ_ROWS, t_)
    vec = pl.BlockSpec((1, d_), lambda b, t: (0, 0))
    return pl.pallas_call(
        _mixer_out_kernel,
        grid=(b_, t_ // rows),
        in_specs=[pl.BlockSpec((1, rows, y.shape[2]), lambda b, t: (b, t, 0)), pl.BlockSpec((1, rows, d_), lambda b, t: (b, t, 0)),
                  _mod_spec(gate, rows), pl.BlockSpec(w.shape, lambda b, t: (0, 0)), vec, vec],
        out_specs=pl.BlockSpec((1, rows, d_), lambda b, t: (b, t, 0)),
        out_shape=jax.ShapeDtypeStruct(x.shape, f32),
        compiler_params=pltpu.CompilerParams(dimension_semantics=("arbitrary", "arbitrary"),
                                             vmem_limit_bytes=V7X_VMEM_LIMIT_BYTES),
        name="mixer_out",
    )(y, x, gate, w, ln_g.reshape(1, d_), ln_b.reshape(1, d_))


def nsa_prompt_layer(x, shift, scale, gate, w_in, w_o, pe, cw1, cb1, cw2, cb2, ln_g, ln_b):
    b_, t_, _ = x.shape
    cos, sin = _rope_tables(jnp.arange(t_), 128)
    q, rows, win, gates, kvb = nsa_project(x, shift, scale, _nsa_proj_weights(w_in), cos, sin)
    n_ch = t_ // CMP_STRIDE
    chunks = rows[..., 0:2 * KD].astype(bf16).reshape(b_, n_ch, CMP_STRIDE, 2, N_KV, HEAD_DIM)
    chunks = chunks.transpose(0, 3, 4, 1, 2, 5).reshape(b_, 2, N_KV, n_ch, CMP_STRIDE * HEAD_DIM)
    cmp_n, cmp_t = compress_kv(chunks, pe, cw1, cb1, cw2, cb2)
    o = nsa_attention(q, kvb, gates, cmp_t[:, 0].astype(bf16), cmp_n[:, 1].astype(bf16))
    x_new = mixer_out(o, x, gate, w_o.astype(bf16), ln_g, ln_b)
    return x_new, rows.reshape(b_, t_, 4, N_KV, HEAD_DIM), win.reshape(b_, t_, 2, N_KV, HEAD_DIM)


PAGES_PER_STEP = 8
S_PAD = 8
CHUNKS_PER_PAGE = PAGE_SIZE // CMP_STRIDE
WIN_KEYS_PAD = 128


def _page_specs(block, tail):
    def spec(i):
        return pl.BlockSpec(block, lambda b, pg, pt: (pt[b, pg * PAGES_PER_STEP + i],) + tail)
    return [spec(i) for i in range(PAGES_PER_STEP)]


def _cmp_pages_kernel(pt_ref, *refs):
    page_refs = refs[:PAGES_PER_STEP]
    perm_ref, w_ref, o_ref = refs[PAGES_PER_STEP:]
    xp = [jnp.dot(perm_ref[...], r[0, 0].astype(bf16), preferred_element_type=f32) for r in page_refs]
    cpp = CHUNKS_PER_PAGE
    xs = [jnp.concatenate([x[s * cpp:(s + 1) * cpp] for x in xp], axis=0) for s in range(CMP_STRIDE)]
    for kv in range(2):
        groups = []
        for g in range(N_KV):
            lo = (kv * N_KV + g) * HEAD_DIM
            groups.append(jnp.concatenate([x[:, lo:lo + HEAD_DIM] for x in xs], axis=1))
        xc = jnp.concatenate(groups, axis=0).astype(bf16)
        o_ref[0, 0, kv] = jnp.dot(xc, w_ref[kv], preferred_element_type=f32)


def cmp_pages(pages, li, page_table, w01):
    b_, n_pages = page_table.shape
    n_pg = n_pages // PAGES_PER_STEP
    rows = PAGES_PER_STEP * N_KV * CHUNKS_PER_PAGE
    pos = jnp.arange(PAGE_SIZE)
    perm = (pos[None, :] == (pos[:, None] % CHUNKS_PER_PAGE) * CMP_STRIDE + pos[:, None] // CHUNKS_PER_PAGE).astype(bf16)
    out = pl.pallas_call(
        _cmp_pages_kernel,
        grid_spec=pltpu.PrefetchScalarGridSpec(
            num_scalar_prefetch=1, grid=(b_, n_pg),
            in_specs=_page_specs((1, 1, PAGE_SIZE, 2 * KD), (li, 0, 0))
            + [pl.BlockSpec(perm.shape, lambda b, pg, pt: (0, 0)), pl.BlockSpec(w01.shape, lambda b, pg, pt: (0, 0, 0))],
            out_specs=pl.BlockSpec((1, 1, 2, rows, w01.shape[2]), lambda b, pg, pt: (b, pg, 0, 0, 0))),
        out_shape=jax.ShapeDtypeStruct((b_, n_pg, 2, rows, w01.shape[2]), f32),
        compiler_params=pltpu.CompilerParams(dimension_semantics=("arbitrary", "arbitrary"),
                                             vmem_limit_bytes=V7X_VMEM_LIMIT_BYTES),
        name="cmp_pages",
    )(page_table, *([pages] * PAGES_PER_STEP), perm, w01)
    out = out.reshape(b_, n_pg, 2, N_KV, PAGES_PER_STEP, CHUNKS_PER_PAGE, w01.shape[2])
    return out.transpose(0, 2, 3, 1, 4, 5, 6).reshape(b_, 2, N_KV, n_pages * CHUNKS_PER_PAGE, w01.shape[2])


def _sample_cmp_win_kernel(q_ref, kct_ref, vc_ref, kw_ref, vw_ref, msel_ref, oc_ref, ow_ref, imp_ref, *, past, w_buf, n_wk):
    rows = Q_PER_KV * S_PAD
    qpos = past + (lax.broadcasted_iota(jnp.int32, (rows, 1), 0) & (S_PAD - 1))
    cend = lax.broadcasted_iota(jnp.int32, (1, kct_ref.shape[3]), 1) * CMP_STRIDE + (CMP_BLK - 1)
    cvalid = cend <= qpos
    widx = lax.broadcasted_iota(jnp.int32, (1, kw_ref.shape[3]), 1)
    kwpos = past - w_buf + widx
    dpos = qpos - kwpos
    wok = (dpos >= 0) & (dpos < WINDOW) & (kwpos >= 0) & (widx < n_wk)
    for g in range(N_KV):
        q = q_ref[0, g]
        s = jnp.dot(q, kct_ref[0, g], preferred_element_type=f32)
        p = _softmax_rows(jnp.where(cvalid, s, NEG), cvalid)
        oc_ref[0, g] = jnp.dot(p.astype(bf16), vc_ref[0, g], preferred_element_type=f32)
        hi = p.astype(bf16)
        lo = (p - hi.astype(f32)).astype(bf16)
        imp = jnp.dot(hi, msel_ref[...], preferred_element_type=f32) + jnp.dot(lo, msel_ref[...], preferred_element_type=f32)
        tot = imp[0:S_PAD]
        for r in range(1, Q_PER_KV):
            tot = tot + imp[r * S_PAD:(r + 1) * S_PAD]
        imp_ref[0, g] = tot
        sw = jnp.where(wok, jnp.dot(q, kw_ref[0, g], preferred_element_type=f32), NEG)
        ew = jnp.exp(sw - _lanes(jnp.max(sw, axis=1, keepdims=True), sw.shape[1])).astype(bf16)
        acc = jnp.dot(ew, vw_ref[0, g], preferred_element_type=f32)
        ow_ref[0, g] = acc[:, 0:HEAD_DIM] * (1.0 / jnp.maximum(acc[:, HEAD_DIM:HEAD_DIM + 1], 1e-30))


def _topk_kernel(imp_ref, o_ref, *, past, n_sel):
    qpos = past + (lax.broadcasted_iota(jnp.int32, imp_ref.shape, 1) & (S_PAD - 1))
    o_ref[...] = _unselected_blocks(imp_ref[...], qpos, n_sel)


def _sample_sel_kernel(pt_ref, *refs, past, n_new):
    page_refs = refs[:PAGES_PER_STEP]
    qbd_ref, nonsel_ref, knew_ref, vnew_ref, os_ref, m_sc, l_sc, acc_sc = refs[PAGES_PER_STEP:]
    pg = pl.program_id(1)
    nt = (((1,), (1,)), ((), ()))

    @pl.when(pg == 0)
    def _():
        m_sc[...] = jnp.full_like(m_sc, NEG)
        l_sc[...] = jnp.zeros_like(l_sc)
        acc_sc[...] = jnp.zeros_like(acc_sc)

    def accumulate(s, v):
        m_old = m_sc[...]
        m_new = jnp.maximum(m_old, jnp.max(s, axis=1, keepdims=True))
        p = jnp.exp(s - m_new)
        alpha = jnp.exp(m_old - m_new)
        l_sc[...] = alpha * l_sc[...] + jnp.sum(p, axis=1, keepdims=True)
        acc_sc[...] = alpha * acc_sc[...] + jnp.dot(p.astype(bf16), v, preferred_element_type=f32)
        m_sc[...] = m_new

    x = jnp.concatenate([r[0, 0] for r in page_refs], axis=0)
    nk = x.shape[0]
    qbd = qbd_ref[0]
    nonsel = nonsel_ref[0]
    s = lax.dot_general(qbd, x[:, 0:KD].astype(bf16), nt, preferred_element_type=f32)
    blk_j = lax.broadcasted_iota(jnp.int32, (nonsel.shape[1], nk), 0)
    key = pg * nk + lax.broadcasted_iota(jnp.int32, (nonsel.shape[1], nk), 1)
    expand = jnp.where(blk_j == key // SEL_BLK, NEG, 0.0).astype(bf16)
    accumulate(s + jnp.dot(nonsel, expand, preferred_element_type=f32), x[:, KD:2 * KD].astype(bf16))

    @pl.when(pg == pl.num_programs(1) - 1)
    def _():
        sn = lax.dot_general(qbd, knew_ref[0], nt, preferred_element_type=f32)
        t_new = lax.broadcasted_iota(jnp.int32, (1, sn.shape[1]), 1)
        s_row = lax.broadcasted_iota(jnp.int32, (sn.shape[0], 1), 0) & (S_PAD - 1)
        new_blk = past // SEL_BLK
        bias_new = nonsel[:, new_blk:new_blk + 1].astype(f32) * NEG
        accumulate(jnp.where((t_new < n_new) & (t_new <= s_row), sn + bias_new, NEG), vnew_ref[0])
        o = acc_sc[...] * (1.0 / jnp.maximum(l_sc[...], 1e-30))
        rows = Q_PER_KV * S_PAD
        for g in range(N_KV):
            os_ref[0, g] = o[g * rows:(g + 1) * rows, g * HEAD_DIM:(g + 1) * HEAD_DIM]


def _gated_out_kernel(oc_ref, os_ref, ow_ref, gc_ref, gs_ref, gw_ref, x_ref, gt_ref, w_ref, lg_ref, lb_ref, o_ref):
    o = gc_ref[0] * oc_ref[0] + gs_ref[0] * os_ref[0] + gw_ref[0] * ow_ref[0]
    f = jnp.dot(o.astype(bf16), w_ref[...], preferred_element_type=f32)
    y = ALPHA * x_ref[0] + (1.0 + gt_ref[0]) * f
    mu = jnp.mean(y, axis=-1, keepdims=True)
    yc = y - mu
    var = jnp.mean(yc * yc, axis=-1, keepdims=True)
    o_ref[0] = yc * lax.rsqrt(var + LN_EPS) * lg_ref[...] + lb_ref[...]


def nsa_sample_layer(x, shift, scale, gate, cache_kv, page_table, li, win_buf, w_in, w_o, pe, cw1, cb1, cw2, cb2, ln_g, ln_b):
    b_, s_, d_ = x.shape
    n_pool, n_pages = cache_kv.shape[0], page_table.shape[1]
    past = n_pages * PAGE_SIZE
    w_buf = win_buf.shape[1]
    n_ch = past // CMP_STRIDE
    n_sel = past // SEL_BLK + 1
    assert past % SEL_BLK == 0 and s_ <= S_PAD and n_pages % PAGES_PER_STEP == 0
    assert (past + s_ - CMP_BLK) // CMP_STRIDE + 2 == n_ch
    rows_n = b_ * s_
    rq = Q_PER_KV * S_PAD
    flat = lambda m: jnp.broadcast_to(m, (b_, s_, d_)).reshape(1, rows_n, d_)
    xf = x.reshape(1, rows_n, d_)
    cos, sin = _rope_tables(past + (jnp.arange(rows_n) % s_), 128)
    q, rows, win, gates, kvb = nsa_project(xf, flat(shift), flat(scale), _nsa_proj_weights(w_in), cos, sin)

    pages = cache_kv.reshape(n_pool, cache_kv.shape[1], PAGE_SIZE, 4 * KD)
    half = cw1.shape[1] // 2
    w01 = jnp.concatenate([cw1[:, :half], cw1[:, half:]], axis=2).astype(bf16)
    cmp_n, cmp_t = compress_kv(cmp_pages(pages, li, page_table, w01), pe, cw1, cb1, cw2, cb2, projected=True)

    q32 = jnp.pad(q.reshape(b_, s_, N_KV, Q_PER_KV, HEAD_DIM), ((0, 0), (0, S_PAD - s_), (0, 0), (0, 0), (0, 0)))
    q32 = q32.transpose(0, 2, 3, 1, 4).reshape(b_, N_KV, rq, HEAD_DIM)
    qbd = (q32[:, :, :, None, :] * jnp.eye(N_KV, dtype=bf16)[None, :, None, :, None]).reshape(b_, N_KV * rq, KD)

    w_all = jnp.concatenate([win_buf, win.reshape(b_, s_, 2, N_KV, HEAD_DIM)], axis=1)
    n_wk = w_buf + s_
    wk_pad = -(-n_wk // WIN_KEYS_PAD) * WIN_KEYS_PAD
    w_pad = jnp.pad(w_all, ((0, 0), (0, wk_pad - n_wk), (0, 0), (0, 0), (0, 0))).astype(bf16)
    kw = w_pad[:, :, 0].transpose(0, 2, 3, 1)
    vw = _with_ones(w_pad[:, :, 1].transpose(0, 2, 1, 3))
    n_sel_pad = 2 * 128
    msel = jnp.pad(_cmp_to_sel_matrix(n_ch, n_sel), ((0, 0), (0, n_sel_pad - n_sel))).astype(bf16)
    per_b = lambda a: pl.BlockSpec((1,) + a.shape[1:], lambda b: (b,) + (0,) * (a.ndim - 1))
    kct, vc = cmp_t[:, 0].astype(bf16), cmp_n[:, 1].astype(bf16)
    o_shape = jax.ShapeDtypeStruct((b_, N_KV, rq, HEAD_DIM), f32)
    o_spec = pl.BlockSpec((1, N_KV, rq, HEAD_DIM), lambda b: (b, 0, 0, 0))
    oc, ow, imp = pl.pallas_call(
        functools.partial(_sample_cmp_win_kernel, past=past, w_buf=w_buf, n_wk=n_wk),
        grid=(b_,),
        in_specs=[per_b(q32), per_b(kct), per_b(vc), per_b(kw), per_b(vw), pl.BlockSpec(msel.shape, lambda b: (0, 0))],
        out_specs=[o_spec, o_spec, pl.BlockSpec((1, N_KV, S_PAD, n_sel_pad), lambda b: (b, 0, 0, 0))],
        out_shape=[o_shape, o_shape, jax.ShapeDtypeStruct((b_, N_KV, S_PAD, n_sel_pad), f32)],
        compiler_params=pltpu.CompilerParams(dimension_semantics=("arbitrary",)),
        name="sample_cmp_win",
    )(q32, kct, vc, kw, vw, msel)

    sel_rows = -(-n_sel // 8) * 8
    imp_t = imp[..., :sel_rows].transpose(3, 0, 1, 2).reshape(sel_rows, b_ * N_KV * S_PAD)
    nonsel_t = pl.pallas_call(
        functools.partial(_topk_kernel, past=past, n_sel=n_sel),
        out_shape=jax.ShapeDtypeStruct(imp_t.shape, f32),
        name="sample_topk",
    )(imp_t)
    nonsel = nonsel_t.reshape(sel_rows, b_, N_KV, 1, S_PAD).transpose(1, 2, 3, 4, 0)
    nonsel = jnp.broadcast_to(nonsel, (b_, N_KV, Q_PER_KV, S_PAD, sel_rows)).reshape(b_, N_KV * rq, sel_rows)
    nonsel = jnp.pad(nonsel, ((0, 0), (0, 0), (0, n_sel_pad - sel_rows)), constant_values=1.0).astype(bf16)

    new_pad = lambda a: jnp.pad(a.reshape(b_, s_, KD), ((0, 0), (0, 128 - s_), (0, 0)))
    knew, vnew = new_pad(kvb[..., 0:KD]), new_pad(kvb[..., KD:2 * KD])
    rows_all = N_KV * rq
    per_b2 = lambda a: pl.BlockSpec((1,) + a.shape[1:], lambda b, pg, pt: (b,) + (0,) * (a.ndim - 1))
    o_s = pl.pallas_call(
        functools.partial(_sample_sel_kernel, past=past, n_new=s_),
        grid_spec=pltpu.PrefetchScalarGridSpec(
            num_scalar_prefetch=1, grid=(b_, n_pages // PAGES_PER_STEP),
            in_specs=_page_specs((1, 1, PAGE_SIZE, 2 * KD), (li, 0, 1)) + [per_b2(qbd), per_b2(nonsel), per_b2(knew), per_b2(vnew)],
            out_specs=pl.BlockSpec((1, N_KV, rq, HEAD_DIM), lambda b, pg, pt: (b, 0, 0, 0)),
            scratch_shapes=[pltpu.VMEM((rows_all, 1), f32), pltpu.VMEM((rows_all, 1), f32), pltpu.VMEM((rows_all, KD), f32)]),
        out_shape=o_shape,
        compiler_params=pltpu.CompilerParams(dimension_semantics=("arbitrary", "arbitrary"),
                                             vmem_limit_bytes=V7X_VMEM_LIMIT_BYTES),
        name="sample_sel",
    )(page_table, *([pages] * PAGES_PER_STEP), qbd, nonsel, knew, vnew)

    tok = lambda o: o.reshape(b_, N_KV, Q_PER_KV, S_PAD, HEAD_DIM)[:, :, :, :s_].transpose(0, 3, 1, 2, 4).reshape(1, rows_n, N_HEADS * HEAD_DIM)
    gexp = lambda br: jnp.repeat(gates[..., br:N_GATE:3], HEAD_DIM, axis=-1)
    full = lambda n: pl.BlockSpec((1, rows_n, n), lambda i: (0, 0, 0))
    vec = pl.BlockSpec((1, d_), lambda i: (0, 0))
    x_new = pl.pallas_call(
        _gated_out_kernel,
        grid=(1,),
        in_specs=[full(N_HEADS * HEAD_DIM)] * 6 + [full(d_), full(d_), pl.BlockSpec(w_o.shape, lambda i: (0, 0)), vec, vec],
        out_specs=full(d_),
        out_shape=jax.ShapeDtypeStruct((1, rows_n, d_), f32),
        name="sample_gated_out",
    )(tok(oc), tok(o_s), tok(ow), gexp(0), gexp(1), gexp(2), xf, flat(gate), w_o.astype(bf16), ln_g.reshape(1, d_), ln_b.reshape(1, d_))
    return x_new.reshape(b_, s_, d_), rows.reshape(b_, s_, 4, N_KV, HEAD_DIM), w_all[:, n_wk - w_buf:]


HPG = SSM_HEADS // SSM_GROUPS
GN = SSM_GROUPS * D_STATE
GW = D_INNER // SSM_GROUPS
DT_PAD = 128
CONV_KEEP = 8


def _ssd_proj_kernel(x_ref, sh_ref, sc_ref, w_ref, z_ref, xbc_ref, dt_ref):
    hb = (x_ref[0] * (1.0 + sc_ref[0]) + sh_ref[0]).astype(bf16)
    z_ref[0] = jnp.dot(hb, w_ref[:, 0:D_INNER], preferred_element_type=f32)
    xbc_ref[0] = jnp.dot(hb, w_ref[:, D_INNER:D_INNER + CONV_DIM], preferred_element_type=f32)
    dt_ref[0] = jnp.dot(hb, w_ref[:, D_INNER + CONV_DIM:], preferred_element_type=f32)


def ssd_project(x, shift, scale, w_in):
    b_, t_, d_ = x.shape
    rows = min(PROJ_ROWS, t_)
    w_all = jnp.pad(w_in, ((0, 0), (0, DT_PAD - SSM_HEADS))).astype(bf16)
    tile = lambda n: pl.BlockSpec((1, rows, n), lambda b, t: (b, t, 0))
    mod = _mod_spec(shift, rows)
    return pl.pallas_call(
        _ssd_proj_kernel,
        grid=(b_, t_ // rows),
        in_specs=[tile(d_), mod, mod, pl.BlockSpec(w_all.shape, lambda b, t: (0, 0), pipeline_mode=pl.Buffered(1))],
        out_specs=[tile(D_INNER), tile(CONV_DIM), tile(DT_PAD)],
        out_shape=[jax.ShapeDtypeStruct((b_, t_, D_INNER), f32), jax.ShapeDtypeStruct((b_, t_, CONV_DIM), f32),
                   jax.ShapeDtypeStruct((b_, t_, DT_PAD), f32)],
        compiler_params=pltpu.CompilerParams(dimension_semantics=("arbitrary", "arbitrary"),
                                             vmem_limit_bytes=V7X_VMEM_LIMIT_BYTES),
        name="ssd_project",
    )(x, shift, scale, w_all)


def _split3(v):
    p1 = v.astype(bf16)
    r1 = v - p1.astype(f32)
    p2 = r1.astype(bf16)
    p3 = (r1 - p2.astype(f32)).astype(bf16)
    return p1, p2, p3


def _dot3(parts, m, left):
    out = None
    for p in parts:
        t = jnp.dot(m, p, preferred_element_type=f32) if left else jnp.dot(p, m, preferred_element_type=f32)
        out = t if out is None else out + t
    return out


def _ssd_scan_kernel(xbc_ref, dt_ref, z_ref, cst_ref, h0_ref, cw_ref, cb_ref, dtb_ref, a_ref, d_ref, ng_ref, ex_ref,
                     y_ref, ht_ref, win_sc, st_sc, *, n_valid):
    c = pl.program_id(1)
    L = SSM_CHUNK

    @pl.when(c == 0)
    def _():
        win_sc[0:CONV_KEEP, :] = cst_ref[0]
        st_sc[...] = h0_ref[0]

    win_sc[CONV_KEEP:CONV_KEEP + L, :] = xbc_ref[0]
    acc = cb_ref[...] + win_sc[pl.ds(CONV_KEEP, L), :] * cw_ref[CONV_W - 1:CONV_W, :]
    for k in range(CONV_W - 1):
        acc = acc + win_sc[pl.ds(CONV_KEEP - (CONV_W - 1) + k, L), :] * cw_ref[k:k + 1, :]
    win_sc[0:CONV_KEEP, :] = win_sc[L:L + CONV_KEEP, :]
    xbc = acc * jax.nn.sigmoid(acc)
    xs = xbc[:, 0:D_INNER]

    t_row = lax.broadcasted_iota(jnp.int32, (L, 1), 0)
    dt_in = dt_ref[0] + dtb_ref[...]
    dt = jnp.maximum(dt_in, 0.0) + jnp.log1p(jnp.exp(-jnp.abs(dt_in)))
    dt = jnp.where(t_row < n_valid, dt, 0.0)
    a = dt * a_ref[...]
    ii = lax.broadcasted_iota(jnp.int32, (L, L), 0)
    jj = lax.broadcasted_iota(jnp.int32, (L, L), 1)
    lower = ii >= jj
    tri = jnp.where(lower, 1.0, 0.0).astype(bf16)
    a_cs = _dot3(_split3(a), tri, left=True)
    a_cs_t = a_cs.T
    ex = ex_ref[...]
    dt_e = _dot3(_split3(dt), ex, left=False)
    acs_e = _dot3(_split3(a_cs), ex, left=False)
    a_tot_e = acs_e[L - 1:L, :]
    xd = xs * dt_e
    xdd = (xd * jnp.exp(a_tot_e - acs_e)).astype(bf16)
    xdb = xd.astype(bf16)
    grow = jnp.exp(acs_e)
    lane_lo = lax.broadcasted_iota(jnp.int32, (L, 2 * SSM_HEAD_DIM), 1) < SSM_HEAD_DIM
    nt = (((1,), (1,)), ((), ()))

    for g in range(SSM_GROUPS):
        gl = slice(g * GW, (g + 1) * GW)
        bm = xbc[:, D_INNER + g * D_STATE:D_INNER + (g + 1) * D_STATE]
        cm = xbc[:, D_INNER + GN + g * D_STATE:D_INNER + GN + (g + 1) * D_STATE].astype(bf16)
        cb = lax.dot_general(cm, bm.astype(bf16), nt, preferred_element_type=f32)
        st_old = st_sc[:, gl]
        y_g = jnp.dot(cm, st_old.astype(bf16), preferred_element_type=f32) * grow[:, gl]
        st_sc[:, gl] = jnp.exp(a_tot_e[:, gl]) * st_old + jnp.dot(bm.T.astype(bf16), xdd[:, gl], preferred_element_type=f32)
        pairs = []
        for k in range(HPG // 2):
            h0 = g * HPG + 2 * k
            ms = []
            for h in (h0, h0 + 1):
                seg = a_cs[:, h:h + 1] - a_cs_t[h:h + 1, :]
                ms.append((cb * jnp.where(lower, jnp.exp(seg), 0.0)).astype(bf16))
            xp = xdb[:, h0 * SSM_HEAD_DIM:(h0 + 2) * SSM_HEAD_DIM]
            pairs.append(jnp.where(lane_lo, jnp.dot(ms[0], xp, preferred_element_type=f32),
                                   jnp.dot(ms[1], xp, preferred_element_type=f32)))
        y_g = y_g + jnp.concatenate(pairs, axis=1) + d_ref[:, gl] * xs[:, gl]
        zg = z_ref[0, :, gl]
        y_g = y_g * (zg * jax.nn.sigmoid(zg))
        y_g = y_g * lax.rsqrt(jnp.mean(y_g * y_g, axis=-1, keepdims=True) + LN_EPS)
        y_ref[0, :, gl] = (y_g * ng_ref[:, gl]).astype(bf16)

    @pl.when(c == pl.num_programs(1) - 1)
    def _():
        ht_ref[0] = st_sc[...]


def ssd_scan(xbc, dt_raw, z, conv_state, h0, conv_w, conv_b, dt_bias, a_log, d_skip, norm_g, n_valid):
    b_, t_, _ = xbc.shape
    L = SSM_CHUNK
    cst = jnp.pad(conv_state, ((0, 0), (CONV_KEEP - (CONV_W - 1), 0), (0, 0)))
    h0t = h0.transpose(0, 3, 1, 2).reshape(b_, D_STATE, D_INNER)
    vec = lambda v: jnp.pad(v, (0, DT_PAD - SSM_HEADS)).reshape(1, DT_PAD)
    ex = (jnp.arange(DT_PAD)[:, None] == (jnp.arange(D_INNER) // SSM_HEAD_DIM)[None, :]).astype(bf16)
    tile = lambda n: pl.BlockSpec((1, L, n), lambda b, c: (b, c, 0))
    per_b = lambda r, n: pl.BlockSpec((1, r, n), lambda b, c: (b, 0, 0))
    const = lambda r, n: pl.BlockSpec((r, n), lambda b, c: (0, 0))
    y, ht = pl.pallas_call(
        functools.partial(_ssd_scan_kernel, n_valid=n_valid),
        grid=(b_, t_ // L),
        in_specs=[tile(CONV_DIM), tile(DT_PAD), tile(D_INNER), per_b(CONV_KEEP, CONV_DIM), per_b(D_STATE, D_INNER),
                  const(CONV_W, CONV_DIM), const(1, CONV_DIM), const(1, DT_PAD), const(1, DT_PAD), const(1, D_INNER),
                  const(1, D_INNER), const(DT_PAD, D_INNER)],
        out_specs=[tile(D_INNER), per_b(D_STATE, D_INNER)],
        out_shape=[jax.ShapeDtypeStruct((b_, t_, D_INNER), bf16), jax.ShapeDtypeStruct((b_, D_STATE, D_INNER), f32)],
        scratch_shapes=[pltpu.VMEM((L + CONV_KEEP, CONV_DIM), f32), pltpu.VMEM((D_STATE, D_INNER), f32)],
        compiler_params=pltpu.CompilerParams(dimension_semantics=("arbitrary", "arbitrary"),
                                             vmem_limit_bytes=V7X_VMEM_LIMIT_BYTES),
        name="ssd_scan",
    )(xbc, dt_raw, z, cst, h0t, conv_w, conv_b.reshape(1, CONV_DIM), vec(dt_bias), vec(-jnp.exp(a_log)),
      jnp.repeat(d_skip, SSM_HEAD_DIM).reshape(1, D_INNER), norm_g.reshape(1, D_INNER), ex)
    return y, ht.reshape(b_, D_STATE, SSM_HEADS, SSM_HEAD_DIM).transpose(0, 2, 3, 1)


def ssd_layer(x, shift, scale, gate, conv_state, ssm_state, w_in, conv_w, conv_b, dt_bias, a_log, d_skip, norm_g, w_out,
              ln_g, ln_b):
    b_, t_, d_ = x.shape
    fold = t_ < SSM_CHUNK
    if fold:
        flat = lambda m: jnp.broadcast_to(m, (b_, t_, d_)).reshape(1, b_ * t_, d_)
        xf, shift, scale, gate = x.reshape(1, b_ * t_, d_), flat(shift), flat(scale), flat(gate)
    else:
        xf = x
    z, xbc, dt_raw = ssd_project(xf, shift, scale, w_in)
    if fold:
        padt = lambda v: jnp.pad(v.reshape(b_, t_, -1), ((0, 0), (0, SSM_CHUNK - t_), (0, 0)))
        z, xbc, dt_raw = padt(z), padt(xbc), padt(dt_raw)
    y, new_ssm = ssd_scan(xbc, dt_raw, z, conv_state, ssm_state, conv_w, conv_b, dt_bias, a_log, d_skip, norm_g, t_)
    new_conv = jnp.concatenate([conv_state, xbc[:, :t_]], axis=1)[:, t_:]
    if fold:
        y = y[:, :t_].reshape(1, b_ * t_, D_INNER)
    x_new = mixer_out(y, xf, gate, w_out.astype(bf16), ln_g, ln_b).reshape(b_, t_, d_)
    return x_new, new_conv, new_ssm


def _layer_norm(x, g, b):
    mu = x.mean(-1, keepdims=True)
    var = jnp.square(x - mu).mean(-1, keepdims=True)
    return (x - mu) * lax.rsqrt(var + LN_EPS) * g + b


def _rope(x, pos):
    half = HEAD_DIM // 2
    inv = ROPE_THETA ** (-jnp.arange(half, dtype=f32) / half)
    ang = pos.astype(f32)[:, None] * inv
    cos, sin = jnp.cos(ang)[:, None, :], jnp.sin(ang)[:, None, :]
    x1, x2 = x[..., :half], x[..., half:]
    return jnp.concatenate([x1 * cos - x2 * sin, x2 * cos + x1 * sin], -1)


def _masked_softmax(s, mask):
    s = jnp.where(mask, s, -jnp.inf)
    m = jnp.max(s, axis=-1, keepdims=True)
    e = jnp.exp(s - jnp.where(jnp.isfinite(m), m, 0.0))
    return e / jnp.maximum(jnp.sum(e, axis=-1, keepdims=True), 1e-30)


def _compress(kv, pe, w1, b1, w2, b2):
    b_, l_, g_, d_ = kv.shape
    n_sub = CMP_BLK // CMP_STRIDE
    n_c = (l_ - CMP_BLK) // CMP_STRIDE + 1
    n_ch = n_c + n_sub - 1
    ch = kv[:, :n_ch * CMP_STRIDE].reshape(b_, n_ch, CMP_STRIDE, g_, d_)
    ch = jnp.moveaxis(ch, 3, 2).reshape(b_, n_ch, g_, CMP_STRIDE * d_)
    w1s = w1.reshape(n_sub, CMP_STRIDE * d_, CMP_HIDDEN)
    proj = jnp.einsum('bngf,jfh->jbngh', ch, w1s)
    pre = sum(proj[j][:, j:j + n_c] for j in range(n_sub)) + pe.reshape(-1) @ w1 + b1
    return jax.nn.gelu(pre) @ w2 + b2


def _compressed_kv(k_raw, v_raw, pe, w1, b1, w2, b2):
    kc = _compress(k_raw, pe[0], w1[0], b1[0], w2[0], b2[0])
    vc = _compress(v_raw, pe[1], w1[1], b1[1], w2[1], b2[1])
    cend = jnp.arange(kc.shape[1]) * CMP_STRIDE + CMP_BLK - 1
    return _rope(kc, cend), vc, cend


def _cmp_to_sel(n_c, n_sel):
    cs = jnp.arange(n_c)[:, None] * CMP_STRIDE
    ss = jnp.arange(n_sel)[None, :] * SEL_BLK
    ov = jnp.minimum(cs + CMP_BLK, ss + SEL_BLK) - jnp.maximum(cs, ss)
    return jnp.clip(ov, 0, None).astype(f32) / CMP_BLK


def _select_blocks(p_cmp, qpos, n_sel):
    imp = jnp.einsum('bqgc,cs->bqgs', p_cmp, _cmp_to_sel(p_cmp.shape[-1], n_sel))
    j = jnp.arange(n_sel)[None, :]
    cur = (qpos // SEL_BLK)[:, None]
    valid = (j * SEL_BLK <= qpos[:, None])[:, None, :]
    forced = ((j == 0) | (j == cur) | (j == cur - 1))[:, None, :]
    score = jnp.where(valid, jnp.where(forced, jnp.inf, imp), -jnp.inf)
    top, idx = lax.top_k(score, min(TOPK, n_sel))
    return idx, top > -jnp.inf


def _nsa_core(q, qpos, kc, vc, cend, n_sel, fetch_sel, kw, vw, kwpos, gates):
    qf = q * HEAD_DIM ** -0.5
    s = jnp.einsum('bqgrd,bcgd->bqgrc', qf, kc)
    pc = _masked_softmax(s, (cend[None, :] <= qpos[:, None])[None, :, None, None, :])
    o_c = jnp.einsum('bqgrc,bcgd->bqgrd', pc, vc)
    idx, ok = _select_blocks(pc.sum(3), qpos, n_sel)
    ks, vs = fetch_sel(idx)
    b_, q_, g_, k_ = idx.shape
    kpos = idx[..., None] * SEL_BLK + jnp.arange(SEL_BLK)
    ms = (ok[..., None] & (kpos <= qpos[None, :, None, None, None])).reshape(b_, q_, g_, 1, k_ * SEL_BLK)
    ks = ks.reshape(b_, q_, g_, k_ * SEL_BLK, HEAD_DIM)
    vs = vs.reshape(b_, q_, g_, k_ * SEL_BLK, HEAD_DIM)
    ps = _masked_softmax(jnp.einsum('bqgrd,bqgkd->bqgrk', qf, ks), ms)
    o_s = jnp.einsum('bqgrk,bqgkd->bqgrd', ps, vs)
    dpos = qpos[:, None] - kwpos[None, :]
    mw = ((dpos >= 0) & (dpos < WINDOW) & (kwpos >= 0)[None, :])[None, :, None, None, :]
    pw = _masked_softmax(jnp.einsum('bqgrd,bkgd->bqgrk', qf, kw), mw)
    o_w = jnp.einsum('bqgrk,bkgd->bqgrd', pw, vw)
    return gates[..., 0:1] * o_c + gates[..., 1:2] * o_s + gates[..., 2:3] * o_w


def _nsa_project(h, w_in, pos):
    b_, t_, _ = h.shape
    qd, kd = N_HEADS * HEAD_DIM, N_KV * HEAD_DIM
    proj = h @ w_in
    q = _rope(proj[..., :qd].reshape(b_, t_, N_HEADS, HEAD_DIM), pos).reshape(b_, t_, N_KV, Q_PER_KV, HEAD_DIM)
    kv = proj[..., qd:qd + 6 * kd].reshape(b_, t_, 6, N_KV, HEAD_DIM)
    rows = jnp.stack([kv[:, :, 0], kv[:, :, 1], _rope(kv[:, :, 2], pos), kv[:, :, 3]], axis=2)
    win = jnp.stack([_rope(kv[:, :, 4], pos), kv[:, :, 5]], axis=2)
    gates = jax.nn.sigmoid(proj[..., qd + 6 * kd:]).reshape(b_, t_, N_KV, Q_PER_KV, 3)
    return q, rows, win, gates


def _nsa_prompt(h, w_in, w_o, pe, cw1, cb1, cw2, cb2):
    b_, t_, _ = h.shape
    q, rows, win, gates = _nsa_project(h, w_in, jnp.arange(t_))
    kc, vc, cend = _compressed_kv(rows[:, :, 0], rows[:, :, 1], pe, cw1, cb1, cw2, cb2)
    n_sel = t_ // SEL_BLK
    k_blk = rows[:, :, 2].reshape(b_, n_sel, SEL_BLK, N_KV, HEAD_DIM)
    v_blk = rows[:, :, 3].reshape(b_, n_sel, SEL_BLK, N_KV, HEAD_DIM)
    bi = jnp.arange(b_)[:, None, None, None]
    gi = jnp.arange(N_KV)[None, None, :, None]

    def fetch(idx):
        return k_blk[bi, idx, :, gi], v_blk[bi, idx, :, gi]

    w_pad = jnp.pad(win, ((0, 0), (WINDOW, 0), (0, 0), (0, 0), (0, 0)))

    def block(i):
        s0 = i * Q_BLOCK
        qb = lax.dynamic_slice_in_dim(q, s0, Q_BLOCK, axis=1)
        gb = lax.dynamic_slice_in_dim(gates, s0, Q_BLOCK, axis=1)
        wb = lax.dynamic_slice_in_dim(w_pad, s0, Q_BLOCK + WINDOW, axis=1)
        qpos = s0 + jnp.arange(Q_BLOCK)
        kwpos = s0 - WINDOW + jnp.arange(Q_BLOCK + WINDOW)
        return _nsa_core(qb, qpos, kc, vc, cend, n_sel, fetch, wb[:, :, 0], wb[:, :, 1], kwpos, gb)

    o = lax.map(block, jnp.arange(t_ // Q_BLOCK))
    o = jnp.moveaxis(o, 0, 1).reshape(b_, t_, N_HEADS * HEAD_DIM)
    return o @ w_o, (rows, win[:, t_ - min(WINDOW, t_):])


def _nsa_sample(h, past_len, cache_kv, page_table, li, win_buf, w_in, w_o, pe, cw1, cb1, cw2, cb2):
    b_, s_, _ = h.shape
    pos = past_len + jnp.arange(s_)
    q, rows, win, gates = _nsa_project(h, w_in, pos)
    cmp_pages = cache_kv[:, li, :, :2].reshape(cache_kv.shape[0], -1)
    past = jnp.take(cmp_pages, page_table.reshape(-1), axis=0).reshape(b_, -1, 2, N_KV, HEAD_DIM)
    k_raw = jnp.concatenate([past[:, :, 0], rows[:, :, 0]], axis=1)
    v_raw = jnp.concatenate([past[:, :, 1], rows[:, :, 1]], axis=1)
    kc, vc, cend = _compressed_kv(k_raw, v_raw, pe, cw1, cb1, cw2, cb2)
    n_past_blk = past_len // SEL_BLK
    n_new_blk = -(-s_ // SEL_BLK)
    n_sel = n_past_blk + n_new_blk
    bpp = PAGE_SIZE // SEL_BLK
    new_sel = jnp.pad(rows[:, :, 2:4], ((0, 0), (0, n_new_blk * SEL_BLK - s_), (0, 0), (0, 0), (0, 0)))
    new_k = new_sel[:, :, 0].reshape(b_, n_new_blk, SEL_BLK, N_KV, HEAD_DIM)
    new_v = new_sel[:, :, 1].reshape(b_, n_new_blk, SEL_BLK, N_KV, HEAD_DIM)
    bi = jnp.arange(b_)[:, None, None, None]
    gi = jnp.arange(N_KV)[None, None, :, None]

    def fetch(idx):
        jp = jnp.minimum(idx, n_past_blk - 1)
        phys = page_table[bi, jp // bpp][..., None]
        rr = (jp % bpp)[..., None] * SEL_BLK + jnp.arange(SEL_BLK)
        pk = cache_kv[phys, li, rr, 2, gi[..., None]]
        pv = cache_kv[phys, li, rr, 3, gi[..., None]]
        jn = jnp.clip(idx - n_past_blk, 0, n_new_blk - 1)
        is_past = (idx < n_past_blk)[..., None, None]
        return (jnp.where(is_past, pk, new_k[bi, jn, :, gi]), jnp.where(is_past, pv, new_v[bi, jn, :, gi]))

    w_buf = win_buf.shape[1]
    w_all = jnp.concatenate([win_buf, win], axis=1)
    kwpos = past_len - w_buf + jnp.arange(w_buf + s_)
    o = _nsa_core(q, pos, kc, vc, cend, n_sel, fetch, w_all[:, :, 0], w_all[:, :, 1], kwpos, gates)
    o = o.reshape(b_, s_, N_HEADS * HEAD_DIM)
    return o @ w_o, (rows, w_all[:, -w_buf:])


def _segsum(a):
    t_ = a.shape[-1]
    ii = jnp.arange(t_)
    x = jnp.where(ii[:, None] > ii[None, :], jnp.broadcast_to(a[..., :, None], a.shape + (t_,)), 0.0)
    return jnp.where(ii[:, None] >= ii[None, :], jnp.cumsum(x, axis=-2), -jnp.inf)


def _ssd_scan(x, dt, a_head, bm, cm, h0):
    b_, t_, g_, r_, p_ = x.shape
    n_ = bm.shape[-1]
    lc = min(SSM_CHUNK, t_)
    nc = -(-t_ // lc)
    pad = nc * lc - t_

    def padt(z):
        return jnp.pad(z, [(0, 0), (0, pad)] + [(0, 0)] * (z.ndim - 2))

    xd = padt(x * dt[..., None]).reshape(b_, nc, lc, g_, r_, p_)
    a = jnp.moveaxis(padt(dt * a_head).reshape(b_, nc, lc, g_, r_), (1, 2), (3, 4))
    bc = padt(bm).reshape(b_, nc, lc, g_, n_)
    cc = padt(cm).reshape(b_, nc, lc, g_, n_)
    a_cs = jnp.cumsum(a, axis=-1)
    lmat = jnp.exp(_segsum(a))
    cb = jnp.einsum('bclgn,bcsgn->bgcls', cc, bc)
    y_diag = jnp.einsum('bgrcls,bcsgrp->bclgrp', cb[:, :, None] * lmat, xd)
    decay = jnp.exp(a_cs[..., -1:] - a_cs)
    states = jnp.einsum('bclgn,bgrcl,bclgrp->bcgrpn', bc, decay, xd)
    states = jnp.concatenate([h0[:, None], states], axis=1)
    chunk_a = jnp.pad(a_cs[..., -1], [(0, 0)] * 3 + [(1, 0)])
    states = jnp.einsum('bgrzc,bcgrpn->bzgrpn', jnp.exp(_segsum(chunk_a)), states)
    y_off = jnp.einsum('bclgn,bcgrpn,bgrcl->bclgrp', cc, states[:, :-1], jnp.exp(a_cs))
    y = (y_diag + y_off).reshape(b_, nc * lc, g_, r_, p_)[:, :t_]
    return y, states[:, -1]


def _ssd_mixer(h, conv_state, ssm_state, w_in, conv_w, conv_b, dt_bias, a_log, d_skip, norm_g, w_out):
    b_, t_, _ = h.shape
    hpg = SSM_HEADS // SSM_GROUPS
    gn = SSM_GROUPS * D_STATE
    proj = h @ w_in
    z = proj[..., :D_INNER]
    xbc = proj[..., D_INNER:D_INNER + CONV_DIM]
    dt_raw = proj[..., D_INNER + CONV_DIM:]
    full = jnp.concatenate([conv_state, xbc], axis=1)
    xbc = jax.nn.silu(sum(full[:, k:k + t_] * conv_w[k] for k in range(CONV_W)) + conv_b)
    xs = xbc[..., :D_INNER].reshape(b_, t_, SSM_GROUPS, hpg, SSM_HEAD_DIM)
    bm = xbc[..., D_INNER:D_INNER + gn].reshape(b_, t_, SSM_GROUPS, D_STATE)
    cm = xbc[..., D_INNER + gn:].reshape(b_, t_, SSM_GROUPS, D_STATE)
    dt = jax.nn.softplus(dt_raw + dt_bias).reshape(b_, t_, SSM_GROUPS, hpg)
    a_head = -jnp.exp(a_log).reshape(SSM_GROUPS, hpg)
    h0 = ssm_state.reshape(b_, SSM_GROUPS, hpg, SSM_HEAD_DIM, D_STATE)
    y, h_t = _ssd_scan(xs, dt, a_head, bm, cm, h0)
    y = y + d_skip.reshape(SSM_GROUPS, hpg, 1) * xs
    y = y.reshape(b_, t_, SSM_GROUPS, D_INNER // SSM_GROUPS) * jax.nn.silu(z).reshape(b_, t_, SSM_GROUPS, D_INNER // SSM_GROUPS)
    y = y * lax.rsqrt(jnp.mean(y * y, axis=-1, keepdims=True) + LN_EPS)
    y = y.reshape(b_, t_, D_INNER) * norm_g
    new_ssm = h_t.reshape(b_, SSM_HEADS, SSM_HEAD_DIM, D_STATE)
    return y @ w_out, (full[:, t_:], new_ssm)


ADA_COLS = 1152


def _adaln_kernel(c_ref, w_ref, b_ref, o_ref):
    c = c_ref[...]
    h = (c * jax.nn.sigmoid(c)).astype(bf16)
    o_ref[...] = jnp.dot(h, w_ref[...].astype(bf16), preferred_element_type=f32) + b_ref[...]


def adaln(c, w, b):
    r_, d_ = c.shape
    n_ = w.shape[1]
    return pl.pallas_call(
        _adaln_kernel,
        grid=(n_ // ADA_COLS,),
        in_specs=[pl.BlockSpec((r_, d_), lambda n: (0, 0)), pl.BlockSpec((d_, ADA_COLS), lambda n: (0, n)),
                  pl.BlockSpec((1, ADA_COLS), lambda n: (0, n))],
        out_specs=pl.BlockSpec((r_, ADA_COLS), lambda n: (0, n)),
        out_shape=jax.ShapeDtypeStruct((r_, n_), f32),
        compiler_params=pltpu.CompilerParams(dimension_semantics=("arbitrary",)),
        name="adaln",
    )(c, w, b.reshape(1, n_))


def _trunk_layer(x, m, i, mixer, ln_g, ln_b, ffn_a, ffn_b):
    b_, t_, d_ = x.shape
    m = m.reshape(b_, 3, 3, 1, D_MODEL)

    def ffn(v, s, w):
        mods = [m[:, s, k] for k in range(3)]
        if t_ < 8:
            mods = [jnp.broadcast_to(z, (b_, t_, d_)).reshape(1, b_ * t_, d_) for z in mods]
            return ffn_sublayer(v.reshape(1, b_ * t_, d_), *mods, *w, ln_g[i, s], ln_b[i, s]).reshape(b_, t_, d_)
        return ffn_sublayer(v, *mods, *w, ln_g[i, s], ln_b[i, s])

    x = ffn(x, 0, ffn_a)
    x, st = mixer(x, m[:, 1, 0], m[:, 1, 1], m[:, 1, 2], ln_g[i, 1], ln_b[i, 1])
    x = ffn(x, 2, ffn_b)
    return x, st


def kernel(x_prompt, x_sample, cache_kv, cache_win, state_conv, state_ssm, page_table, c_prompt, c_sample, ada_w, ada_b, ln_g, ln_b, ffn_w1, ffn_w3, ffn_w2, nsa_w_in, nsa_w_o, nsa_cmp_pe, nsa_cmp_w1, nsa_cmp_b1, nsa_cmp_w2, nsa_cmp_b2, ssm_w_in, ssm_conv_w, ssm_conv_b, ssm_dt_bias, ssm_a_log, ssm_d, ssm_norm_g, ssm_w_out):
    xp, xs = x_prompt, x_sample
    nb_p, nb_s = c_prompt.shape[0], c_sample.shape[0]
    c_all = jnp.concatenate([c_prompt, c_sample], axis=0)
    c_all = jnp.pad(c_all, ((0, -(nb_p + nb_s) % 8), (0, 0)))
    kv_p, kv_s, win_p, win_s, conv_p, conv_s, ssm_p, ssm_s = [], [], [], [], [], [], [], []
    for i in range(DEPTH):
        j = i // N_MIXERS
        ffn_a = _chunk_ffn_weights(ffn_w1[i, 0], ffn_w3[i, 0], ffn_w2[i, 0])
        ffn_b = _chunk_ffn_weights(ffn_w1[i, 1], ffn_w3[i, 1], ffn_w2[i, 1])
        m_all = adaln(c_all, ada_w[i], ada_b[i])
        m_p, m_s = m_all[:nb_p], m_all[nb_p:nb_p + nb_s]
        lw = (ln_g, ln_b, ffn_a, ffn_b)
        if i % N_MIXERS == 0:
            nw = (nsa_w_in[j], nsa_w_o[j], nsa_cmp_pe[j], nsa_cmp_w1[j], nsa_cmp_b1[j], nsa_cmp_w2[j], nsa_cmp_b2[j])
            def prompt_mixer(x, sh, sc, gt, lg, lb):
                x_new, rows, win = nsa_prompt_layer(x, sh, sc, gt, *nw, lg, lb)
                return x_new, (rows, win[:, x.shape[1] - min(WINDOW, x.shape[1]):])

            xp, (r_p, w_p) = _trunk_layer(xp, m_p, i, prompt_mixer, *lw)
            def sample_mixer(x, sh, sc, gt, lg, lb):
                x_new, rows, win = nsa_sample_layer(x, sh, sc, gt, cache_kv, page_table, j, cache_win[:, j], *nw, lg, lb)
                return x_new, (rows, win)

            xs, (r_s, w_s) = _trunk_layer(xs, m_s, i, sample_mixer, *lw)
            kv_p.append(r_p); kv_s.append(r_s); win_p.append(w_p); win_s.append(w_s)
        else:
            sw = (ssm_w_in[j], ssm_conv_w[j], ssm_conv_b[j], ssm_dt_bias[j], ssm_a_log[j], ssm_d[j], ssm_norm_g[j], ssm_w_out[j])
            zc = jnp.zeros((xp.shape[0], CONV_W - 1, CONV_DIM), f32)
            zs = jnp.zeros((xp.shape[0], SSM_HEADS, SSM_HEAD_DIM, D_STATE), f32)
            def ssd_mixer(conv0, ssm0):
                def run(x, sh, sc, gt, lg, lb):
                    x_new, cv, st = ssd_layer(x, sh, sc, gt, conv0, ssm0, *sw, lg, lb)
                    return x_new, (cv, st)
                return run

            xp, (cv_p, st_p) = _trunk_layer(xp, m_p, i, ssd_mixer(zc, zs), *lw)
            xs, (cv_s, st_s) = _trunk_layer(xs, m_s, i, ssd_mixer(state_conv[:, j], state_ssm[:, j]), *lw)
            conv_p.append(cv_p); conv_s.append(cv_s); ssm_p.append(st_p); ssm_s.append(st_s)
    return (xp, xs, jnp.stack(kv_p, 1), jnp.stack(kv_s, 1), jnp.stack(win_p, 1), jnp.stack(win_s, 1),
            jnp.stack(conv_p, 1), jnp.stack(conv_s, 1), jnp.stack(ssm_p, 1), jnp.stack(ssm_s, 1))
```

```python
import functools
import math

import jax
import jax.numpy as jnp
from jax import lax
from jax.experimental import pallas as pl
from jax.experimental.pallas import tpu as pltpu

f32 = jnp.float32
bf16 = jnp.bfloat16

D_MODEL = 1024
DEPTH = 4
PAGE_SIZE = 128
N_HEADS = 16
HEAD_DIM = 64
N_KV = 4
Q_PER_KV = N_HEADS // N_KV
CMP_BLK = 32
CMP_STRIDE = 16
CMP_HIDDEN = 2 * HEAD_DIM
SEL_BLK = 64
TOPK = 16
WINDOW = 512
Q_BLOCK = 128
ROPE_THETA = 10000.0
D_INNER = 2 * D_MODEL
SSM_HEAD_DIM = 64
SSM_HEADS = D_INNER // SSM_HEAD_DIM
SSM_GROUPS = 4
D_STATE = 128
CONV_W = 4
CONV_DIM = D_INNER + 2 * SSM_GROUPS * D_STATE
SSM_CHUNK = 128
D_FF = 256 * ((8 * D_MODEL // 3 + 255) // 256)
N_MIXERS = 2
ALPHA = (2 * DEPTH) ** 0.25
N_ADA = 9
LN_EPS = 1e-5

V7X_VMEM_LIMIT_BYTES = 56 * 1024 * 1024
FF_CHUNK = 256
FFN_ROWS = 512


def _ffn_kernel(x_ref, sh_ref, sc_ref, gt_ref, w1_ref, w3_ref, w2_ref, lg_ref, lb_ref, o_ref, acc_ref):
    x = x_ref[0]
    hb = (x * (1.0 + sc_ref[0]) + sh_ref[0]).astype(bf16)
    acc_ref[...] = jnp.zeros_like(acc_ref)

    def chunk(c, carry):
        a = jnp.dot(hb, w1_ref[c], preferred_element_type=f32)
        b = jnp.dot(hb, w3_ref[c], preferred_element_type=f32)
        g = (a * jax.nn.sigmoid(a) * b).astype(bf16)
        acc_ref[...] += jnp.dot(g, w2_ref[c], preferred_element_type=f32)
        return carry

    lax.fori_loop(0, w1_ref.shape[0], chunk, 0)
    y = ALPHA * x + (1.0 + gt_ref[0]) * (0.5 * acc_ref[...])
    mu = jnp.mean(y, axis=-1, keepdims=True)
    yc = y - mu
    var = jnp.mean(yc * yc, axis=-1, keepdims=True)
    o_ref[0] = yc * lax.rsqrt(var + LN_EPS) * lg_ref[...] + lb_ref[...]


def _mod_spec(m, rows):
    if m.shape[1] == 1:
        return pl.BlockSpec((1, 1, m.shape[2]), lambda b, t: (b, 0, 0))
    return pl.BlockSpec((1, rows, m.shape[2]), lambda b, t: (b, t, 0))


def ffn_sublayer(x, shift, scale, gate, w1c, w3c, w2c, ln_g, ln_b):
    b_, t_, d_ = x.shape
    rows = min(FFN_ROWS, t_)
    mod_spec = _mod_spec(shift, rows)
    const3 = lambda b, t: (0, 0, 0)
    wspec = lambda w: pl.BlockSpec(w.shape, const3, pipeline_mode=pl.Buffered(1))
    vec = pl.BlockSpec((1, d_), lambda b, t: (0, 0))
    return pl.pallas_call(
        _ffn_kernel,
        grid=(b_, t_ // rows),
        in_specs=[pl.BlockSpec((1, rows, d_), lambda b, t: (b, t, 0)), mod_spec, mod_spec, mod_spec,
                  wspec(w1c), wspec(w3c), wspec(w2c), vec, vec],
        out_specs=pl.BlockSpec((1, rows, d_), lambda b, t: (b, t, 0)),
        out_shape=jax.ShapeDtypeStruct(x.shape, f32),
        scratch_shapes=[pltpu.VMEM((rows, d_), f32)],
        compiler_params=pltpu.CompilerParams(dimension_semantics=("arbitrary", "arbitrary"),
                                             vmem_limit_bytes=V7X_VMEM_LIMIT_BYTES),
        name="ffn_sublayer",
    )(x, shift, scale, gate, w1c, w3c, w2c, ln_g.reshape(1, d_), ln_b.reshape(1, d_))


def _chunk_ffn_weights(w1, w3, w2):
    n = D_FF // FF_CHUNK
    w1c = w1.astype(bf16).reshape(D_MODEL, n, FF_CHUNK).transpose(1, 0, 2)
    w3c = w3.astype(bf16).reshape(D_MODEL, n, FF_CHUNK).transpose(1, 0, 2)
    w2c = w2.astype(bf16).reshape(n, FF_CHUNK, D_MODEL)
    return w1c, w3c, w2c


NEG = -1e30
PROJ_ROWS = 512
SEL_TK = 512
KD = N_KV * HEAD_DIM
_C_Q, _C_QR, _C_CMP, _C_SEL, _C_WIN, _C_G, _C_END = 0, 1024, 2048, 2560, 3328, 4096, 4224
N_GATE = 3 * N_HEADS


def _rot_cols(w):
    w4 = w.reshape(w.shape[0], -1, 2, HEAD_DIM // 2)
    return jnp.stack([-w4[:, :, 1], w4[:, :, 0]], axis=2).reshape(w.shape)


def _nsa_proj_weights(w_in):
    qd = N_HEADS * HEAD_DIM
    wq = w_in[:, :qd] * HEAD_DIM ** -0.5
    kc, vc, ks, vs, kw, vw = [w_in[:, qd + i * KD: qd + (i + 1) * KD] for i in range(6)]
    wg = jnp.pad(w_in[:, qd + 6 * KD:], ((0, 0), (0, 128 - N_GATE)))
    return jnp.concatenate([wq, _rot_cols(wq), kc, vc, ks, _rot_cols(ks), vs, kw, _rot_cols(kw), vw, wg], axis=1).astype(bf16)


def _rope_tables(pos, width):
    half = HEAD_DIM // 2
    inv = ROPE_THETA ** (-jnp.arange(half, dtype=f32) / half)
    ang = jnp.tile(pos.astype(f32)[:, None] * inv, (1, width // half))
    return jnp.cos(ang), jnp.sin(ang)


def _nsa_proj_kernel(x_ref, sh_ref, sc_ref, w_ref, cos_ref, sin_ref, q_ref, rows_ref, win_ref, gate_ref, kvb_ref):
    hb = (x_ref[0] * (1.0 + sc_ref[0]) + sh_ref[0]).astype(bf16)
    cos, sin = cos_ref[...], sin_ref[...]

    def mm(lo, hi):
        return jnp.dot(hb, w_ref[:, lo:hi], preferred_element_type=f32)

    def rope(a, b, c):
        return a[:, c * 128:(c + 1) * 128] * cos + b[:, c * 128:(c + 1) * 128] * sin

    qa, qb = mm(_C_Q, _C_QR), mm(_C_QR, _C_CMP)
    for c in range(N_HEADS * HEAD_DIM // 128):
        q_ref[0, :, c * 128:(c + 1) * 128] = rope(qa, qb, c).astype(bf16)
    rows_ref[0, :, 0:2 * KD] = mm(_C_CMP, _C_SEL)
    sel = mm(_C_SEL, _C_WIN)
    win = mm(_C_WIN, _C_G)
    for c in range(KD // 128):
        ks = rope(sel[:, 0:KD], sel[:, KD:2 * KD], c)
        kw = rope(win[:, 0:KD], win[:, KD:2 * KD], c)
        rows_ref[0, :, 2 * KD + c * 128:2 * KD + (c + 1) * 128] = ks
        win_ref[0, :, c * 128:(c + 1) * 128] = kw
        kvb_ref[0, :, c * 128:(c + 1) * 128] = ks.astype(bf16)
        kvb_ref[0, :, 2 * KD + c * 128:2 * KD + (c + 1) * 128] = kw.astype(bf16)
    rows_ref[0, :, 3 * KD:4 * KD] = sel[:, 2 * KD:3 * KD]
    win_ref[0, :, KD:2 * KD] = win[:, 2 * KD:3 * KD]
    kvb_ref[0, :, KD:2 * KD] = sel[:, 2 * KD:3 * KD].astype(bf16)
    kvb_ref[0, :, 3 * KD:4 * KD] = win[:, 2 * KD:3 * KD].astype(bf16)
    gate_ref[0] = jax.nn.sigmoid(mm(_C_G, _C_END))


def nsa_project(x, shift, scale, w_all, cos, sin):
    b_, t_, d_ = x.shape
    rows = min(PROJ_ROWS, t_)
    tile = lambda n: pl.BlockSpec((1, rows, n), lambda b, t: (b, t, 0))
    mod = _mod_spec(shift, rows)
    tab = pl.BlockSpec((rows, 128), lambda b, t: (t, 0))
    return pl.pallas_call(
        _nsa_proj_kernel,
        grid=(b_, t_ // rows),
        in_specs=[tile(d_), mod, mod, pl.BlockSpec(w_all.shape, lambda b, t: (0, 0), pipeline_mode=pl.Buffered(1)), tab, tab],
        out_specs=[tile(4 * KD), tile(4 * KD), tile(2 * KD), tile(128), tile(4 * KD)],
        out_shape=[jax.ShapeDtypeStruct((b_, t_, 4 * KD), bf16), jax.ShapeDtypeStruct((b_, t_, 4 * KD), f32),
                   jax.ShapeDtypeStruct((b_, t_, 2 * KD), f32), jax.ShapeDtypeStruct((b_, t_, 128), f32),
                   jax.ShapeDtypeStruct((b_, t_, 4 * KD), bf16)],
        compiler_params=pltpu.CompilerParams(dimension_semantics=("arbitrary", "arbitrary"),
                                             vmem_limit_bytes=V7X_VMEM_LIMIT_BYTES),
        name="nsa_project",
    )(x, shift, scale, w_all, cos, sin)


def _gelu_tanh(x):
    return 0.5 * x * (1.0 + jnp.tanh(math.sqrt(2.0 / math.pi) * (x + 0.044715 * (x * x * x))))


def _compress_kernel(xa_ref, xb_ref, pe_ref, w1_ref, b1_ref, w2_ref, w2r_ref, b2_ref, b2r_ref, w2t_ref, w2rt_ref,
                     b2t_ref, b2rt_ref, cos_ref, sin_ref, cost_ref, sint_ref, on_ref, ot_ref, *, projected):
    w1 = w1_ref[0]
    half = w1.shape[0] // 2
    bias = jnp.dot(pe_ref[0], w1, preferred_element_type=f32)[0:1] + b1_ref[0]
    if projected:
        pre = xa_ref[0, 0, 0] + xb_ref[0, 0, 0] + bias
    else:
        pre = (jnp.dot(xa_ref[0, 0, 0], w1[:half], preferred_element_type=f32)
               + jnp.dot(xb_ref[0, 0, 0], w1[half:], preferred_element_type=f32) + bias)
    hb = _gelu_tanh(pre).astype(bf16)
    y = jnp.dot(hb, w2_ref[0], preferred_element_type=f32) + b2_ref[0]
    yr = jnp.dot(hb, w2r_ref[0], preferred_element_type=f32) + b2r_ref[0]
    on_ref[0, 0, 0] = y * cos_ref[0] + yr * sin_ref[0]
    nt = (((1,), (1,)), ((), ()))
    yt = lax.dot_general(w2t_ref[0], hb, nt, preferred_element_type=f32) + b2t_ref[0]
    yrt = lax.dot_general(w2rt_ref[0], hb, nt, preferred_element_type=f32) + b2rt_ref[0]
    ot_ref[0, 0, 0] = yt * cost_ref[0] + yrt * sint_ref[0]


def compress_kv(chunks, pe, w1, b1, w2, b2, projected=False):
    b_, _, g_, n_ch, feat = chunks.shape
    if projected:
        feat = CMP_HIDDEN
        first, second = chunks[..., :feat], chunks[..., feat:]
    else:
        first = second = chunks
    chunks = first
    nxt = jnp.concatenate([second[:, :, :, 1:], jnp.zeros_like(second[:, :, :, :1])], axis=3)
    cend = jnp.arange(n_ch) * CMP_STRIDE + CMP_BLK - 1
    cos, sin = _rope_tables(cend, HEAD_DIM)
    cos = jnp.stack([cos, jnp.ones_like(cos)])
    sin = jnp.stack([sin, jnp.zeros_like(sin)])
    w2b = w2.astype(bf16)
    w2r = _rot_cols(w2).astype(bf16)
    b2r = _rot_cols(b2[:, None, :])
    pe8 = jnp.broadcast_to(pe.reshape(2, 1, -1), (2, 8, pe.shape[1] * pe.shape[2])).astype(bf16)
    xspec = pl.BlockSpec((1, 1, 1, n_ch, feat), lambda b, k, g: (b, k, g, 0, 0))
    per_kv = lambda *s: pl.BlockSpec((1,) + s, lambda b, k, g: (k,) + (0,) * len(s))
    return pl.pallas_call(
        functools.partial(_compress_kernel, projected=projected),
        grid=(b_, 2, g_),
        in_specs=[xspec, xspec, per_kv(8, pe8.shape[2]), per_kv(*w1.shape[1:]), per_kv(1, CMP_HIDDEN),
                  per_kv(CMP_HIDDEN, HEAD_DIM), per_kv(CMP_HIDDEN, HEAD_DIM), per_kv(1, HEAD_DIM), per_kv(1, HEAD_DIM),
                  per_kv(HEAD_DIM, CMP_HIDDEN), per_kv(HEAD_DIM, CMP_HIDDEN), per_kv(HEAD_DIM, 1), per_kv(HEAD_DIM, 1),
                  per_kv(n_ch, HEAD_DIM), per_kv(n_ch, HEAD_DIM), per_kv(HEAD_DIM, n_ch), per_kv(HEAD_DIM, n_ch)],
        out_specs=[pl.BlockSpec((1, 1, 1, n_ch, HEAD_DIM), lambda b, k, g: (b, k, g, 0, 0)),
                   pl.BlockSpec((1, 1, 1, HEAD_DIM, n_ch), lambda b, k, g: (b, k, g, 0, 0))],
        out_shape=[jax.ShapeDtypeStruct((b_, 2, g_, n_ch, HEAD_DIM), f32),
                   jax.ShapeDtypeStruct((b_, 2, g_, HEAD_DIM, n_ch), f32)],
        compiler_params=pltpu.CompilerParams(dimension_semantics=("arbitrary",) * 3),
        name="compress_kv",
    )(chunks, nxt, pe8, w1.astype(bf16), b1[:, None, :], w2b, w2r, b2[:, None, :], b2r,
      w2b.transpose(0, 2, 1), w2r.transpose(0, 2, 1), b2[:, :, None], b2r.transpose(0, 2, 1),
      cos, sin, cos.transpose(0, 2, 1), sin.transpose(0, 2, 1))


def _cmp_to_sel_matrix(n_c, n_sel):
    cs = jnp.arange(n_c)[:, None] * CMP_STRIDE
    ss = jnp.arange(n_sel)[None, :] * SEL_BLK
    ov = jnp.minimum(cs + CMP_BLK, ss + SEL_BLK) - jnp.maximum(cs, ss)
    return jnp.clip(ov, 0, None).astype(f32) / CMP_BLK


def _lanes(col, n):
    if n % 128:
        return jnp.broadcast_to(col, (col.shape[0], n))
    tile = jnp.broadcast_to(col, (col.shape[0], 128))
    return tile if n == 128 else jnp.concatenate([tile] * (n // 128), axis=1)


def _softmax_rows(s, valid):
    n = s.shape[1]
    e = jnp.where(valid, jnp.exp(s - _lanes(jnp.max(s, axis=1, keepdims=True), n)), 0.0)
    l = jnp.sum(e, axis=1, keepdims=True)
    return e * _lanes(1.0 / jnp.maximum(l, 1e-30), n)


def _unselected_blocks(imp_t, qpos, n_sel):
    jj = lax.broadcasted_iota(jnp.int32, imp_t.shape, 0)
    cur = qpos // SEL_BLK
    forced = (jj == 0) | (jj == cur) | (jj == cur - 1)
    sc0 = jnp.where((jj <= cur) & (jj < n_sel), jnp.where(forced, -NEG, imp_t), NEG)
    jjf = jj.astype(f32)

    def pick(_, carry):
        sc, nonsel = carry
        m = jnp.max(sc, axis=0, keepdims=True)
        idx = jnp.min(jnp.where(sc == m, jjf, float(imp_t.shape[0])), axis=0, keepdims=True)
        hit = (jjf == idx) & (m > NEG)
        return jnp.where(hit, NEG, sc), jnp.where(hit, 0.0, nonsel)

    return lax.fori_loop(0, min(TOPK, n_sel), pick, (sc0, jnp.ones(imp_t.shape, f32)))[1]


def _nsa_attn_kernel(q_ref, kct_ref, vc_ref, ks_ref, vs_ref, kw_ref, vw_ref, gate_ref, msel_ref, o_ref,
                     oc_sc, score_sc, nonsel_sc, m_sc, acc_sc, *, n_cmp, n_sel):
    qt = pl.program_id(1)
    s0 = qt * Q_BLOCK
    rq = Q_PER_KV * Q_BLOCK
    row_q = s0 + (lax.broadcasted_iota(jnp.int32, (rq, 1), 0) & (Q_BLOCK - 1))

    def cmp_branch(ncol):
        cend = lax.broadcasted_iota(jnp.int32, (1, ncol), 1) * CMP_STRIDE + (CMP_BLK - 1)
        cvalid = cend <= row_q
        for g in range(N_KV):
            s = jnp.dot(q_ref[0, g, 0], kct_ref[0, g, :, 0:ncol], preferred_element_type=f32)
            p = _softmax_rows(jnp.where(cvalid, s, NEG), cvalid)
            oc_sc[g] = jnp.dot(p.astype(bf16), vc_ref[0, g, 0:ncol, :], preferred_element_type=f32)
            ps = p[0:Q_BLOCK]
            for r in range(1, Q_PER_KV):
                ps = ps + p[r * Q_BLOCK:(r + 1) * Q_BLOCK]
            hi = ps.astype(bf16)
            lo = (ps - hi.astype(f32)).astype(bf16)
            imp = (jnp.dot(hi, msel_ref[0:ncol, :], preferred_element_type=f32)
                   + jnp.dot(lo, msel_ref[0:ncol, :], preferred_element_type=f32))
            score_sc[:, g * Q_BLOCK:(g + 1) * Q_BLOCK] = imp.T

    widths = [w for w in range(128, n_cmp + 1, 128)] or [n_cmp]
    n_live = (s0 + Q_BLOCK - CMP_BLK) // CMP_STRIDE + 1
    lax.switch(jnp.clip((n_live + 127) // 128 - 1, 0, len(widths) - 1), [functools.partial(cmp_branch, w) for w in widths])

    qpos = s0 + (lax.broadcasted_iota(jnp.int32, (n_sel, N_KV * Q_BLOCK), 1) & (Q_BLOCK - 1))
    nonsel = _unselected_blocks(score_sc[...], qpos, n_sel)
    for g in range(N_KV):
        nonsel_sc[g] = nonsel[:, g * Q_BLOCK:(g + 1) * Q_BLOCK].T.astype(bf16)

    n_tiles = (s0 + Q_BLOCK + SEL_TK - 1) // SEL_TK
    key_l = lax.broadcasted_iota(jnp.int32, (1, SEL_TK), 1)
    blk_j = lax.broadcasted_iota(jnp.int32, (n_sel, SEL_TK), 0)
    blk_l = lax.broadcasted_iota(jnp.int32, (n_sel, SEL_TK), 1)
    wkey_l = lax.broadcasted_iota(jnp.int32, (1, Q_BLOCK), 1)
    n_win = WINDOW // Q_BLOCK + 1
    m_sc[...] = jnp.full_like(m_sc, NEG)
    acc_sc[...] = jnp.zeros_like(acc_sc)

    def sel_tile(kt, causal):
        expand = jnp.where(blk_j == (kt * SEL_TK + blk_l) // SEL_BLK, NEG, 0.0).astype(bf16)
        ss = []
        for g in range(N_KV):
            s = jnp.dot(q_ref[0, g, 0], ks_ref[0, g, kt], preferred_element_type=f32)
            bias = jnp.dot(nonsel_sc[g], expand, preferred_element_type=f32)
            s = s + jnp.concatenate([bias] * Q_PER_KV, axis=0)
            if causal:
                s = jnp.where(kt * SEL_TK + key_l <= row_q, s, NEG)
            ss.append(s)
        for g in range(N_KV):
            m_old = m_sc[g]
            m_new = jnp.maximum(m_old, jnp.broadcast_to(jnp.max(ss[g], axis=1, keepdims=True), m_old.shape))
            p = jnp.exp(ss[g] - jnp.concatenate([m_new] * (SEL_TK // 128), axis=1)).astype(bf16)
            acc_sc[g] = jnp.exp(m_old - m_new) * acc_sc[g] + jnp.dot(p, vs_ref[0, g, kt], preferred_element_type=f32)
            m_sc[g] = m_new

    def body(kt, carry):
        sel_tile(kt, False)
        return carry

    lax.fori_loop(0, n_tiles - 1, body, 0)
    sel_tile(n_tiles - 1, True)

    for g in range(N_KV):
        q = q_ref[0, g, 0]
        acc = acc_sc[g]
        o_s = acc[:, 0:HEAD_DIM] * (1.0 / jnp.maximum(acc[:, HEAD_DIM:HEAD_DIM + 1], 1e-30))

        s_parts, w_tiles = [], []
        for i in range(n_win):
            kt = qt - (n_win - 1) + i
            ktc = jnp.maximum(kt, 0)
            kpos = kt * Q_BLOCK + wkey_l
            dpos = row_q - kpos
            ok = (kpos >= 0) & (dpos >= 0) & (dpos < WINDOW)
            s_parts.append(jnp.where(ok, jnp.dot(q, kw_ref[0, g, ktc], preferred_element_type=f32), NEG))
            w_tiles.append(ktc)
        s_w = jnp.concatenate(s_parts, axis=1)
        e_w = jnp.exp(s_w - _lanes(jnp.max(s_w, axis=1, keepdims=True), s_w.shape[1])).astype(bf16)
        acc_w = jnp.dot(e_w[:, 0:Q_BLOCK], vw_ref[0, g, w_tiles[0]], preferred_element_type=f32)
        for i in range(1, n_win):
            acc_w = acc_w + jnp.dot(e_w[:, i * Q_BLOCK:(i + 1) * Q_BLOCK], vw_ref[0, g, w_tiles[i]], preferred_element_type=f32)
        o_w = acc_w[:, 0:HEAD_DIM] * (1.0 / jnp.maximum(acc_w[:, HEAD_DIM:HEAD_DIM + 1], 1e-30))

        o_c = oc_sc[g]
        gates = gate_ref[0]
        for r in range(Q_PER_KV):
            c = (g * Q_PER_KV + r) * 3
            rows = slice(r * Q_BLOCK, (r + 1) * Q_BLOCK)
            o = (gates[:, c:c + 1] * o_c[rows] + gates[:, c + 1:c + 2] * o_s[rows] + gates[:, c + 2:c + 3] * o_w[rows])
            o_ref[0, g, 0, rows, :] = o.astype(bf16)


def _with_ones(v):
    pad = jnp.zeros(v.shape[:-1] + (128 - v.shape[-1] - 1,), v.dtype)
    return jnp.concatenate([v, jnp.ones(v.shape[:-1] + (1,), v.dtype), pad], axis=-1)


def nsa_attention(q, kvb, gates, kct, vc):
    b_, t_, _ = q.shape
    n_qt, n_st, n_sel = t_ // Q_BLOCK, t_ // SEL_TK, t_ // SEL_BLK
    n_cmp = kct.shape[-1]
    rq = Q_PER_KV * Q_BLOCK
    q5 = q.reshape(b_, n_qt, Q_BLOCK, N_KV, Q_PER_KV, HEAD_DIM).transpose(0, 3, 1, 4, 2, 5).reshape(b_, N_KV, n_qt, rq, HEAD_DIM)

    def tiles(x, tk):
        return x.reshape(b_, t_ // tk, tk, N_KV, HEAD_DIM).transpose(0, 3, 1, 2, 4)

    ks = tiles(kvb[..., 0:KD], SEL_TK).transpose(0, 1, 2, 4, 3)
    vs = _with_ones(tiles(kvb[..., KD:2 * KD], SEL_TK))
    kw = tiles(kvb[..., 2 * KD:3 * KD], Q_BLOCK).transpose(0, 1, 2, 4, 3)
    vw = _with_ones(tiles(kvb[..., 3 * KD:4 * KD], Q_BLOCK))
    msel = _cmp_to_sel_matrix(n_cmp, n_sel).astype(bf16)
    per_b = lambda x: pl.BlockSpec((1,) + x.shape[1:], lambda b, t: (b,) + (0,) * (x.ndim - 1), pipeline_mode=pl.Buffered(1))
    qspec = pl.BlockSpec((1, N_KV, 1, rq, HEAD_DIM), lambda b, t: (b, 0, t, 0, 0))
    o5 = pl.pallas_call(
        functools.partial(_nsa_attn_kernel, n_cmp=n_cmp, n_sel=n_sel),
        grid=(b_, n_qt),
        in_specs=[qspec, per_b(kct), per_b(vc), per_b(ks), per_b(vs), per_b(kw), per_b(vw),
                  pl.BlockSpec((1, Q_BLOCK, 128), lambda b, t: (b, t, 0)),
                  pl.BlockSpec(msel.shape, lambda b, t: (0, 0))],
        out_specs=qspec,
        out_shape=jax.ShapeDtypeStruct(q5.shape, bf16),
        scratch_shapes=[pltpu.VMEM((N_KV, rq, HEAD_DIM), f32), pltpu.VMEM((n_sel, N_KV * Q_BLOCK), f32),
                        pltpu.VMEM((N_KV, Q_BLOCK, n_sel), bf16), pltpu.VMEM((N_KV, rq, 128), f32), pltpu.VMEM((N_KV, rq, 128), f32)],
        compiler_params=pltpu.CompilerParams(dimension_semantics=("arbitrary", "arbitrary"),
                                             vmem_limit_bytes=V7X_VMEM_LIMIT_BYTES),
        name="nsa_attention",
    )(q5, kct, vc, ks, vs, kw, vw, gates, msel)
    return o5.reshape(b_, N_KV, n_qt, Q_PER_KV, Q_BLOCK, HEAD_DIM).transpose(0, 2, 4, 1, 3, 5).reshape(b_, t_, N_HEADS * HEAD_DIM)


def _mixer_out_kernel(y_ref, x_ref, gt_ref, w_ref, lg_ref, lb_ref, o_ref):
    f = jnp.dot(y_ref[0], w_ref[...], preferred_element_type=f32)
    y = ALPHA * x_ref[0] + (1.0 + gt_ref[0]) * f
    mu = jnp.mean(y, axis=-1, keepdims=True)
    yc = y - mu
    var = jnp.mean(yc * yc, axis=-1, keepdims=True)
    o_ref[0] = yc * lax.rsqrt(var + LN_EPS) * lg_ref[...] + lb_ref[...]


def mixer_out(y, x, gate, w, ln_g, ln_b):
    b_, t_, d_ = x.shape
    rows = min(PROJ_ROWS, t_)
    vec = pl.BlockSpec((1, d_), lambda b, t: (0, 0))
    return pl.pallas_call(
        _mixer_out_kernel,
        grid=(b_, t_ // rows),
        in_specs=[pl.BlockSpec((1, rows, y.shape[2]), lambda b, t: (b, t, 0)), pl.BlockSpec((1, rows, d_), lambda b, t: (b, t, 0)),
                  _mod_spec(gate, rows), pl.BlockSpec(w.shape, lambda b, t: (0, 0)), vec, vec],
        out_specs=pl.BlockSpec((1, rows, d_), lambda b, t: (b, t, 0)),
        out_shape=jax.ShapeDtypeStruct(x.shape, f32),
        compiler_params=pltpu.CompilerParams(dimension_semantics=("arbitrary", "arbitrary"),
                                             vmem_limit_bytes=V7X_VMEM_LIMIT_BYTES),
        name="mixer_out",
    )(y, x, gate, w, ln_g.reshape(1, d_), ln_b.reshape(1, d_))


def nsa_prompt_layer(x, shift, scale, gate, w_in, w_o, pe, cw1, cb1, cw2, cb2, ln_g, ln_b):
    b_, t_, _ = x.shape
    cos, sin = _rope_tables(jnp.arange(t_), 128)
    q, rows, win, gates, kvb = nsa_project(x, shift, scale, _nsa_proj_weights(w_in), cos, sin)
    n_ch = t_ // CMP_STRIDE
    chunks = rows[..., 0:2 * KD].astype(bf16).reshape(b_, n_ch, CMP_STRIDE, 2, N_KV, HEAD_DIM)
    chunks = chunks.transpose(0, 3, 4, 1, 2, 5).reshape(b_, 2, N_KV, n_ch, CMP_STRIDE * HEAD_DIM)
    cmp_n, cmp_t = compress_kv(chunks, pe, cw1, cb1, cw2, cb2)
    o = nsa_attention(q, kvb, gates, cmp_t[:, 0].astype(bf16), cmp_n[:, 1].astype(bf16))
    x_new = mixer_out(o, x, gate, w_o.astype(bf16), ln_g, ln_b)
    return x_new, rows.reshape(b_, t_, 4, N_KV, HEAD_DIM), win.reshape(b_, t_, 2, N_KV, HEAD_DIM)


PAGES_PER_STEP = 8
S_PAD = 8
CHUNKS_PER_PAGE = PAGE_SIZE // CMP_STRIDE
WIN_KEYS_PAD = 128


def _page_specs(block, tail):
    def spec(i):
        return pl.BlockSpec(block, lambda b, pg, pt: (pt[b, pg * PAGES_PER_STEP + i],) + tail)
    return [spec(i) for i in range(PAGES_PER_STEP)]


def _cmp_pages_kernel(pt_ref, *refs):
    page_refs = refs[:PAGES_PER_STEP]
    perm_ref, w_ref, o_ref = refs[PAGES_PER_STEP:]
    nt = (((1,), (1,)), ((), ()))
    xp = [lax.dot_general(perm_ref[...], r[0, 0].reshape(2 * KD, PAGE_SIZE).astype(bf16), nt, preferred_element_type=f32)
          for r in page_refs]
    cpp = CHUNKS_PER_PAGE
    xs = [jnp.concatenate([x[s * cpp:(s + 1) * cpp] for x in xp], axis=0) for s in range(CMP_STRIDE)]
    for kv in range(2):
        groups = []
        for g in range(N_KV):
            lo = (kv * N_KV + g) * HEAD_DIM
            groups.append(jnp.concatenate([x[:, lo:lo + HEAD_DIM] for x in xs], axis=1))
        xc = jnp.concatenate(groups, axis=0).astype(bf16)
        o_ref[0, 0, kv] = jnp.dot(xc, w_ref[kv], preferred_element_type=f32)


def cmp_pages(pages, li, page_table, w01):
    b_, n_pages = page_table.shape
    n_pg = n_pages // PAGES_PER_STEP
    rows = PAGES_PER_STEP * N_KV * CHUNKS_PER_PAGE
    pos = jnp.arange(PAGE_SIZE)
    perm = (pos[None, :] == (pos[:, None] % CHUNKS_PER_PAGE) * CMP_STRIDE + pos[:, None] // CHUNKS_PER_PAGE).astype(bf16)
    out = pl.pallas_call(
        _cmp_pages_kernel,
        grid_spec=pltpu.PrefetchScalarGridSpec(
            num_scalar_prefetch=1, grid=(b_, n_pg),
            in_specs=_page_specs((1, 1, 2, N_KV, HEAD_DIM, PAGE_SIZE), (li, 0, 0, 0, 0))
            + [pl.BlockSpec(perm.shape, lambda b, pg, pt: (0, 0)), pl.BlockSpec(w01.shape, lambda b, pg, pt: (0, 0, 0))],
            out_specs=pl.BlockSpec((1, 1, 2, rows, w01.shape[2]), lambda b, pg, pt: (b, pg, 0, 0, 0))),
        out_shape=jax.ShapeDtypeStruct((b_, n_pg, 2, rows, w01.shape[2]), f32),
        compiler_params=pltpu.CompilerParams(dimension_semantics=("arbitrary", "arbitrary"),
                                             vmem_limit_bytes=V7X_VMEM_LIMIT_BYTES),
        name="cmp_pages",
    )(page_table, *([pages] * PAGES_PER_STEP), perm, w01)
    out = out.reshape(b_, n_pg, 2, N_KV, PAGES_PER_STEP, CHUNKS_PER_PAGE, w01.shape[2])
    return out.transpose(0, 2, 3, 1, 4, 5, 6).reshape(b_, 2, N_KV, n_pages * CHUNKS_PER_PAGE, w01.shape[2])


def _sample_cmp_win_kernel(q_ref, kct_ref, vc_ref, kw_ref, vw_ref, msel_ref, oc_ref, ow_ref, imp_ref, *, past, w_buf, n_wk):
    rows = Q_PER_KV * S_PAD
    qpos = past + (lax.broadcasted_iota(jnp.int32, (rows, 1), 0) & (S_PAD - 1))
    cend = lax.broadcasted_iota(jnp.int32, (1, kct_ref.shape[3]), 1) * CMP_STRIDE + (CMP_BLK - 1)
    cvalid = cend <= qpos
    widx = lax.broadcasted_iota(jnp.int32, (1, kw_ref.shape[3]), 1)
    kwpos = past - w_buf + widx
    dpos = qpos - kwpos
    wok = (dpos >= 0) & (dpos < WINDOW) & (kwpos >= 0) & (widx < n_wk)
    for g in range(N_KV):
        q = q_ref[0, g]
        s = jnp.dot(q, kct_ref[0, g], preferred_element_type=f32)
        p = _softmax_rows(jnp.where(cvalid, s, NEG), cvalid)
        oc_ref[0, g] = jnp.dot(p.astype(bf16), vc_ref[0, g], preferred_element_type=f32)
        hi = p.astype(bf16)
        lo = (p - hi.astype(f32)).astype(bf16)
        imp = jnp.dot(hi, msel_ref[...], preferred_element_type=f32) + jnp.dot(lo, msel_ref[...], preferred_element_type=f32)
        tot = imp[0:S_PAD]
        for r in range(1, Q_PER_KV):
            tot = tot + imp[r * S_PAD:(r + 1) * S_PAD]
        imp_ref[0, g] = tot
        sw = jnp.where(wok, jnp.dot(q, kw_ref[0, g], preferred_element_type=f32), NEG)
        ew = jnp.exp(sw - _lanes(jnp.max(sw, axis=1, keepdims=True), sw.shape[1])).astype(bf16)
        acc = jnp.dot(ew, vw_ref[0, g], preferred_element_type=f32)
        ow_ref[0, g] = acc[:, 0:HEAD_DIM] * (1.0 / jnp.maximum(acc[:, HEAD_DIM:HEAD_DIM + 1], 1e-30))


def _topk_kernel(imp_ref, o_ref, *, past, n_sel):
    qpos = past + (lax.broadcasted_iota(jnp.int32, imp_ref.shape, 1) & (S_PAD - 1))
    o_ref[...] = _unselected_blocks(imp_ref[...], qpos, n_sel)


def _sample_sel_kernel(pt_ref, *refs, past, n_new):
    page_refs = refs[:PAGES_PER_STEP]
    q_ref, nonsel_ref, knew_ref, vnew_ref, os_ref, m_sc, l_sc, acc_sc = refs[PAGES_PER_STEP:]
    pg = pl.program_id(1)
    nt = (((1,), (1,)), ((), ()))
    rows = Q_PER_KV * S_PAD

    @pl.when(pg == 0)
    def _():
        m_sc[...] = jnp.full_like(m_sc, NEG)
        l_sc[...] = jnp.zeros_like(l_sc)
        acc_sc[...] = jnp.zeros_like(acc_sc)

    def accumulate(s, pv):
        m_old = m_sc[...]
        m_new = jnp.maximum(m_old, jnp.max(s, axis=1, keepdims=True))
        p = jnp.exp(s - m_new)
        alpha = jnp.exp(m_old - m_new)
        l_sc[...] = alpha * l_sc[...] + jnp.sum(p, axis=1, keepdims=True)
        pb = p.astype(bf16)
        upd = jnp.concatenate([pv(g, pb[g * rows:(g + 1) * rows]) for g in range(N_KV)], axis=0)
        acc_sc[...] = alpha * acc_sc[...] + upd
        m_sc[...] = m_new

    nk = PAGES_PER_STEP * PAGE_SIZE
    kt = [jnp.concatenate([r[0, 0, 0, g] for r in page_refs], axis=1).astype(bf16) for g in range(N_KV)]
    vt = [jnp.concatenate([r[0, 0, 1, g] for r in page_refs], axis=1).astype(bf16) for g in range(N_KV)]
    nonsel = nonsel_ref[0]
    s = jnp.concatenate([jnp.dot(q_ref[0, g], kt[g], preferred_element_type=f32) for g in range(N_KV)], axis=0)
    blk_j = lax.broadcasted_iota(jnp.int32, (nonsel.shape[1], nk), 0)
    key = pg * nk + lax.broadcasted_iota(jnp.int32, (nonsel.shape[1], nk), 1)
    expand = jnp.where(blk_j == key // SEL_BLK, NEG, 0.0).astype(bf16)
    accumulate(s + jnp.dot(nonsel, expand, preferred_element_type=f32),
               lambda g, p: lax.dot_general(p, vt[g], nt, preferred_element_type=f32))

    @pl.when(pg == pl.num_programs(1) - 1)
    def _():
        sn = jnp.concatenate([jnp.dot(q_ref[0, g], knew_ref[0, g], preferred_element_type=f32) for g in range(N_KV)], axis=0)
        t_new = lax.broadcasted_iota(jnp.int32, (1, sn.shape[1]), 1)
        s_row = lax.broadcasted_iota(jnp.int32, (sn.shape[0], 1), 0) & (S_PAD - 1)
        new_blk = past // SEL_BLK
        bias_new = nonsel[:, new_blk:new_blk + 1].astype(f32) * NEG
        accumulate(jnp.where((t_new < n_new) & (t_new <= s_row), sn + bias_new, NEG),
                   lambda g, p: jnp.dot(p, vnew_ref[0, g], preferred_element_type=f32))
        o = acc_sc[...] * (1.0 / jnp.maximum(l_sc[...], 1e-30))
        for g in range(N_KV):
            os_ref[0, g] = o[g * rows:(g + 1) * rows]


def _gated_out_kernel(oc_ref, os_ref, ow_ref, gc_ref, gs_ref, gw_ref, x_ref, gt_ref, w_ref, lg_ref, lb_ref, o_ref):
    o = gc_ref[0] * oc_ref[0] + gs_ref[0] * os_ref[0] + gw_ref[0] * ow_ref[0]
    f = jnp.dot(o.astype(bf16), w_ref[...], preferred_element_type=f32)
    y = ALPHA * x_ref[0] + (1.0 + gt_ref[0]) * f
    mu = jnp.mean(y, axis=-1, keepdims=True)
    yc = y - mu
    var = jnp.mean(yc * yc, axis=-1, keepdims=True)
    o_ref[0] = yc * lax.rsqrt(var + LN_EPS) * lg_ref[...] + lb_ref[...]


def nsa_sample_layer(x, shift, scale, gate, cache_kv, page_table, li, win_buf, w_in, w_o, pe, cw1, cb1, cw2, cb2, ln_g, ln_b):
    b_, s_, d_ = x.shape
    n_pool, n_pages = cache_kv.shape[0], page_table.shape[1]
    past = n_pages * PAGE_SIZE
    w_buf = win_buf.shape[1]
    n_ch = past // CMP_STRIDE
    n_sel = past // SEL_BLK + 1
    assert past % SEL_BLK == 0 and s_ <= S_PAD and n_pages % PAGES_PER_STEP == 0
    assert (past + s_ - CMP_BLK) // CMP_STRIDE + 2 == n_ch
    rows_n = b_ * s_
    rq = Q_PER_KV * S_PAD
    flat = lambda m: jnp.broadcast_to(m, (b_, s_, d_)).reshape(1, rows_n, d_)
    xf = x.reshape(1, rows_n, d_)
    cos, sin = _rope_tables(past + (jnp.arange(rows_n) % s_), 128)
    q, rows, win, gates, kvb = nsa_project(xf, flat(shift), flat(scale), _nsa_proj_weights(w_in), cos, sin)

    pages = cache_kv.transpose(0, 1, 3, 4, 5, 2)
    half = cw1.shape[1] // 2
    w01 = jnp.concatenate([cw1[:, :half], cw1[:, half:]], axis=2).astype(bf16)
    cmp_n, cmp_t = compress_kv(cmp_pages(pages, li, page_table, w01), pe, cw1, cb1, cw2, cb2, projected=True)

    q32 = jnp.pad(q.reshape(b_, s_, N_KV, Q_PER_KV, HEAD_DIM), ((0, 0), (0, S_PAD - s_), (0, 0), (0, 0), (0, 0)))
    q32 = q32.transpose(0, 2, 3, 1, 4).reshape(b_, N_KV, rq, HEAD_DIM)

    w_all = jnp.concatenate([win_buf, win.reshape(b_, s_, 2, N_KV, HEAD_DIM)], axis=1)
    n_wk = w_buf + s_
    wk_pad = -(-n_wk // WIN_KEYS_PAD) * WIN_KEYS_PAD
    w_pad = jnp.pad(w_all, ((0, 0), (0, wk_pad - n_wk), (0, 0), (0, 0), (0, 0))).astype(bf16)
    kw = w_pad[:, :, 0].transpose(0, 2, 3, 1)
    vw = _with_ones(w_pad[:, :, 1].transpose(0, 2, 1, 3))
    n_sel_pad = 2 * 128
    msel = jnp.pad(_cmp_to_sel_matrix(n_ch, n_sel), ((0, 0), (0, n_sel_pad - n_sel))).astype(bf16)
    per_b = lambda a: pl.BlockSpec((1,) + a.shape[1:], lambda b: (b,) + (0,) * (a.ndim - 1))
    kct, vc = cmp_t[:, 0].astype(bf16), cmp_n[:, 1].astype(bf16)
    o_shape = jax.ShapeDtypeStruct((b_, N_KV, rq, HEAD_DIM), f32)
    o_spec = pl.BlockSpec((1, N_KV, rq, HEAD_DIM), lambda b: (b, 0, 0, 0))
    oc, ow, imp = pl.pallas_call(
        functools.partial(_sample_cmp_win_kernel, past=past, w_buf=w_buf, n_wk=n_wk),
        grid=(b_,),
        in_specs=[per_b(q32), per_b(kct), per_b(vc), per_b(kw), per_b(vw), pl.BlockSpec(msel.shape, lambda b: (0, 0))],
        out_specs=[o_spec, o_spec, pl.BlockSpec((1, N_KV, S_PAD, n_sel_pad), lambda b: (b, 0, 0, 0))],
        out_shape=[o_shape, o_shape, jax.ShapeDtypeStruct((b_, N_KV, S_PAD, n_sel_pad), f32)],
        compiler_params=pltpu.CompilerParams(dimension_semantics=("arbitrary",)),
        name="sample_cmp_win",
    )(q32, kct, vc, kw, vw, msel)

    sel_rows = -(-n_sel // 8) * 8
    imp_t = imp[..., :sel_rows].transpose(3, 0, 1, 2).reshape(sel_rows, b_ * N_KV * S_PAD)
    nonsel_t = pl.pallas_call(
        functools.partial(_topk_kernel, past=past, n_sel=n_sel),
        out_shape=jax.ShapeDtypeStruct(imp_t.shape, f32),
        name="sample_topk",
    )(imp_t)
    nonsel = nonsel_t.reshape(sel_rows, b_, N_KV, 1, S_PAD).transpose(1, 2, 3, 4, 0)
    nonsel = jnp.broadcast_to(nonsel, (b_, N_KV, Q_PER_KV, S_PAD, sel_rows)).reshape(b_, N_KV * rq, sel_rows)
    nonsel = jnp.pad(nonsel, ((0, 0), (0, 0), (0, n_sel_pad - sel_rows)), constant_values=1.0).astype(bf16)

    new_pad = lambda a: jnp.pad(a.reshape(b_, s_, N_KV, HEAD_DIM), ((0, 0), (0, 128 - s_), (0, 0), (0, 0)))
    knew = new_pad(kvb[..., 0:KD]).transpose(0, 2, 3, 1)
    vnew = new_pad(kvb[..., KD:2 * KD]).transpose(0, 2, 1, 3)
    rows_all = N_KV * rq
    per_b2 = lambda a: pl.BlockSpec((1,) + a.shape[1:], lambda b, pg, pt: (b,) + (0,) * (a.ndim - 1))
    o_s = pl.pallas_call(
        functools.partial(_sample_sel_kernel, past=past, n_new=s_),
        grid_spec=pltpu.PrefetchScalarGridSpec(
            num_scalar_prefetch=1, grid=(b_, n_pages // PAGES_PER_STEP),
            in_specs=_page_specs((1, 1, 2, N_KV, HEAD_DIM, PAGE_SIZE), (li, 1, 0, 0, 0))
            + [per_b2(q32), per_b2(nonsel), per_b2(knew), per_b2(vnew)],
            out_specs=pl.BlockSpec((1, N_KV, rq, HEAD_DIM), lambda b, pg, pt: (b, 0, 0, 0)),
            scratch_shapes=[pltpu.VMEM((rows_all, 1), f32), pltpu.VMEM((rows_all, 1), f32), pltpu.VMEM((rows_all, HEAD_DIM), f32)]),
        out_shape=o_shape,
        compiler_params=pltpu.CompilerParams(dimension_semantics=("arbitrary", "arbitrary"),
                                             vmem_limit_bytes=V7X_VMEM_LIMIT_BYTES),
        name="sample_sel",
    )(page_table, *([pages] * PAGES_PER_STEP), q32, nonsel, knew, vnew)

    tok = lambda o: o.reshape(b_, N_KV, Q_PER_KV, S_PAD, HEAD_DIM)[:, :, :, :s_].transpose(0, 3, 1, 2, 4).reshape(1, rows_n, N_HEADS * HEAD_DIM)
    gexp = lambda br: jnp.repeat(gates[..., br:N_GATE:3], HEAD_DIM, axis=-1)
    full = lambda n: pl.BlockSpec((1, rows_n, n), lambda i: (0, 0, 0))
    vec = pl.BlockSpec((1, d_), lambda i: (0, 0))
    x_new = pl.pallas_call(
        _gated_out_kernel,
        grid=(1,),
        in_specs=[full(N_HEADS * HEAD_DIM)] * 6 + [full(d_), full(d_), pl.BlockSpec(w_o.shape, lambda i: (0, 0)), vec, vec],
        out_specs=full(d_),
        out_shape=jax.ShapeDtypeStruct((1, rows_n, d_), f32),
        name="sample_gated_out",
    )(tok(oc), tok(o_s), tok(ow), gexp(0), gexp(1), gexp(2), xf, flat(gate), w_o.astype(bf16), ln_g.reshape(1, d_), ln_b.reshape(1, d_))
    return x_new.reshape(b_, s_, d_), rows.reshape(b_, s_, 4, N_KV, HEAD_DIM), w_all[:, n_wk - w_buf:]


HPG = SSM_HEADS // SSM_GROUPS
GN = SSM_GROUPS * D_STATE
GW = D_INNER // SSM_GROUPS
DT_PAD = 128
CONV_KEEP = 8


def _ssd_proj_kernel(x_ref, sh_ref, sc_ref, w_ref, z_ref, xbc_ref, dt_ref):
    hb = (x_ref[0] * (1.0 + sc_ref[0]) + sh_ref[0]).astype(bf16)
    z_ref[0] = jnp.dot(hb, w_ref[:, 0:D_INNER], preferred_element_type=f32)
    xbc_ref[0] = jnp.dot(hb, w_ref[:, D_INNER:D_INNER + CONV_DIM], preferred_element_type=f32)
    dt_ref[0] = jnp.dot(hb, w_ref[:, D_INNER + CONV_DIM:], preferred_element_type=f32)


def ssd_project(x, shift, scale, w_in):
    b_, t_, d_ = x.shape
    rows = min(PROJ_ROWS, t_)
    w_all = jnp.pad(w_in, ((0, 0), (0, DT_PAD - SSM_HEADS))).astype(bf16)
    tile = lambda n: pl.BlockSpec((1, rows, n), lambda b, t: (b, t, 0))
    mod = _mod_spec(shift, rows)
    return pl.pallas_call(
        _ssd_proj_kernel,
        grid=(b_, t_ // rows),
        in_specs=[tile(d_), mod, mod, pl.BlockSpec(w_all.shape, lambda b, t: (0, 0), pipeline_mode=pl.Buffered(1))],
        out_specs=[tile(D_INNER), tile(CONV_DIM), tile(DT_PAD)],
        out_shape=[jax.ShapeDtypeStruct((b_, t_, D_INNER), f32), jax.ShapeDtypeStruct((b_, t_, CONV_DIM), f32),
                   jax.ShapeDtypeStruct((b_, t_, DT_PAD), f32)],
        compiler_params=pltpu.CompilerParams(dimension_semantics=("arbitrary", "arbitrary"),
                                             vmem_limit_bytes=V7X_VMEM_LIMIT_BYTES),
        name="ssd_project",
    )(x, shift, scale, w_all)


def _split3(v):
    p1 = v.astype(bf16)
    r1 = v - p1.astype(f32)
    p2 = r1.astype(bf16)
    p3 = (r1 - p2.astype(f32)).astype(bf16)
    return p1, p2, p3


def _dot3(parts, m, left):
    out = None
    for p in parts:
        t = jnp.dot(m, p, preferred_element_type=f32) if left else jnp.dot(p, m, preferred_element_type=f32)
        out = t if out is None else out + t
    return out


def _ssd_scan_kernel(xbc_ref, dt_ref, z_ref, cst_ref, h0_ref, cw_ref, cb_ref, dtb_ref, a_ref, d_ref, ng_ref, ex_ref,
                     y_ref, ht_ref, win_sc, st_sc, *, n_valid):
    c = pl.program_id(1)
    L = SSM_CHUNK

    @pl.when(c == 0)
    def _():
        win_sc[0:CONV_KEEP, :] = cst_ref[0]
        st_sc[...] = h0_ref[0]

    win_sc[CONV_KEEP:CONV_KEEP + L, :] = xbc_ref[0]
    acc = cb_ref[...] + win_sc[pl.ds(CONV_KEEP, L), :] * cw_ref[CONV_W - 1:CONV_W, :]
    for k in range(CONV_W - 1):
        acc = acc + win_sc[pl.ds(CONV_KEEP - (CONV_W - 1) + k, L), :] * cw_ref[k:k + 1, :]
    win_sc[0:CONV_KEEP, :] = win_sc[L:L + CONV_KEEP, :]
    xbc = acc * jax.nn.sigmoid(acc)
    xs = xbc[:, 0:D_INNER]

    t_row = lax.broadcasted_iota(jnp.int32, (L, 1), 0)
    dt_in = dt_ref[0] + dtb_ref[...]
    dt = jnp.maximum(dt_in, 0.0) + jnp.log1p(jnp.exp(-jnp.abs(dt_in)))
    dt = jnp.where(t_row < n_valid, dt, 0.0)
    a = dt * a_ref[...]
    ii = lax.broadcasted_iota(jnp.int32, (L, L), 0)
    jj = lax.broadcasted_iota(jnp.int32, (L, L), 1)
    lower = ii >= jj
    tri = jnp.where(lower, 1.0, 0.0).astype(bf16)
    a_cs = _dot3(_split3(a), tri, left=True)
    a_cs_t = a_cs.T
    ex = ex_ref[...]
    dt_e = _dot3(_split3(dt), ex, left=False)
    acs_e = _dot3(_split3(a_cs), ex, left=False)
    a_tot_e = acs_e[L - 1:L, :]
    xd = xs * dt_e
    xdd = (xd * jnp.exp(a_tot_e - acs_e)).astype(bf16)
    xdb = xd.astype(bf16)
    grow = jnp.exp(acs_e)
    lane_lo = lax.broadcasted_iota(jnp.int32, (L, 2 * SSM_HEAD_DIM), 1) < SSM_HEAD_DIM
    nt = (((1,), (1,)), ((), ()))

    for g in range(SSM_GROUPS):
        gl = slice(g * GW, (g + 1) * GW)
        bm = xbc[:, D_INNER + g * D_STATE:D_INNER + (g + 1) * D_STATE]
        cm = xbc[:, D_INNER + GN + g * D_STATE:D_INNER + GN + (g + 1) * D_STATE].astype(bf16)
        cb = lax.dot_general(cm, bm.astype(bf16), nt, preferred_element_type=f32)
        st_old = st_sc[:, gl]
        y_g = jnp.dot(cm, st_old.astype(bf16), preferred_element_type=f32) * grow[:, gl]
        st_sc[:, gl] = jnp.exp(a_tot_e[:, gl]) * st_old + jnp.dot(bm.T.astype(bf16), xdd[:, gl], preferred_element_type=f32)
        pairs = []
        for k in range(HPG // 2):
            h0 = g * HPG + 2 * k
            ms = []
            for h in (h0, h0 + 1):
                seg = a_cs[:, h:h + 1] - a_cs_t[h:h + 1, :]
                ms.append((cb * jnp.where(lower, jnp.exp(seg), 0.0)).astype(bf16))
            xp = xdb[:, h0 * SSM_HEAD_DIM:(h0 + 2) * SSM_HEAD_DIM]
            pairs.append(jnp.where(lane_lo, jnp.dot(ms[0], xp, preferred_element_type=f32),
                                   jnp.dot(ms[1], xp, preferred_element_type=f32)))
        y_g = y_g + jnp.concatenate(pairs, axis=1) + d_ref[:, gl] * xs[:, gl]
        zg = z_ref[0, :, gl]
        y_g = y_g * (zg * jax.nn.sigmoid(zg))
        y_g = y_g * lax.rsqrt(jnp.mean(y_g * y_g, axis=-1, keepdims=True) + LN_EPS)
        y_ref[0, :, gl] = (y_g * ng_ref[:, gl]).astype(bf16)

    @pl.when(c == pl.num_programs(1) - 1)
    def _():
        ht_ref[0] = st_sc[...]


def ssd_scan(xbc, dt_raw, z, conv_state, h0, conv_w, conv_b, dt_bias, a_log, d_skip, norm_g, n_valid):
    b_, t_, _ = xbc.shape
    L = SSM_CHUNK
    cst = jnp.pad(conv_state, ((0, 0), (CONV_KEEP - (CONV_W - 1), 0), (0, 0)))
    h0t = h0.transpose(0, 3, 1, 2).reshape(b_, D_STATE, D_INNER)
    vec = lambda v: jnp.pad(v, (0, DT_PAD - SSM_HEADS)).reshape(1, DT_PAD)
    ex = (jnp.arange(DT_PAD)[:, None] == (jnp.arange(D_INNER) // SSM_HEAD_DIM)[None, :]).astype(bf16)
    tile = lambda n: pl.BlockSpec((1, L, n), lambda b, c: (b, c, 0))
    per_b = lambda r, n: pl.BlockSpec((1, r, n), lambda b, c: (b, 0, 0))
    const = lambda r, n: pl.BlockSpec((r, n), lambda b, c: (0, 0))
    y, ht = pl.pallas_call(
        functools.partial(_ssd_scan_kernel, n_valid=n_valid),
        grid=(b_, t_ // L),
        in_specs=[tile(CONV_DIM), tile(DT_PAD), tile(D_INNER), per_b(CONV_KEEP, CONV_DIM), per_b(D_STATE, D_INNER),
                  const(CONV_W, CONV_DIM), const(1, CONV_DIM), const(1, DT_PAD), const(1, DT_PAD), const(1, D_INNER),
                  const(1, D_INNER), const(DT_PAD, D_INNER)],
        out_specs=[tile(D_INNER), per_b(D_STATE, D_INNER)],
        out_shape=[jax.ShapeDtypeStruct((b_, t_, D_INNER), bf16), jax.ShapeDtypeStruct((b_, D_STATE, D_INNER), f32)],
        scratch_shapes=[pltpu.VMEM((L + CONV_KEEP, CONV_DIM), f32), pltpu.VMEM((D_STATE, D_INNER), f32)],
        compiler_params=pltpu.CompilerParams(dimension_semantics=("arbitrary", "arbitrary"),
                                             vmem_limit_bytes=V7X_VMEM_LIMIT_BYTES),
        name="ssd_scan",
    )(xbc, dt_raw, z, cst, h0t, conv_w, conv_b.reshape(1, CONV_DIM), vec(dt_bias), vec(-jnp.exp(a_log)),
      jnp.repeat(d_skip, SSM_HEAD_DIM).reshape(1, D_INNER), norm_g.reshape(1, D_INNER), ex)
    return y, ht.reshape(b_, D_STATE, SSM_HEADS, SSM_HEAD_DIM).transpose(0, 2, 3, 1)


def ssd_layer(x, shift, scale, gate, conv_state, ssm_state, w_in, conv_w, conv_b, dt_bias, a_log, d_skip, norm_g, w_out,
              ln_g, ln_b):
    b_, t_, d_ = x.shape
    fold = t_ < SSM_CHUNK
    if fold:
        flat = lambda m: jnp.broadcast_to(m, (b_, t_, d_)).reshape(1, b_ * t_, d_)
        xf, shift, scale, gate = x.reshape(1, b_ * t_, d_), flat(shift), flat(scale), flat(gate)
    else:
        xf = x
    z, xbc, dt_raw = ssd_project(xf, shift, scale, w_in)
    if fold:
        padt = lambda v: jnp.pad(v.reshape(b_, t_, -1), ((0, 0), (0, SSM_CHUNK - t_), (0, 0)))
        z, xbc, dt_raw = padt(z), padt(xbc), padt(dt_raw)
    y, new_ssm = ssd_scan(xbc, dt_raw, z, conv_state, ssm_state, conv_w, conv_b, dt_bias, a_log, d_skip, norm_g, t_)
    new_conv = jnp.concatenate([conv_state, xbc[:, :t_]], axis=1)[:, t_:]
    if fold:
        y = y[:, :t_].reshape(1, b_ * t_, D_INNER)
    x_new = mixer_out(y, xf, gate, w_out.astype(bf16), ln_g, ln_b).reshape(b_, t_, d_)
    return x_new, new_conv, new_ssm


ADA_COLS = 1152


def _adaln_kernel(c_ref, w_ref, b_ref, o_ref):
    c = c_ref[...]
    h = (c * jax.nn.sigmoid(c)).astype(bf16)
    o_ref[...] = jnp.dot(h, w_ref[...].astype(bf16), preferred_element_type=f32) + b_ref[...]


def adaln(c, w, b):
    r_, d_ = c.shape
    n_ = w.shape[1]
    return pl.pallas_call(
        _adaln_kernel,
        grid=(n_ // ADA_COLS,),
        in_specs=[pl.BlockSpec((r_, d_), lambda n: (0, 0)), pl.BlockSpec((d_, ADA_COLS), lambda n: (0, n)),
                  pl.BlockSpec((1, ADA_COLS), lambda n: (0, n))],
        out_specs=pl.BlockSpec((r_, ADA_COLS), lambda n: (0, n)),
        out_shape=jax.ShapeDtypeStruct((r_, n_), f32),
        compiler_params=pltpu.CompilerParams(dimension_semantics=("arbitrary",)),
        name="adaln",
    )(c, w, b.reshape(1, n_))


def _trunk_layer(x, m, i, mixer, ln_g, ln_b, ffn_a, ffn_b):
    b_, t_, d_ = x.shape
    m = m.reshape(b_, 3, 3, 1, D_MODEL)

    def ffn(v, s, w):
        mods = [m[:, s, k] for k in range(3)]
        if t_ < 8:
            mods = [jnp.broadcast_to(z, (b_, t_, d_)).reshape(1, b_ * t_, d_) for z in mods]
            return ffn_sublayer(v.reshape(1, b_ * t_, d_), *mods, *w, ln_g[i, s], ln_b[i, s]).reshape(b_, t_, d_)
        return ffn_sublayer(v, *mods, *w, ln_g[i, s], ln_b[i, s])

    x = ffn(x, 0, ffn_a)
    x, st = mixer(x, m[:, 1, 0], m[:, 1, 1], m[:, 1, 2], ln_g[i, 1], ln_b[i, 1])
    x = ffn(x, 2, ffn_b)
    return x, st


def kernel(x_prompt, x_sample, cache_kv, cache_win, state_conv, state_ssm, page_table, c_prompt, c_sample, ada_w, ada_b, ln_g, ln_b, ffn_w1, ffn_w3, ffn_w2, nsa_w_in, nsa_w_o, nsa_cmp_pe, nsa_cmp_w1, nsa_cmp_b1, nsa_cmp_w2, nsa_cmp_b2, ssm_w_in, ssm_conv_w, ssm_conv_b, ssm_dt_bias, ssm_a_log, ssm_d, ssm_norm_g, ssm_w_out):
    xp, xs = x_prompt, x_sample
    nb_p, nb_s = c_prompt.shape[0], c_sample.shape[0]
    c_all = jnp.concatenate([c_prompt, c_sample], axis=0)
    c_all = jnp.pad(c_all, ((0, -(nb_p + nb_s) % 8), (0, 0)))
    kv_p, kv_s, win_p, win_s, conv_p, conv_s, ssm_p, ssm_s = [], [], [], [], [], [], [], []
    for i in range(DEPTH):
        j = i // N_MIXERS
        ffn_a = _chunk_ffn_weights(ffn_w1[i, 0], ffn_w3[i, 0], ffn_w2[i, 0])
        ffn_b = _chunk_ffn_weights(ffn_w1[i, 1], ffn_w3[i, 1], ffn_w2[i, 1])
        m_all = adaln(c_all, ada_w[i], ada_b[i])
        m_p, m_s = m_all[:nb_p], m_all[nb_p:nb_p + nb_s]
        lw = (ln_g, ln_b, ffn_a, ffn_b)
        if i % N_MIXERS == 0:
            nw = (nsa_w_in[j], nsa_w_o[j], nsa_cmp_pe[j], nsa_cmp_w1[j], nsa_cmp_b1[j], nsa_cmp_w2[j], nsa_cmp_b2[j])
            def prompt_mixer(x, sh, sc, gt, lg, lb):
                x_new, rows, win = nsa_prompt_layer(x, sh, sc, gt, *nw, lg, lb)
                return x_new, (rows, win[:, x.shape[1] - min(WINDOW, x.shape[1]):])

            xp, (r_p, w_p) = _trunk_layer(xp, m_p, i, prompt_mixer, *lw)
            def sample_mixer(x, sh, sc, gt, lg, lb):
                x_new, rows, win = nsa_sample_layer(x, sh, sc, gt, cache_kv, page_table, j, cache_win[:, j], *nw, lg, lb)
                return x_new, (rows, win)

            xs, (r_s, w_s) = _trunk_layer(xs, m_s, i, sample_mixer, *lw)
            kv_p.append(r_p); kv_s.append(r_s); win_p.append(w_p); win_s.append(w_s)
        else:
            sw = (ssm_w_in[j], ssm_conv_w[j], ssm_conv_b[j], ssm_dt_bias[j], ssm_a_log[j], ssm_d[j], ssm_norm_g[j], ssm_w_out[j])
            zc = jnp.zeros((xp.shape[0], CONV_W - 1, CONV_DIM), f32)
            zs = jnp.zeros((xp.shape[0], SSM_HEADS, SSM_HEAD_DIM, D_STATE), f32)
            def ssd_mixer(conv0, ssm0):
                def run(x, sh, sc, gt, lg, lb):
                    x_new, cv, st = ssd_layer(x, sh, sc, gt, conv0, ssm0, *sw, lg, lb)
                    return x_new, (cv, st)
                return run

            xp, (cv_p, st_p) = _trunk_layer(xp, m_p, i, ssd_mixer(zc, zs), *lw)
            xs, (cv_s, st_s) = _trunk_layer(xs, m_s, i, ssd_mixer(state_conv[:, j], state_ssm[:, j]), *lw)
            conv_p.append(cv_p); conv_s.append(cv_s); ssm_p.append(st_p); ssm_s.append(st_s)
    return (xp, xs, jnp.stack(kv_p, 1), jnp.stack(kv_s, 1), jnp.stack(win_p, 1), jnp.stack(win_s, 1),
            jnp.stack(conv_p, 1), jnp.stack(conv_s, 1), jnp.stack(ssm_p, 1), jnp.stack(ssm_s, 1))
```

```python
import functools
import math

import jax
import jax.numpy as jnp
from jax import lax
from jax.experimental import pallas as pl
from jax.experimental.pallas import tpu as pltpu

f32 = jnp.float32
bf16 = jnp.bfloat16

D_MODEL = 1024
DEPTH = 4
PAGE_SIZE = 128
N_HEADS = 16
HEAD_DIM = 64
N_KV = 4
Q_PER_KV = N_HEADS // N_KV
CMP_BLK = 32
CMP_STRIDE = 16
CMP_HIDDEN = 2 * HEAD_DIM
SEL_BLK = 64
TOPK = 16
WINDOW = 512
Q_BLOCK = 256
ROPE_THETA = 10000.0
D_INNER = 2 * D_MODEL
SSM_HEAD_DIM = 64
SSM_HEADS = D_INNER // SSM_HEAD_DIM
SSM_GROUPS = 4
D_STATE = 128
CONV_W = 4
CONV_DIM = D_INNER + 2 * SSM_GROUPS * D_STATE
SSM_CHUNK = 128
D_FF = 256 * ((8 * D_MODEL // 3 + 255) // 256)
N_MIXERS = 2
ALPHA = (2 * DEPTH) ** 0.25
N_ADA = 9
LN_EPS = 1e-5

V7X_VMEM_LIMIT_BYTES = 56 * 1024 * 1024
FF_CHUNK = 256
FFN_ROWS = 512


def _ffn_kernel(x_ref, sh_ref, sc_ref, gt_ref, w1_ref, w3_ref, w2_ref, lg_ref, lb_ref, o_ref, acc_ref):
    x = x_ref[0]
    hb = (x * (1.0 + sc_ref[0]) + sh_ref[0]).astype(bf16)
    acc_ref[...] = jnp.zeros_like(acc_ref)

    def chunk(c, carry):
        a = jnp.dot(hb, w1_ref[c], preferred_element_type=f32)
        b = jnp.dot(hb, w3_ref[c], preferred_element_type=f32)
        g = (a * jax.nn.sigmoid(a) * b).astype(bf16)
        acc_ref[...] += jnp.dot(g, w2_ref[c], preferred_element_type=f32)
        return carry

    lax.fori_loop(0, w1_ref.shape[0], chunk, 0)
    y = ALPHA * x + (1.0 + gt_ref[0]) * (0.5 * acc_ref[...])
    mu = jnp.mean(y, axis=-1, keepdims=True)
    yc = y - mu
    var = jnp.mean(yc * yc, axis=-1, keepdims=True)
    o_ref[0] = yc * lax.rsqrt(var + LN_EPS) * lg_ref[...] + lb_ref[...]


def _mod_spec(m, rows):
    if m.shape[1] == 1:
        return pl.BlockSpec((1, 1, m.shape[2]), lambda b, t: (b, 0, 0))
    return pl.BlockSpec((1, rows, m.shape[2]), lambda b, t: (b, t, 0))


def ffn_sublayer(x, shift, scale, gate, w1c, w3c, w2c, ln_g, ln_b):
    b_, t_, d_ = x.shape
    rows = min(FFN_ROWS, t_)
    mod_spec = _mod_spec(shift, rows)
    const3 = lambda b, t: (0, 0, 0)
    wspec = lambda w: pl.BlockSpec(w.shape, const3, pipeline_mode=pl.Buffered(1))
    vec = pl.BlockSpec((1, d_), lambda b, t: (0, 0))
    return pl.pallas_call(
        _ffn_kernel,
        grid=(b_, t_ // rows),
        in_specs=[pl.BlockSpec((1, rows, d_), lambda b, t: (b, t, 0)), mod_spec, mod_spec, mod_spec,
                  wspec(w1c), wspec(w3c), wspec(w2c), vec, vec],
        out_specs=pl.BlockSpec((1, rows, d_), lambda b, t: (b, t, 0)),
        out_shape=jax.ShapeDtypeStruct(x.shape, f32),
        scratch_shapes=[pltpu.VMEM((rows, d_), f32)],
        compiler_params=pltpu.CompilerParams(dimension_semantics=("arbitrary", "arbitrary"),
                                             vmem_limit_bytes=V7X_VMEM_LIMIT_BYTES),
        name="ffn_sublayer",
    )(x, shift, scale, gate, w1c, w3c, w2c, ln_g.reshape(1, d_), ln_b.reshape(1, d_))


def _chunk_ffn_weights(w1, w3, w2):
    n = D_FF // FF_CHUNK
    w1c = w1.astype(bf16).reshape(D_MODEL, n, FF_CHUNK).transpose(1, 0, 2)
    w3c = w3.astype(bf16).reshape(D_MODEL, n, FF_CHUNK).transpose(1, 0, 2)
    w2c = w2.astype(bf16).reshape(n, FF_CHUNK, D_MODEL)
    return w1c, w3c, w2c


NEG = -1e30
PROJ_ROWS = 512
SEL_TK = 512
KD = N_KV * HEAD_DIM
_C_Q, _C_QR, _C_CMP, _C_SEL, _C_WIN, _C_G, _C_END = 0, 1024, 2048, 2560, 3328, 4096, 4224
N_GATE = 3 * N_HEADS


def _rot_cols(w):
    w4 = w.reshape(w.shape[0], -1, 2, HEAD_DIM // 2)
    return jnp.stack([-w4[:, :, 1], w4[:, :, 0]], axis=2).reshape(w.shape)


def _nsa_proj_weights(w_in):
    qd = N_HEADS * HEAD_DIM
    wq = w_in[:, :qd] * HEAD_DIM ** -0.5
    kc, vc, ks, vs, kw, vw = [w_in[:, qd + i * KD: qd + (i + 1) * KD] for i in range(6)]
    wg = jnp.pad(w_in[:, qd + 6 * KD:], ((0, 0), (0, 128 - N_GATE)))
    return jnp.concatenate([wq, _rot_cols(wq), kc, vc, ks, _rot_cols(ks), vs, kw, _rot_cols(kw), vw, wg], axis=1).astype(bf16)


def _rope_tables(pos, width):
    half = HEAD_DIM // 2
    inv = ROPE_THETA ** (-jnp.arange(half, dtype=f32) / half)
    ang = jnp.tile(pos.astype(f32)[:, None] * inv, (1, width // half))
    return jnp.cos(ang), jnp.sin(ang)


def _nsa_proj_kernel(x_ref, sh_ref, sc_ref, w_ref, cos_ref, sin_ref, q_ref, rows_ref, win_ref, gate_ref, kvb_ref):
    hb = (x_ref[0] * (1.0 + sc_ref[0]) + sh_ref[0]).astype(bf16)
    cos, sin = cos_ref[...], sin_ref[...]

    def mm(lo, hi):
        return jnp.dot(hb, w_ref[:, lo:hi], preferred_element_type=f32)

    def rope(a, b, c):
        return a[:, c * 128:(c + 1) * 128] * cos + b[:, c * 128:(c + 1) * 128] * sin

    qa, qb = mm(_C_Q, _C_QR), mm(_C_QR, _C_CMP)
    for c in range(N_HEADS * HEAD_DIM // 128):
        q_ref[0, :, c * 128:(c + 1) * 128] = rope(qa, qb, c).astype(bf16)
    rows_ref[0, :, 0:2 * KD] = mm(_C_CMP, _C_SEL)
    sel = mm(_C_SEL, _C_WIN)
    win = mm(_C_WIN, _C_G)
    for c in range(KD // 128):
        ks = rope(sel[:, 0:KD], sel[:, KD:2 * KD], c)
        kw = rope(win[:, 0:KD], win[:, KD:2 * KD], c)
        rows_ref[0, :, 2 * KD + c * 128:2 * KD + (c + 1) * 128] = ks
        win_ref[0, :, c * 128:(c + 1) * 128] = kw
        kvb_ref[0, :, c * 128:(c + 1) * 128] = ks.astype(bf16)
        kvb_ref[0, :, 2 * KD + c * 128:2 * KD + (c + 1) * 128] = kw.astype(bf16)
    rows_ref[0, :, 3 * KD:4 * KD] = sel[:, 2 * KD:3 * KD]
    win_ref[0, :, KD:2 * KD] = win[:, 2 * KD:3 * KD]
    kvb_ref[0, :, KD:2 * KD] = sel[:, 2 * KD:3 * KD].astype(bf16)
    kvb_ref[0, :, 3 * KD:4 * KD] = win[:, 2 * KD:3 * KD].astype(bf16)
    gate_ref[0] = jax.nn.sigmoid(mm(_C_G, _C_END))


def nsa_project(x, shift, scale, w_all, cos, sin):
    b_, t_, d_ = x.shape
    rows = min(PROJ_ROWS, t_)
    tile = lambda n: pl.BlockSpec((1, rows, n), lambda b, t: (b, t, 0))
    mod = _mod_spec(shift, rows)
    tab = pl.BlockSpec((rows, 128), lambda b, t: (t, 0))
    return pl.pallas_call(
        _nsa_proj_kernel,
        grid=(b_, t_ // rows),
        in_specs=[tile(d_), mod, mod, pl.BlockSpec(w_all.shape, lambda b, t: (0, 0), pipeline_mode=pl.Buffered(1)), tab, tab],
        out_specs=[tile(4 * KD), tile(4 * KD), tile(2 * KD), tile(128), tile(4 * KD)],
        out_shape=[jax.ShapeDtypeStruct((b_, t_, 4 * KD), bf16), jax.ShapeDtypeStruct((b_, t_, 4 * KD), f32),
                   jax.ShapeDtypeStruct((b_, t_, 2 * KD), f32), jax.ShapeDtypeStruct((b_, t_, 128), f32),
                   jax.ShapeDtypeStruct((b_, t_, 4 * KD), bf16)],
        compiler_params=pltpu.CompilerParams(dimension_semantics=("arbitrary", "arbitrary"),
                                             vmem_limit_bytes=V7X_VMEM_LIMIT_BYTES),
        name="nsa_project",
    )(x, shift, scale, w_all, cos, sin)


def _gelu_tanh(x):
    return 0.5 * x * (1.0 + jnp.tanh(math.sqrt(2.0 / math.pi) * (x + 0.044715 * (x * x * x))))


def _compress_kernel(xa_ref, xb_ref, pe_ref, w1_ref, b1_ref, w2_ref, w2r_ref, b2_ref, b2r_ref, w2t_ref, w2rt_ref,
                     b2t_ref, b2rt_ref, cos_ref, sin_ref, cost_ref, sint_ref, on_ref, ot_ref, *, projected):
    w1 = w1_ref[0]
    half = w1.shape[0] // 2
    bias = jnp.dot(pe_ref[0], w1, preferred_element_type=f32)[0:1] + b1_ref[0]
    nt = (((1,), (1,)), ((), ()))
    for g in range(xa_ref.shape[2]):
        if projected:
            pre = xa_ref[0, 0, g] + xb_ref[0, 0, g] + bias
        else:
            pre = (jnp.dot(xa_ref[0, 0, g], w1[:half], preferred_element_type=f32)
                   + jnp.dot(xb_ref[0, 0, g], w1[half:], preferred_element_type=f32) + bias)
        hb = _gelu_tanh(pre).astype(bf16)
        y = jnp.dot(hb, w2_ref[0], preferred_element_type=f32) + b2_ref[0]
        yr = jnp.dot(hb, w2r_ref[0], preferred_element_type=f32) + b2r_ref[0]
        on_ref[0, 0, g] = y * cos_ref[0] + yr * sin_ref[0]
        yt = lax.dot_general(w2t_ref[0], hb, nt, preferred_element_type=f32) + b2t_ref[0]
        yrt = lax.dot_general(w2rt_ref[0], hb, nt, preferred_element_type=f32) + b2rt_ref[0]
        ot_ref[0, 0, g] = yt * cost_ref[0] + yrt * sint_ref[0]


def compress_kv(chunks, pe, w1, b1, w2, b2, projected=False):
    b_, _, g_, n_ch, feat = chunks.shape
    if projected:
        feat = CMP_HIDDEN
        first, second = chunks[..., :feat], chunks[..., feat:]
    else:
        first = second = chunks
    chunks = first
    nxt = jnp.concatenate([second[:, :, :, 1:], jnp.zeros_like(second[:, :, :, :1])], axis=3)
    cend = jnp.arange(n_ch) * CMP_STRIDE + CMP_BLK - 1
    cos, sin = _rope_tables(cend, HEAD_DIM)
    cos = jnp.stack([cos, jnp.ones_like(cos)])
    sin = jnp.stack([sin, jnp.zeros_like(sin)])
    w2b = w2.astype(bf16)
    w2r = _rot_cols(w2).astype(bf16)
    b2r = _rot_cols(b2[:, None, :])
    pe8 = jnp.broadcast_to(pe.reshape(2, 1, -1), (2, 8, pe.shape[1] * pe.shape[2])).astype(bf16)
    xspec = pl.BlockSpec((1, 1, g_, n_ch, feat), lambda b, k: (b, k, 0, 0, 0))
    per_kv = lambda *s: pl.BlockSpec((1,) + s, lambda b, k: (k,) + (0,) * len(s))
    return pl.pallas_call(
        functools.partial(_compress_kernel, projected=projected),
        grid=(b_, 2),
        in_specs=[xspec, xspec, per_kv(8, pe8.shape[2]), per_kv(*w1.shape[1:]), per_kv(1, CMP_HIDDEN),
                  per_kv(CMP_HIDDEN, HEAD_DIM), per_kv(CMP_HIDDEN, HEAD_DIM), per_kv(1, HEAD_DIM), per_kv(1, HEAD_DIM),
                  per_kv(HEAD_DIM, CMP_HIDDEN), per_kv(HEAD_DIM, CMP_HIDDEN), per_kv(HEAD_DIM, 1), per_kv(HEAD_DIM, 1),
                  per_kv(n_ch, HEAD_DIM), per_kv(n_ch, HEAD_DIM), per_kv(HEAD_DIM, n_ch), per_kv(HEAD_DIM, n_ch)],
        out_specs=[pl.BlockSpec((1, 1, g_, n_ch, HEAD_DIM), lambda b, k: (b, k, 0, 0, 0)),
                   pl.BlockSpec((1, 1, g_, HEAD_DIM, n_ch), lambda b, k: (b, k, 0, 0, 0))],
        out_shape=[jax.ShapeDtypeStruct((b_, 2, g_, n_ch, HEAD_DIM), f32),
                   jax.ShapeDtypeStruct((b_, 2, g_, HEAD_DIM, n_ch), f32)],
        compiler_params=pltpu.CompilerParams(dimension_semantics=("arbitrary",) * 2,
                                             vmem_limit_bytes=V7X_VMEM_LIMIT_BYTES),
        name="compress_kv",
    )(chunks, nxt, pe8, w1.astype(bf16), b1[:, None, :], w2b, w2r, b2[:, None, :], b2r,
      w2b.transpose(0, 2, 1), w2r.transpose(0, 2, 1), b2[:, :, None], b2r.transpose(0, 2, 1),
      cos, sin, cos.transpose(0, 2, 1), sin.transpose(0, 2, 1))


def _cmp_to_sel_matrix(n_c, n_sel):
    cs = jnp.arange(n_c)[:, None] * CMP_STRIDE
    ss = jnp.arange(n_sel)[None, :] * SEL_BLK
    ov = jnp.minimum(cs + CMP_BLK, ss + SEL_BLK) - jnp.maximum(cs, ss)
    return jnp.clip(ov, 0, None).astype(f32) / CMP_BLK


def _lanes(col, n):
    if n % 128:
        return jnp.broadcast_to(col, (col.shape[0], n))
    tile = jnp.broadcast_to(col, (col.shape[0], 128))
    return tile if n == 128 else jnp.concatenate([tile] * (n // 128), axis=1)


def _softmax_rows(s, valid):
    n = s.shape[1]
    e = jnp.where(valid, jnp.exp(s - _lanes(jnp.max(s, axis=1, keepdims=True), n)), 0.0)
    l = jnp.sum(e, axis=1, keepdims=True)
    return e * _lanes(1.0 / jnp.maximum(l, 1e-30), n)


def _unselected_blocks(imp_t, qpos, n_sel):
    jj = lax.broadcasted_iota(jnp.int32, imp_t.shape, 0)
    cur = qpos // SEL_BLK
    forced = (jj == 0) | (jj == cur) | (jj == cur - 1)
    sc0 = jnp.where((jj <= cur) & (jj < n_sel), jnp.where(forced, -NEG, imp_t), NEG)
    jjf = jj.astype(f32)

    def pick(_, carry):
        sc, nonsel = carry
        m = jnp.max(sc, axis=0, keepdims=True)
        idx = jnp.min(jnp.where(sc == m, jjf, float(imp_t.shape[0])), axis=0, keepdims=True)
        hit = (jjf == idx) & (m > NEG)
        return jnp.where(hit, NEG, sc), jnp.where(hit, 0.0, nonsel)

    return lax.fori_loop(0, min(TOPK, n_sel), pick, (sc0, jnp.ones(imp_t.shape, f32)))[1]


def _nsa_attn_kernel(q_ref, kct_ref, vc_ref, ks_ref, vs_ref, *refs, n_cmp, n_sel):
    n_win = WINDOW // Q_BLOCK + 1
    kw_refs, vw_refs = refs[:n_win], refs[n_win:2 * n_win]
    gate_ref, msel_ref, o_ref, oc_sc, score_sc, nonsel_sc, m_sc, acc_sc = refs[2 * n_win:]
    qt = pl.program_id(1)
    s0 = qt * Q_BLOCK
    rq = Q_PER_KV * Q_BLOCK
    row_q = s0 + (lax.broadcasted_iota(jnp.int32, (rq, 1), 0) & (Q_BLOCK - 1))

    def cmp_branch(ncol):
        cend = lax.broadcasted_iota(jnp.int32, (1, ncol), 1) * CMP_STRIDE + (CMP_BLK - 1)
        cvalid = cend <= row_q
        for g in range(N_KV):
            s = jnp.dot(q_ref[0, g, 0], kct_ref[0, g, :, 0:ncol], preferred_element_type=f32)
            p = _softmax_rows(jnp.where(cvalid, s, NEG), cvalid)
            oc_sc[g] = jnp.dot(p.astype(bf16), vc_ref[0, g, 0:ncol, :], preferred_element_type=f32)
            ps = p[0:Q_BLOCK]
            for r in range(1, Q_PER_KV):
                ps = ps + p[r * Q_BLOCK:(r + 1) * Q_BLOCK]
            hi = ps.astype(bf16)
            lo = (ps - hi.astype(f32)).astype(bf16)
            imp = (jnp.dot(hi, msel_ref[0:ncol, :], preferred_element_type=f32)
                   + jnp.dot(lo, msel_ref[0:ncol, :], preferred_element_type=f32))
            score_sc[:, g * Q_BLOCK:(g + 1) * Q_BLOCK] = imp.T

    widths = [w for w in range(128, n_cmp + 1, 128)] or [n_cmp]
    n_live = (s0 + Q_BLOCK - CMP_BLK) // CMP_STRIDE + 1
    lax.switch(jnp.clip((n_live + 127) // 128 - 1, 0, len(widths) - 1), [functools.partial(cmp_branch, w) for w in widths])

    qpos = s0 + (lax.broadcasted_iota(jnp.int32, (n_sel, N_KV * Q_BLOCK), 1) & (Q_BLOCK - 1))
    nonsel = _unselected_blocks(score_sc[...], qpos, n_sel)
    for g in range(N_KV):
        nonsel_sc[g] = nonsel[:, g * Q_BLOCK:(g + 1) * Q_BLOCK].T.astype(bf16)

    n_tiles = (s0 + Q_BLOCK + SEL_TK - 1) // SEL_TK
    key_l = lax.broadcasted_iota(jnp.int32, (1, SEL_TK), 1)
    blk_j = lax.broadcasted_iota(jnp.int32, (n_sel, SEL_TK), 0)
    blk_l = lax.broadcasted_iota(jnp.int32, (n_sel, SEL_TK), 1)
    wkey_l = lax.broadcasted_iota(jnp.int32, (1, Q_BLOCK), 1)
    m_sc[...] = jnp.full_like(m_sc, NEG)
    acc_sc[...] = jnp.zeros_like(acc_sc)

    def sel_tile(kt, causal):
        expand = jnp.where(blk_j == (kt * SEL_TK + blk_l) // SEL_BLK, NEG, 0.0).astype(bf16)
        ss = []
        for g in range(N_KV):
            s = jnp.dot(q_ref[0, g, 0], ks_ref[0, g, kt], preferred_element_type=f32)
            bias = jnp.dot(nonsel_sc[g], expand, preferred_element_type=f32)
            s = s + jnp.concatenate([bias] * Q_PER_KV, axis=0)
            if causal:
                s = jnp.where(kt * SEL_TK + key_l <= row_q, s, NEG)
            ss.append(s)
        for g in range(N_KV):
            m_old = m_sc[g]
            m_new = jnp.maximum(m_old, jnp.broadcast_to(jnp.max(ss[g], axis=1, keepdims=True), m_old.shape))
            p = jnp.exp(ss[g] - jnp.concatenate([m_new] * (SEL_TK // 128), axis=1)).astype(bf16)
            acc_sc[g] = jnp.exp(m_old - m_new) * acc_sc[g] + jnp.dot(p, vs_ref[0, g, kt], preferred_element_type=f32)
            m_sc[g] = m_new

    def body(kt, carry):
        sel_tile(kt, False)
        return carry

    lax.fori_loop(0, n_tiles - 1, body, 0)
    sel_tile(n_tiles - 1, True)

    for g in range(N_KV):
        q = q_ref[0, g, 0]
        acc = acc_sc[g]
        o_s = acc[:, 0:HEAD_DIM] * (1.0 / jnp.maximum(acc[:, HEAD_DIM:HEAD_DIM + 1], 1e-30))

        s_parts = []
        for i in range(n_win):
            kpos = (qt - (n_win - 1) + i) * Q_BLOCK + wkey_l
            dpos = row_q - kpos
            ok = (kpos >= 0) & (dpos >= 0) & (dpos < WINDOW)
            s_parts.append(jnp.where(ok, jnp.dot(q, kw_refs[i][0, g, 0], preferred_element_type=f32), NEG))
        s_w = jnp.concatenate(s_parts, axis=1)
        e_w = jnp.exp(s_w - _lanes(jnp.max(s_w, axis=1, keepdims=True), s_w.shape[1])).astype(bf16)
        acc_w = jnp.dot(e_w[:, 0:Q_BLOCK], vw_refs[0][0, g, 0], preferred_element_type=f32)
        for i in range(1, n_win):
            acc_w = acc_w + jnp.dot(e_w[:, i * Q_BLOCK:(i + 1) * Q_BLOCK], vw_refs[i][0, g, 0], preferred_element_type=f32)
        o_w = acc_w[:, 0:HEAD_DIM] * (1.0 / jnp.maximum(acc_w[:, HEAD_DIM:HEAD_DIM + 1], 1e-30))

        o_c = oc_sc[g]
        gates = gate_ref[0]
        for r in range(Q_PER_KV):
            c = (g * Q_PER_KV + r) * 3
            rows = slice(r * Q_BLOCK, (r + 1) * Q_BLOCK)
            o = (gates[:, c:c + 1] * o_c[rows] + gates[:, c + 1:c + 2] * o_s[rows] + gates[:, c + 2:c + 3] * o_w[rows])
            o_ref[0, g, 0, rows, :] = o.astype(bf16)


def _with_ones(v):
    pad = jnp.zeros(v.shape[:-1] + (128 - v.shape[-1] - 1,), v.dtype)
    return jnp.concatenate([v, jnp.ones(v.shape[:-1] + (1,), v.dtype), pad], axis=-1)


def nsa_attention(q, kvb, gates, kct, vc):
    b_, t_, _ = q.shape
    n_qt, n_st, n_sel = t_ // Q_BLOCK, t_ // SEL_TK, t_ // SEL_BLK
    n_cmp = kct.shape[-1]
    rq = Q_PER_KV * Q_BLOCK
    q5 = q.reshape(b_, n_qt, Q_BLOCK, N_KV, Q_PER_KV, HEAD_DIM).transpose(0, 3, 1, 4, 2, 5).reshape(b_, N_KV, n_qt, rq, HEAD_DIM)

    def tiles(x, tk):
        return x.reshape(b_, t_ // tk, tk, N_KV, HEAD_DIM).transpose(0, 3, 1, 2, 4)

    ks = tiles(kvb[..., 0:KD], SEL_TK).transpose(0, 1, 2, 4, 3)
    vs = _with_ones(tiles(kvb[..., KD:2 * KD], SEL_TK))
    kw = tiles(kvb[..., 2 * KD:3 * KD], Q_BLOCK).transpose(0, 1, 2, 4, 3)
    vw = _with_ones(tiles(kvb[..., 3 * KD:4 * KD], Q_BLOCK))
    msel = _cmp_to_sel_matrix(n_cmp, n_sel).astype(bf16)
    per_b = lambda x: pl.BlockSpec((1,) + x.shape[1:], lambda b, t: (b,) + (0,) * (x.ndim - 1), pipeline_mode=pl.Buffered(1))
    qspec = pl.BlockSpec((1, N_KV, 1, rq, HEAD_DIM), lambda b, t: (b, 0, t, 0, 0))
    n_win = WINDOW // Q_BLOCK + 1

    def win_specs(x):
        def spec(i):
            return pl.BlockSpec((1, N_KV, 1) + x.shape[3:], lambda b, t: (b, 0, jnp.maximum(t - (n_win - 1) + i, 0), 0, 0))
        return [spec(i) for i in range(n_win)]

    o5 = pl.pallas_call(
        functools.partial(_nsa_attn_kernel, n_cmp=n_cmp, n_sel=n_sel),
        grid=(b_, n_qt),
        in_specs=[qspec, per_b(kct), per_b(vc), per_b(ks), per_b(vs)] + win_specs(kw) + win_specs(vw)
        + [pl.BlockSpec((1, Q_BLOCK, 128), lambda b, t: (b, t, 0)), pl.BlockSpec(msel.shape, lambda b, t: (0, 0))],
        out_specs=qspec,
        out_shape=jax.ShapeDtypeStruct(q5.shape, bf16),
        scratch_shapes=[pltpu.VMEM((N_KV, rq, HEAD_DIM), f32), pltpu.VMEM((n_sel, N_KV * Q_BLOCK), f32),
                        pltpu.VMEM((N_KV, Q_BLOCK, n_sel), bf16), pltpu.VMEM((N_KV, rq, 128), f32), pltpu.VMEM((N_KV, rq, 128), f32)],
        compiler_params=pltpu.CompilerParams(dimension_semantics=("arbitrary", "arbitrary"),
                                             vmem_limit_bytes=V7X_VMEM_LIMIT_BYTES),
        name="nsa_attention",
    )(q5, kct, vc, ks, vs, *([kw] * n_win), *([vw] * n_win), gates, msel)
    return o5.reshape(b_, N_KV, n_qt, Q_PER_KV, Q_BLOCK, HEAD_DIM).transpose(0, 2, 4, 1, 3, 5).reshape(b_, t_, N_HEADS * HEAD_DIM)


def _mixer_out_kernel(y_ref, x_ref, gt_ref, w_ref, lg_ref, lb_ref, o_ref):
    f = jnp.dot(y_ref[0], w_ref[...], preferred_element_type=f32)
    y = ALPHA * x_ref[0] + (1.0 + gt_ref[0]) * f
    mu = jnp.mean(y, axis=-1, keepdims=True)
    yc = y - mu
    var = jnp.mean(yc * yc, axis=-1, keepdims=True)
    o_ref[0] = yc * lax.rsqrt(var + LN_EPS) * lg_ref[...] + lb_ref[...]


def mixer_out(y, x, gate, w, ln_g, ln_b):
    b_, t_, d_ = x.shape
    rows = min(PROJ_ROWS, t_)
    vec = pl.BlockSpec((1, d_), lambda b, t: (0, 0))
    return pl.pallas_call(
        _mixer_out_kernel,
        grid=(b_, t_ // rows),
        in_specs=[pl.BlockSpec((1, rows, y.shape[2]), lambda b, t: (b, t, 0)), pl.BlockSpec((1, rows, d_), lambda b, t: (b, t, 0)),
                  _mod_spec(gate, rows), pl.BlockSpec(w.shape, lambda b, t: (0, 0)), vec, vec],
        out_specs=pl.BlockSpec((1, rows, d_), lambda b, t: (b, t, 0)),
        out_shape=jax.ShapeDtypeStruct(x.shape, f32),
        compiler_params=pltpu.CompilerParams(dimension_semantics=("arbitrary", "arbitrary"),
                                             vmem_limit_bytes=V7X_VMEM_LIMIT_BYTES),
        name="mixer_out",
    )(y, x, gate, w, ln_g.reshape(1, d_), ln_b.reshape(1, d_))


def nsa_prompt_layer(x, shift, scale, gate, w_in, w_o, pe, cw1, cb1, cw2, cb2, ln_g, ln_b):
    b_, t_, _ = x.shape
    cos, sin = _rope_tables(jnp.arange(t_), 128)
    q, rows, win, gates, kvb = nsa_project(x, shift, scale, _nsa_proj_weights(w_in), cos, sin)
    n_ch = t_ // CMP_STRIDE
    chunks = rows[..., 0:2 * KD].astype(bf16).reshape(b_, n_ch, CMP_STRIDE, 2, N_KV, HEAD_DIM)
    chunks = chunks.transpose(0, 3, 4, 1, 2, 5).reshape(b_, 2, N_KV, n_ch, CMP_STRIDE * HEAD_DIM)
    cmp_n, cmp_t = compress_kv(chunks, pe, cw1, cb1, cw2, cb2)
    o = nsa_attention(q, kvb, gates, cmp_t[:, 0].astype(bf16), cmp_n[:, 1].astype(bf16))
    x_new = mixer_out(o, x, gate, w_o.astype(bf16), ln_g, ln_b)
    return x_new, rows.reshape(b_, t_, 4, N_KV, HEAD_DIM), win.reshape(b_, t_, 2, N_KV, HEAD_DIM)


PAGES_PER_STEP = 8
S_PAD = 8
CHUNKS_PER_PAGE = PAGE_SIZE // CMP_STRIDE
WIN_KEYS_PAD = 128


def _page_specs(block, tail):
    def spec(i):
        return pl.BlockSpec(block, lambda b, pg, pt: (pt[b, pg * PAGES_PER_STEP + i],) + tail)
    return [spec(i) for i in range(PAGES_PER_STEP)]


def _cmp_pages_kernel(pt_ref, *refs):
    page_refs = refs[:PAGES_PER_STEP]
    perm_ref, w_ref, o_ref = refs[PAGES_PER_STEP:]
    nt = (((1,), (1,)), ((), ()))
    xp = [lax.dot_general(perm_ref[...], r[0, 0].reshape(2 * KD, PAGE_SIZE).astype(bf16), nt, preferred_element_type=f32)
          for r in page_refs]
    cpp = CHUNKS_PER_PAGE
    xs = [jnp.concatenate([x[s * cpp:(s + 1) * cpp] for x in xp], axis=0) for s in range(CMP_STRIDE)]
    for kv in range(2):
        groups = []
        for g in range(N_KV):
            lo = (kv * N_KV + g) * HEAD_DIM
            groups.append(jnp.concatenate([x[:, lo:lo + HEAD_DIM] for x in xs], axis=1))
        xc = jnp.concatenate(groups, axis=0).astype(bf16)
        o_ref[0, 0, kv] = jnp.dot(xc, w_ref[kv], preferred_element_type=f32)


def cmp_pages(pages, li, page_table, w01):
    b_, n_pages = page_table.shape
    n_pg = n_pages // PAGES_PER_STEP
    rows = PAGES_PER_STEP * N_KV * CHUNKS_PER_PAGE
    pos = jnp.arange(PAGE_SIZE)
    perm = (pos[None, :] == (pos[:, None] % CHUNKS_PER_PAGE) * CMP_STRIDE + pos[:, None] // CHUNKS_PER_PAGE).astype(bf16)
    out = pl.pallas_call(
        _cmp_pages_kernel,
        grid_spec=pltpu.PrefetchScalarGridSpec(
            num_scalar_prefetch=1, grid=(b_, n_pg),
            in_specs=_page_specs((1, 1, 2, N_KV, HEAD_DIM, PAGE_SIZE), (li, 0, 0, 0, 0))
            + [pl.BlockSpec(perm.shape, lambda b, pg, pt: (0, 0)), pl.BlockSpec(w01.shape, lambda b, pg, pt: (0, 0, 0))],
            out_specs=pl.BlockSpec((1, 1, 2, rows, w01.shape[2]), lambda b, pg, pt: (b, pg, 0, 0, 0))),
        out_shape=jax.ShapeDtypeStruct((b_, n_pg, 2, rows, w01.shape[2]), f32),
        compiler_params=pltpu.CompilerParams(dimension_semantics=("arbitrary", "arbitrary"),
                                             vmem_limit_bytes=V7X_VMEM_LIMIT_BYTES),
        name="cmp_pages",
    )(page_table, *([pages] * PAGES_PER_STEP), perm, w01)
    out = out.reshape(b_, n_pg, 2, N_KV, PAGES_PER_STEP, CHUNKS_PER_PAGE, w01.shape[2])
    return out.transpose(0, 2, 3, 1, 4, 5, 6).reshape(b_, 2, N_KV, n_pages * CHUNKS_PER_PAGE, w01.shape[2])


def _sample_cmp_win_kernel(q_ref, kct_ref, vc_ref, kw_ref, vw_ref, msel_ref, oc_ref, ow_ref, imp_ref, *, past, w_buf, n_wk):
    rows = Q_PER_KV * S_PAD
    qpos = past + (lax.broadcasted_iota(jnp.int32, (rows, 1), 0) & (S_PAD - 1))
    cend = lax.broadcasted_iota(jnp.int32, (1, kct_ref.shape[3]), 1) * CMP_STRIDE + (CMP_BLK - 1)
    cvalid = cend <= qpos
    widx = lax.broadcasted_iota(jnp.int32, (1, kw_ref.shape[3]), 1)
    kwpos = past - w_buf + widx
    dpos = qpos - kwpos
    wok = (dpos >= 0) & (dpos < WINDOW) & (kwpos >= 0) & (widx < n_wk)
    for g in range(N_KV):
        q = q_ref[0, g]
        s = jnp.dot(q, kct_ref[0, g], preferred_element_type=f32)
        p = _softmax_rows(jnp.where(cvalid, s, NEG), cvalid)
        oc_ref[0, g] = jnp.dot(p.astype(bf16), vc_ref[0, g], preferred_element_type=f32)
        hi = p.astype(bf16)
        lo = (p - hi.astype(f32)).astype(bf16)
        imp = jnp.dot(hi, msel_ref[...], preferred_element_type=f32) + jnp.dot(lo, msel_ref[...], preferred_element_type=f32)
        tot = imp[0:S_PAD]
        for r in range(1, Q_PER_KV):
            tot = tot + imp[r * S_PAD:(r + 1) * S_PAD]
        imp_ref[0, g] = tot
        sw = jnp.where(wok, jnp.dot(q, kw_ref[0, g], preferred_element_type=f32), NEG)
        ew = jnp.exp(sw - _lanes(jnp.max(sw, axis=1, keepdims=True), sw.shape[1])).astype(bf16)
        acc = jnp.dot(ew, vw_ref[0, g], preferred_element_type=f32)
        ow_ref[0, g] = acc[:, 0:HEAD_DIM] * (1.0 / jnp.maximum(acc[:, HEAD_DIM:HEAD_DIM + 1], 1e-30))


def _topk_kernel(imp_ref, o_ref, *, past, n_sel):
    qpos = past + (lax.broadcasted_iota(jnp.int32, imp_ref.shape, 1) & (S_PAD - 1))
    o_ref[...] = _unselected_blocks(imp_ref[...], qpos, n_sel)


def _sample_sel_kernel(pt_ref, *refs, past, n_new):
    page_refs = refs[:PAGES_PER_STEP]
    q_ref, nonsel_ref, knew_ref, vnew_ref, os_ref, m_sc, l_sc, acc_sc = refs[PAGES_PER_STEP:]
    pg = pl.program_id(1)
    nt = (((1,), (1,)), ((), ()))
    rows = Q_PER_KV * S_PAD

    @pl.when(pg == 0)
    def _():
        m_sc[...] = jnp.full_like(m_sc, NEG)
        l_sc[...] = jnp.zeros_like(l_sc)
        acc_sc[...] = jnp.zeros_like(acc_sc)

    def accumulate(s, pv):
        m_old = m_sc[...]
        m_new = jnp.maximum(m_old, jnp.max(s, axis=1, keepdims=True))
        p = jnp.exp(s - m_new)
        alpha = jnp.exp(m_old - m_new)
        l_sc[...] = alpha * l_sc[...] + jnp.sum(p, axis=1, keepdims=True)
        pb = p.astype(bf16)
        upd = jnp.concatenate([pv(g, pb[g * rows:(g + 1) * rows]) for g in range(N_KV)], axis=0)
        acc_sc[...] = alpha * acc_sc[...] + upd
        m_sc[...] = m_new

    nk = PAGES_PER_STEP * PAGE_SIZE
    kt = [jnp.concatenate([r[0, 0, 0, g] for r in page_refs], axis=1).astype(bf16) for g in range(N_KV)]
    vt = [jnp.concatenate([r[0, 0, 1, g] for r in page_refs], axis=1).astype(bf16) for g in range(N_KV)]
    nonsel = nonsel_ref[0]
    s = jnp.concatenate([jnp.dot(q_ref[0, g], kt[g], preferred_element_type=f32) for g in range(N_KV)], axis=0)
    blk_j = lax.broadcasted_iota(jnp.int32, (nonsel.shape[1], nk), 0)
    key = pg * nk + lax.broadcasted_iota(jnp.int32, (nonsel.shape[1], nk), 1)
    expand = jnp.where(blk_j == key // SEL_BLK, NEG, 0.0).astype(bf16)
    accumulate(s + jnp.dot(nonsel, expand, preferred_element_type=f32),
               lambda g, p: lax.dot_general(p, vt[g], nt, preferred_element_type=f32))

    @pl.when(pg == pl.num_programs(1) - 1)
    def _():
        sn = jnp.concatenate([jnp.dot(q_ref[0, g], knew_ref[0, g], preferred_element_type=f32) for g in range(N_KV)], axis=0)
        t_new = lax.broadcasted_iota(jnp.int32, (1, sn.shape[1]), 1)
        s_row = lax.broadcasted_iota(jnp.int32, (sn.shape[0], 1), 0) & (S_PAD - 1)
        new_blk = past // SEL_BLK
        bias_new = nonsel[:, new_blk:new_blk + 1].astype(f32) * NEG
        accumulate(jnp.where((t_new < n_new) & (t_new <= s_row), sn + bias_new, NEG),
                   lambda g, p: jnp.dot(p, vnew_ref[0, g], preferred_element_type=f32))
        o = acc_sc[...] * (1.0 / jnp.maximum(l_sc[...], 1e-30))
        for g in range(N_KV):
            os_ref[0, g] = o[g * rows:(g + 1) * rows]


def _gated_out_kernel(oc_ref, os_ref, ow_ref, gc_ref, gs_ref, gw_ref, x_ref, gt_ref, w_ref, lg_ref, lb_ref, o_ref):
    o = gc_ref[0] * oc_ref[0] + gs_ref[0] * os_ref[0] + gw_ref[0] * ow_ref[0]
    f = jnp.dot(o.astype(bf16), w_ref[...], preferred_element_type=f32)
    y = ALPHA * x_ref[0] + (1.0 + gt_ref[0]) * f
    mu = jnp.mean(y, axis=-1, keepdims=True)
    yc = y - mu
    var = jnp.mean(yc * yc, axis=-1, keepdims=True)
    o_ref[0] = yc * lax.rsqrt(var + LN_EPS) * lg_ref[...] + lb_ref[...]


def nsa_sample_layer(x, shift, scale, gate, cache_kv, page_table, li, win_buf, w_in, w_o, pe, cw1, cb1, cw2, cb2, ln_g, ln_b):
    b_, s_, d_ = x.shape
    n_pool, n_pages = cache_kv.shape[0], page_table.shape[1]
    past = n_pages * PAGE_SIZE
    w_buf = win_buf.shape[1]
    n_ch = past // CMP_STRIDE
    n_sel = past // SEL_BLK + 1
    assert past % SEL_BLK == 0 and s_ <= S_PAD and n_pages % PAGES_PER_STEP == 0
    assert (past + s_ - CMP_BLK) // CMP_STRIDE + 2 == n_ch
    rows_n = b_ * s_
    rq = Q_PER_KV * S_PAD
    flat = lambda m: jnp.broadcast_to(m, (b_, s_, d_)).reshape(1, rows_n, d_)
    xf = x.reshape(1, rows_n, d_)
    cos, sin = _rope_tables(past + (jnp.arange(rows_n) % s_), 128)
    q, rows, win, gates, kvb = nsa_project(xf, flat(shift), flat(scale), _nsa_proj_weights(w_in), cos, sin)

    pages = cache_kv.transpose(0, 1, 3, 4, 5, 2)
    half = cw1.shape[1] // 2
    w01 = jnp.concatenate([cw1[:, :half], cw1[:, half:]], axis=2).astype(bf16)
    cmp_n, cmp_t = compress_kv(cmp_pages(pages, li, page_table, w01), pe, cw1, cb1, cw2, cb2, projected=True)

    q32 = jnp.pad(q.reshape(b_, s_, N_KV, Q_PER_KV, HEAD_DIM), ((0, 0), (0, S_PAD - s_), (0, 0), (0, 0), (0, 0)))
    q32 = q32.transpose(0, 2, 3, 1, 4).reshape(b_, N_KV, rq, HEAD_DIM)

    w_all = jnp.concatenate([win_buf, win.reshape(b_, s_, 2, N_KV, HEAD_DIM)], axis=1)
    n_wk = w_buf + s_
    wk_pad = -(-n_wk // WIN_KEYS_PAD) * WIN_KEYS_PAD
    w_pad = jnp.pad(w_all, ((0, 0), (0, wk_pad - n_wk), (0, 0), (0, 0), (0, 0))).astype(bf16)
    kw = w_pad[:, :, 0].transpose(0, 2, 3, 1)
    vw = _with_ones(w_pad[:, :, 1].transpose(0, 2, 1, 3))
    n_sel_pad = 2 * 128
    msel = jnp.pad(_cmp_to_sel_matrix(n_ch, n_sel), ((0, 0), (0, n_sel_pad - n_sel))).astype(bf16)
    per_b = lambda a: pl.BlockSpec((1,) + a.shape[1:], lambda b: (b,) + (0,) * (a.ndim - 1))
    kct, vc = cmp_t[:, 0].astype(bf16), cmp_n[:, 1].astype(bf16)
    o_shape = jax.ShapeDtypeStruct((b_, N_KV, rq, HEAD_DIM), f32)
    o_spec = pl.BlockSpec((1, N_KV, rq, HEAD_DIM), lambda b: (b, 0, 0, 0))
    oc, ow, imp = pl.pallas_call(
        functools.partial(_sample_cmp_win_kernel, past=past, w_buf=w_buf, n_wk=n_wk),
        grid=(b_,),
        in_specs=[per_b(q32), per_b(kct), per_b(vc), per_b(kw), per_b(vw), pl.BlockSpec(msel.shape, lambda b: (0, 0))],
        out_specs=[o_spec, o_spec, pl.BlockSpec((1, N_KV, S_PAD, n_sel_pad), lambda b: (b, 0, 0, 0))],
        out_shape=[o_shape, o_shape, jax.ShapeDtypeStruct((b_, N_KV, S_PAD, n_sel_pad), f32)],
        compiler_params=pltpu.CompilerParams(dimension_semantics=("arbitrary",)),
        name="sample_cmp_win",
    )(q32, kct, vc, kw, vw, msel)

    sel_rows = -(-n_sel // 8) * 8
    imp_t = imp[..., :sel_rows].transpose(3, 0, 1, 2).reshape(sel_rows, b_ * N_KV * S_PAD)
    nonsel_t = pl.pallas_call(
        functools.partial(_topk_kernel, past=past, n_sel=n_sel),
        out_shape=jax.ShapeDtypeStruct(imp_t.shape, f32),
        name="sample_topk",
    )(imp_t)
    nonsel = nonsel_t.reshape(sel_rows, b_, N_KV, 1, S_PAD).transpose(1, 2, 3, 4, 0)
    nonsel = jnp.broadcast_to(nonsel, (b_, N_KV, Q_PER_KV, S_PAD, sel_rows)).reshape(b_, N_KV * rq, sel_rows)
    nonsel = jnp.pad(nonsel, ((0, 0), (0, 0), (0, n_sel_pad - sel_rows)), constant_values=1.0).astype(bf16)

    new_pad = lambda a: jnp.pad(a.reshape(b_, s_, N_KV, HEAD_DIM), ((0, 0), (0, 128 - s_), (0, 0), (0, 0)))
    knew = new_pad(kvb[..., 0:KD]).transpose(0, 2, 3, 1)
    vnew = new_pad(kvb[..., KD:2 * KD]).transpose(0, 2, 1, 3)
    rows_all = N_KV * rq
    per_b2 = lambda a: pl.BlockSpec((1,) + a.shape[1:], lambda b, pg, pt: (b,) + (0,) * (a.ndim - 1))
    o_s = pl.pallas_call(
        functools.partial(_sample_sel_kernel, past=past, n_new=s_),
        grid_spec=pltpu.PrefetchScalarGridSpec(
            num_scalar_prefetch=1, grid=(b_, n_pages // PAGES_PER_STEP),
            in_specs=_page_specs((1, 1, 2, N_KV, HEAD_DIM, PAGE_SIZE), (li, 1, 0, 0, 0))
            + [per_b2(q32), per_b2(nonsel), per_b2(knew), per_b2(vnew)],
            out_specs=pl.BlockSpec((1, N_KV, rq, HEAD_DIM), lambda b, pg, pt: (b, 0, 0, 0)),
            scratch_shapes=[pltpu.VMEM((rows_all, 1), f32), pltpu.VMEM((rows_all, 1), f32), pltpu.VMEM((rows_all, HEAD_DIM), f32)]),
        out_shape=o_shape,
        compiler_params=pltpu.CompilerParams(dimension_semantics=("arbitrary", "arbitrary"),
                                             vmem_limit_bytes=V7X_VMEM_LIMIT_BYTES),
        name="sample_sel",
    )(page_table, *([pages] * PAGES_PER_STEP), q32, nonsel, knew, vnew)

    tok = lambda o: o.reshape(b_, N_KV, Q_PER_KV, S_PAD, HEAD_DIM)[:, :, :, :s_].transpose(0, 3, 1, 2, 4).reshape(1, rows_n, N_HEADS * HEAD_DIM)
    gexp = lambda br: jnp.repeat(gates[..., br:N_GATE:3], HEAD_DIM, axis=-1)
    full = lambda n: pl.BlockSpec((1, rows_n, n), lambda i: (0, 0, 0))
    vec = pl.BlockSpec((1, d_), lambda i: (0, 0))
    x_new = pl.pallas_call(
        _gated_out_kernel,
        grid=(1,),
        in_specs=[full(N_HEADS * HEAD_DIM)] * 6 + [full(d_), full(d_), pl.BlockSpec(w_o.shape, lambda i: (0, 0)), vec, vec],
        out_specs=full(d_),
        out_shape=jax.ShapeDtypeStruct((1, rows_n, d_), f32),
        name="sample_gated_out",
    )(tok(oc), tok(o_s), tok(ow), gexp(0), gexp(1), gexp(2), xf, flat(gate), w_o.astype(bf16), ln_g.reshape(1, d_), ln_b.reshape(1, d_))
    return x_new.reshape(b_, s_, d_), rows.reshape(b_, s_, 4, N_KV, HEAD_DIM), w_all[:, n_wk - w_buf:]


HPG = SSM_HEADS // SSM_GROUPS
GN = SSM_GROUPS * D_STATE
GW = D_INNER // SSM_GROUPS
DT_PAD = 128
CONV_KEEP = 8


def _ssd_proj_kernel(x_ref, sh_ref, sc_ref, w_ref, z_ref, xbc_ref, dt_ref):
    hb = (x_ref[0] * (1.0 + sc_ref[0]) + sh_ref[0]).astype(bf16)
    z_ref[0] = jnp.dot(hb, w_ref[:, 0:D_INNER], preferred_element_type=f32)
    xbc_ref[0] = jnp.dot(hb, w_ref[:, D_INNER:D_INNER + CONV_DIM], preferred_element_type=f32)
    dt_ref[0] = jnp.dot(hb, w_ref[:, D_INNER + CONV_DIM:], preferred_element_type=f32)


def ssd_project(x, shift, scale, w_in):
    b_, t_, d_ = x.shape
    rows = min(PROJ_ROWS, t_)
    w_all = jnp.pad(w_in, ((0, 0), (0, DT_PAD - SSM_HEADS))).astype(bf16)
    tile = lambda n: pl.BlockSpec((1, rows, n), lambda b, t: (b, t, 0))
    mod = _mod_spec(shift, rows)
    return pl.pallas_call(
        _ssd_proj_kernel,
        grid=(b_, t_ // rows),
        in_specs=[tile(d_), mod, mod, pl.BlockSpec(w_all.shape, lambda b, t: (0, 0), pipeline_mode=pl.Buffered(1))],
        out_specs=[tile(D_INNER), tile(CONV_DIM), tile(DT_PAD)],
        out_shape=[jax.ShapeDtypeStruct((b_, t_, D_INNER), f32), jax.ShapeDtypeStruct((b_, t_, CONV_DIM), f32),
                   jax.ShapeDtypeStruct((b_, t_, DT_PAD), f32)],
        compiler_params=pltpu.CompilerParams(dimension_semantics=("arbitrary", "arbitrary"),
                                             vmem_limit_bytes=V7X_VMEM_LIMIT_BYTES),
        name="ssd_project",
    )(x, shift, scale, w_all)


def _split3(v):
    p1 = v.astype(bf16)
    r1 = v - p1.astype(f32)
    p2 = r1.astype(bf16)
    p3 = (r1 - p2.astype(f32)).astype(bf16)
    return p1, p2, p3


def _dot3(parts, m, left):
    out = None
    for p in parts:
        t = jnp.dot(m, p, preferred_element_type=f32) if left else jnp.dot(p, m, preferred_element_type=f32)
        out = t if out is None else out + t
    return out


def _ssd_scan_kernel(xbc_ref, dt_ref, z_ref, cst_ref, h0_ref, cw_ref, cb_ref, dtb_ref, a_ref, d_ref, ng_ref, ex_ref,
                     y_ref, ht_ref, win_sc, st_sc, *, n_valid):
    c = pl.program_id(1)
    L = SSM_CHUNK

    @pl.when(c == 0)
    def _():
        win_sc[0:CONV_KEEP, :] = cst_ref[0]
        st_sc[...] = h0_ref[0]

    win_sc[CONV_KEEP:CONV_KEEP + L, :] = xbc_ref[0]
    acc = cb_ref[...] + win_sc[pl.ds(CONV_KEEP, L), :] * cw_ref[CONV_W - 1:CONV_W, :]
    for k in range(CONV_W - 1):
        acc = acc + win_sc[pl.ds(CONV_KEEP - (CONV_W - 1) + k, L), :] * cw_ref[k:k + 1, :]
    win_sc[0:CONV_KEEP, :] = win_sc[L:L + CONV_KEEP, :]
    xbc = acc * jax.nn.sigmoid(acc)
    xs = xbc[:, 0:D_INNER]

    t_row = lax.broadcasted_iota(jnp.int32, (L, 1), 0)
    dt_in = dt_ref[0] + dtb_ref[...]
    dt = jnp.maximum(dt_in, 0.0) + jnp.log1p(jnp.exp(-jnp.abs(dt_in)))
    dt = jnp.where(t_row < n_valid, dt, 0.0)
    a = dt * a_ref[...]
    ii = lax.broadcasted_iota(jnp.int32, (L, L), 0)
    jj = lax.broadcasted_iota(jnp.int32, (L, L), 1)
    lower = ii >= jj
    tri = jnp.where(lower, 1.0, 0.0).astype(bf16)
    a_cs = _dot3(_split3(a), tri, left=True)
    a_cs_t = a_cs.T
    ex = ex_ref[...]
    dt_e = _dot3(_split3(dt), ex, left=False)
    acs_e = _dot3(_split3(a_cs), ex, left=False)
    a_tot_e = acs_e[L - 1:L, :]
    xd = xs * dt_e
    xdd = (xd * jnp.exp(a_tot_e - acs_e)).astype(bf16)
    xdb = xd.astype(bf16)
    grow = jnp.exp(acs_e)
    lane_lo = lax.broadcasted_iota(jnp.int32, (L, 2 * SSM_HEAD_DIM), 1) < SSM_HEAD_DIM
    nt = (((1,), (1,)), ((), ()))

    for g in range(SSM_GROUPS):
        gl = slice(g * GW, (g + 1) * GW)
        bm = xbc[:, D_INNER + g * D_STATE:D_INNER + (g + 1) * D_STATE]
        cm = xbc[:, D_INNER + GN + g * D_STATE:D_INNER + GN + (g + 1) * D_STATE].astype(bf16)
        cb = lax.dot_general(cm, bm.astype(bf16), nt, preferred_element_type=f32)
        st_old = st_sc[:, gl]
        y_g = jnp.dot(cm, st_old.astype(bf16), preferred_element_type=f32) * grow[:, gl]
        st_sc[:, gl] = jnp.exp(a_tot_e[:, gl]) * st_old + jnp.dot(bm.T.astype(bf16), xdd[:, gl], preferred_element_type=f32)
        pairs = []
        for k in range(HPG // 2):
            h0 = g * HPG + 2 * k
            ms = []
            for h in (h0, h0 + 1):
                seg = a_cs[:, h:h + 1] - a_cs_t[h:h + 1, :]
                ms.append((cb * jnp.where(lower, jnp.exp(seg), 0.0)).astype(bf16))
            xp = xdb[:, h0 * SSM_HEAD_DIM:(h0 + 2) * SSM_HEAD_DIM]
            pairs.append(jnp.where(lane_lo, jnp.dot(ms[0], xp, preferred_element_type=f32),
                                   jnp.dot(ms[1], xp, preferred_element_type=f32)))
        y_g = y_g + jnp.concatenate(pairs, axis=1) + d_ref[:, gl] * xs[:, gl]
        zg = z_ref[0, :, gl]
        y_g = y_g * (zg * jax.nn.sigmoid(zg))
        y_g = y_g * lax.rsqrt(jnp.mean(y_g * y_g, axis=-1, keepdims=True) + LN_EPS)
        y_ref[0, :, gl] = (y_g * ng_ref[:, gl]).astype(bf16)

    @pl.when(c == pl.num_programs(1) - 1)
    def _():
        ht_ref[0] = st_sc[...]


def ssd_scan(xbc, dt_raw, z, conv_state, h0, conv_w, conv_b, dt_bias, a_log, d_skip, norm_g, n_valid):
    b_, t_, _ = xbc.shape
    L = SSM_CHUNK
    cst = jnp.pad(conv_state, ((0, 0), (CONV_KEEP - (CONV_W - 1), 0), (0, 0)))
    h0t = h0.transpose(0, 3, 1, 2).reshape(b_, D_STATE, D_INNER)
    vec = lambda v: jnp.pad(v, (0, DT_PAD - SSM_HEADS)).reshape(1, DT_PAD)
    ex = (jnp.arange(DT_PAD)[:, None] == (jnp.arange(D_INNER) // SSM_HEAD_DIM)[None, :]).astype(bf16)
    tile = lambda n: pl.BlockSpec((1, L, n), lambda b, c: (b, c, 0))
    per_b = lambda r, n: pl.BlockSpec((1, r, n), lambda b, c: (b, 0, 0))
    const = lambda r, n: pl.BlockSpec((r, n), lambda b, c: (0, 0))
    y, ht = pl.pallas_call(
        functools.partial(_ssd_scan_kernel, n_valid=n_valid),
        grid=(b_, t_ // L),
        in_specs=[tile(CONV_DIM), tile(DT_PAD), tile(D_INNER), per_b(CONV_KEEP, CONV_DIM), per_b(D_STATE, D_INNER),
                  const(CONV_W, CONV_DIM), const(1, CONV_DIM), const(1, DT_PAD), const(1, DT_PAD), const(1, D_INNER),
                  const(1, D_INNER), const(DT_PAD, D_INNER)],
        out_specs=[tile(D_INNER), per_b(D_STATE, D_INNER)],
        out_shape=[jax.ShapeDtypeStruct((b_, t_, D_INNER), bf16), jax.ShapeDtypeStruct((b_, D_STATE, D_INNER), f32)],
        scratch_shapes=[pltpu.VMEM((L + CONV_KEEP, CONV_DIM), f32), pltpu.VMEM((D_STATE, D_INNER), f32)],
        compiler_params=pltpu.CompilerParams(dimension_semantics=("arbitrary", "arbitrary"),
                                             vmem_limit_bytes=V7X_VMEM_LIMIT_BYTES),
        name="ssd_scan",
    )(xbc, dt_raw, z, cst, h0t, conv_w, conv_b.reshape(1, CONV_DIM), vec(dt_bias), vec(-jnp.exp(a_log)),
      jnp.repeat(d_skip, SSM_HEAD_DIM).reshape(1, D_INNER), norm_g.reshape(1, D_INNER), ex)
    return y, ht.reshape(b_, D_STATE, SSM_HEADS, SSM_HEAD_DIM).transpose(0, 2, 3, 1)


def ssd_layer(x, shift, scale, gate, conv_state, ssm_state, w_in, conv_w, conv_b, dt_bias, a_log, d_skip, norm_g, w_out,
              ln_g, ln_b):
    b_, t_, d_ = x.shape
    fold = t_ < SSM_CHUNK
    if fold:
        flat = lambda m: jnp.broadcast_to(m, (b_, t_, d_)).reshape(1, b_ * t_, d_)
        xf, shift, scale, gate = x.reshape(1, b_ * t_, d_), flat(shift), flat(scale), flat(gate)
    else:
        xf = x
    z, xbc, dt_raw = ssd_project(xf, shift, scale, w_in)
    if fold:
        padt = lambda v: jnp.pad(v.reshape(b_, t_, -1), ((0, 0), (0, SSM_CHUNK - t_), (0, 0)))
        z, xbc, dt_raw = padt(z), padt(xbc), padt(dt_raw)
    y, new_ssm = ssd_scan(xbc, dt_raw, z, conv_state, ssm_state, conv_w, conv_b, dt_bias, a_log, d_skip, norm_g, t_)
    new_conv = jnp.concatenate([conv_state, xbc[:, :t_]], axis=1)[:, t_:]
    if fold:
        y = y[:, :t_].reshape(1, b_ * t_, D_INNER)
    x_new = mixer_out(y, xf, gate, w_out.astype(bf16), ln_g, ln_b).reshape(b_, t_, d_)
    return x_new, new_conv, new_ssm


ADA_COLS = 1152


def _adaln_kernel(c_ref, w_ref, b_ref, o_ref):
    c = c_ref[...]
    h = (c * jax.nn.sigmoid(c)).astype(bf16)
    o_ref[...] = jnp.dot(h, w_ref[...].astype(bf16), preferred_element_type=f32) + b_ref[...]


def adaln(c, w, b):
    r_, d_ = c.shape
    n_ = w.shape[1]
    return pl.pallas_call(
        _adaln_kernel,
        grid=(n_ // ADA_COLS,),
        in_specs=[pl.BlockSpec((r_, d_), lambda n: (0, 0)), pl.BlockSpec((d_, ADA_COLS), lambda n: (0, n)),
                  pl.BlockSpec((1, ADA_COLS), lambda n: (0, n))],
        out_specs=pl.BlockSpec((r_, ADA_COLS), lambda n: (0, n)),
        out_shape=jax.ShapeDtypeStruct((r_, n_), f32),
        compiler_params=pltpu.CompilerParams(dimension_semantics=("arbitrary",)),
        name="adaln",
    )(c, w, b.reshape(1, n_))


def _trunk_layer(x, m, i, mixer, ln_g, ln_b, ffn_a, ffn_b):
    b_, t_, d_ = x.shape
    m = m.reshape(b_, 3, 3, 1, D_MODEL)

    def ffn(v, s, w):
        mods = [m[:, s, k] for k in range(3)]
        if t_ < 8:
            mods = [jnp.broadcast_to(z, (b_, t_, d_)).reshape(1, b_ * t_, d_) for z in mods]
            return ffn_sublayer(v.reshape(1, b_ * t_, d_), *mods, *w, ln_g[i, s], ln_b[i, s]).reshape(b_, t_, d_)
        return ffn_sublayer(v, *mods, *w, ln_g[i, s], ln_b[i, s])

    x = ffn(x, 0, ffn_a)
    x, st = mixer(x, m[:, 1, 0], m[:, 1, 1], m[:, 1, 2], ln_g[i, 1], ln_b[i, 1])
    x = ffn(x, 2, ffn_b)
    return x, st


def kernel(x_prompt, x_sample, cache_kv, cache_win, state_conv, state_ssm, page_table, c_prompt, c_sample, ada_w, ada_b, ln_g, ln_b, ffn_w1, ffn_w3, ffn_w2, nsa_w_in, nsa_w_o, nsa_cmp_pe, nsa_cmp_w1, nsa_cmp_b1, nsa_cmp_w2, nsa_cmp_b2, ssm_w_in, ssm_conv_w, ssm_conv_b, ssm_dt_bias, ssm_a_log, ssm_d, ssm_norm_g, ssm_w_out):
    xp, xs = x_prompt, x_sample
    nb_p, nb_s = c_prompt.shape[0], c_sample.shape[0]
    c_all = jnp.concatenate([c_prompt, c_sample], axis=0)
    c_all = jnp.pad(c_all, ((0, -(nb_p + nb_s) % 8), (0, 0)))
    kv_p, kv_s, win_p, win_s, conv_p, conv_s, ssm_p, ssm_s = [], [], [], [], [], [], [], []
    for i in range(DEPTH):
        j = i // N_MIXERS
        ffn_a = _chunk_ffn_weights(ffn_w1[i, 0], ffn_w3[i, 0], ffn_w2[i, 0])
        ffn_b = _chunk_ffn_weights(ffn_w1[i, 1], ffn_w3[i, 1], ffn_w2[i, 1])
        m_all = adaln(c_all, ada_w[i], ada_b[i])
        m_p, m_s = m_all[:nb_p], m_all[nb_p:nb_p + nb_s]
        lw = (ln_g, ln_b, ffn_a, ffn_b)
        if i % N_MIXERS == 0:
            nw = (nsa_w_in[j], nsa_w_o[j], nsa_cmp_pe[j], nsa_cmp_w1[j], nsa_cmp_b1[j], nsa_cmp_w2[j], nsa_cmp_b2[j])
            def prompt_mixer(x, sh, sc, gt, lg, lb):
                x_new, rows, win = nsa_prompt_layer(x, sh, sc, gt, *nw, lg, lb)
                return x_new, (rows, win[:, x.shape[1] - min(WINDOW, x.shape[1]):])

            xp, (r_p, w_p) = _trunk_layer(xp, m_p, i, prompt_mixer, *lw)
            def sample_mixer(x, sh, sc, gt, lg, lb):
                x_new, rows, win = nsa_sample_layer(x, sh, sc, gt, cache_kv, page_table, j, cache_win[:, j], *nw, lg, lb)
                return x_new, (rows, win)

            xs, (r_s, w_s) = _trunk_layer(xs, m_s, i, sample_mixer, *lw)
            kv_p.append(r_p); kv_s.append(r_s); win_p.append(w_p); win_s.append(w_s)
        else:
            sw = (ssm_w_in[j], ssm_conv_w[j], ssm_conv_b[j], ssm_dt_bias[j], ssm_a_log[j], ssm_d[j], ssm_norm_g[j], ssm_w_out[j])
            zc = jnp.zeros((xp.shape[0], CONV_W - 1, CONV_DIM), f32)
            zs = jnp.zeros((xp.shape[0], SSM_HEADS, SSM_HEAD_DIM, D_STATE), f32)
            def ssd_mixer(conv0, ssm0):
                def run(x, sh, sc, gt, lg, lb):
                    x_new, cv, st = ssd_layer(x, sh, sc, gt, conv0, ssm0, *sw, lg, lb)
                    return x_new, (cv, st)
                return run

            xp, (cv_p, st_p) = _trunk_layer(xp, m_p, i, ssd_mixer(zc, zs), *lw)
            xs, (cv_s, st_s) = _trunk_layer(xs, m_s, i, ssd_mixer(state_conv[:, j], state_ssm[:, j]), *lw)
            conv_p.append(cv_p); conv_s.append(cv_s); ssm_p.append(st_p); ssm_s.append(st_s)
    return (xp, xs, jnp.stack(kv_p, 1), jnp.stack(kv_s, 1), jnp.stack(win_p, 1), jnp.stack(win_s, 1),
            jnp.stack(conv_p, 1), jnp.stack(conv_s, 1), jnp.stack(ssm_p, 1), jnp.stack(ssm_s, 1))
```

```python
import functools
import math

import jax
import jax.numpy as jnp
from jax import lax
from jax.experimental import pallas as pl
from jax.experimental.pallas import tpu as pltpu

f32 = jnp.float32
bf16 = jnp.bfloat16

D_MODEL = 1024
DEPTH = 4
PAGE_SIZE = 128
N_HEADS = 16
HEAD_DIM = 64
N_KV = 4
Q_PER_KV = N_HEADS // N_KV
CMP_BLK = 32
CMP_STRIDE = 16
CMP_HIDDEN = 2 * HEAD_DIM
SEL_BLK = 64
TOPK = 16
WINDOW = 512
Q_BLOCK = 256
ROPE_THETA = 10000.0
D_INNER = 2 * D_MODEL
SSM_HEAD_DIM = 64
SSM_HEADS = D_INNER // SSM_HEAD_DIM
SSM_GROUPS = 4
D_STATE = 128
CONV_W = 4
CONV_DIM = D_INNER + 2 * SSM_GROUPS * D_STATE
SSM_CHUNK = 128
D_FF = 256 * ((8 * D_MODEL // 3 + 255) // 256)
N_MIXERS = 2
ALPHA = (2 * DEPTH) ** 0.25
N_ADA = 9
LN_EPS = 1e-5

V7X_VMEM_LIMIT_BYTES = 56 * 1024 * 1024
FF_CHUNK = 256
FFN_ROWS = 512


def _ffn_kernel(x_ref, sh_ref, sc_ref, gt_ref, w1_ref, w3_ref, w2_ref, lg_ref, lb_ref, o_ref, acc_ref):
    x = x_ref[0]
    hb = (x * (1.0 + sc_ref[0]) + sh_ref[0]).astype(bf16)
    acc_ref[...] = jnp.zeros_like(acc_ref)

    def chunk(c, carry):
        a = jnp.dot(hb, w1_ref[c], preferred_element_type=f32)
        b = jnp.dot(hb, w3_ref[c], preferred_element_type=f32)
        g = (a * jax.nn.sigmoid(a) * b).astype(bf16)
        acc_ref[...] += jnp.dot(g, w2_ref[c], preferred_element_type=f32)
        return carry

    for c in range(w1_ref.shape[0]):
        chunk(c, 0)
    y = ALPHA * x + (1.0 + gt_ref[0]) * (0.5 * acc_ref[...])
    mu = jnp.mean(y, axis=-1, keepdims=True)
    yc = y - mu
    var = jnp.mean(yc * yc, axis=-1, keepdims=True)
    o_ref[0] = yc * lax.rsqrt(var + LN_EPS) * lg_ref[...] + lb_ref[...]


def _mod_spec(m, rows):
    if m.shape[1] == 1:
        return pl.BlockSpec((1, 1, m.shape[2]), lambda b, t: (b, 0, 0))
    return pl.BlockSpec((1, rows, m.shape[2]), lambda b, t: (b, t, 0))


def ffn_sublayer(x, shift, scale, gate, w1c, w3c, w2c, ln_g, ln_b):
    b_, t_, d_ = x.shape
    rows = min(FFN_ROWS, t_)
    mod_spec = _mod_spec(shift, rows)
    const3 = lambda b, t: (0, 0, 0)
    wspec = lambda w: pl.BlockSpec(w.shape, const3, pipeline_mode=pl.Buffered(1))
    vec = pl.BlockSpec((1, d_), lambda b, t: (0, 0))
    return pl.pallas_call(
        _ffn_kernel,
        grid=(b_, t_ // rows),
        in_specs=[pl.BlockSpec((1, rows, d_), lambda b, t: (b, t, 0)), mod_spec, mod_spec, mod_spec,
                  wspec(w1c), wspec(w3c), wspec(w2c), vec, vec],
        out_specs=pl.BlockSpec((1, rows, d_), lambda b, t: (b, t, 0)),
        out_shape=jax.ShapeDtypeStruct(x.shape, f32),
        scratch_shapes=[pltpu.VMEM((rows, d_), f32)],
        compiler_params=pltpu.CompilerParams(dimension_semantics=("arbitrary", "arbitrary"),
                                             vmem_limit_bytes=V7X_VMEM_LIMIT_BYTES),
        name="ffn_sublayer",
    )(x, shift, scale, gate, w1c, w3c, w2c, ln_g.reshape(1, d_), ln_b.reshape(1, d_))


def _chunk_ffn_weights(w1, w3, w2):
    n = D_FF // FF_CHUNK
    w1c = w1.astype(bf16).reshape(D_MODEL, n, FF_CHUNK).transpose(1, 0, 2)
    w3c = w3.astype(bf16).reshape(D_MODEL, n, FF_CHUNK).transpose(1, 0, 2)
    w2c = w2.astype(bf16).reshape(n, FF_CHUNK, D_MODEL)
    return w1c, w3c, w2c


NEG = -1e30
PROJ_ROWS = 512
SEL_TK = 512
KD = N_KV * HEAD_DIM
_C_Q, _C_QR, _C_CMP, _C_SEL, _C_WIN, _C_G, _C_END = 0, 1024, 2048, 2560, 3328, 4096, 4224
N_GATE = 3 * N_HEADS


def _rot_cols(w):
    w4 = w.reshape(w.shape[0], -1, 2, HEAD_DIM // 2)
    return jnp.stack([-w4[:, :, 1], w4[:, :, 0]], axis=2).reshape(w.shape)


def _nsa_proj_weights(w_in):
    qd = N_HEADS * HEAD_DIM
    wq = w_in[:, :qd] * HEAD_DIM ** -0.5
    kc, vc, ks, vs, kw, vw = [w_in[:, qd + i * KD: qd + (i + 1) * KD] for i in range(6)]
    wg = jnp.pad(w_in[:, qd + 6 * KD:], ((0, 0), (0, 128 - N_GATE)))
    return jnp.concatenate([wq, _rot_cols(wq), kc, vc, ks, _rot_cols(ks), vs, kw, _rot_cols(kw), vw, wg], axis=1).astype(bf16)


def _rope_tables(pos, width):
    half = HEAD_DIM // 2
    inv = ROPE_THETA ** (-jnp.arange(half, dtype=f32) / half)
    ang = jnp.tile(pos.astype(f32)[:, None] * inv, (1, width // half))
    return jnp.cos(ang), jnp.sin(ang)


def _nsa_proj_kernel(x_ref, sh_ref, sc_ref, w_ref, cos_ref, sin_ref, q_ref, rows_ref, win_ref, gate_ref, kvb_ref):
    hb = (x_ref[0] * (1.0 + sc_ref[0]) + sh_ref[0]).astype(bf16)
    cos, sin = cos_ref[...], sin_ref[...]

    def mm(lo, hi):
        return jnp.dot(hb, w_ref[:, lo:hi], preferred_element_type=f32)

    def rope(a, b, c):
        return a[:, c * 128:(c + 1) * 128] * cos + b[:, c * 128:(c + 1) * 128] * sin

    qa, qb = mm(_C_Q, _C_QR), mm(_C_QR, _C_CMP)
    for c in range(N_HEADS * HEAD_DIM // 128):
        q_ref[0, :, c * 128:(c + 1) * 128] = rope(qa, qb, c).astype(bf16)
    rows_ref[0, :, 0:2 * KD] = mm(_C_CMP, _C_SEL)
    sel = mm(_C_SEL, _C_WIN)
    win = mm(_C_WIN, _C_G)
    for c in range(KD // 128):
        ks = rope(sel[:, 0:KD], sel[:, KD:2 * KD], c)
        kw = rope(win[:, 0:KD], win[:, KD:2 * KD], c)
        rows_ref[0, :, 2 * KD + c * 128:2 * KD + (c + 1) * 128] = ks
        win_ref[0, :, c * 128:(c + 1) * 128] = kw
        kvb_ref[0, :, c * 128:(c + 1) * 128] = ks.astype(bf16)
        kvb_ref[0, :, 2 * KD + c * 128:2 * KD + (c + 1) * 128] = kw.astype(bf16)
    rows_ref[0, :, 3 * KD:4 * KD] = sel[:, 2 * KD:3 * KD]
    win_ref[0, :, KD:2 * KD] = win[:, 2 * KD:3 * KD]
    kvb_ref[0, :, KD:2 * KD] = sel[:, 2 * KD:3 * KD].astype(bf16)
    kvb_ref[0, :, 3 * KD:4 * KD] = win[:, 2 * KD:3 * KD].astype(bf16)
    gate_ref[0] = jax.nn.sigmoid(mm(_C_G, _C_END))


def nsa_project(x, shift, scale, w_all, cos, sin):
    b_, t_, d_ = x.shape
    rows = min(PROJ_ROWS, t_)
    tile = lambda n: pl.BlockSpec((1, rows, n), lambda b, t: (b, t, 0))
    mod = _mod_spec(shift, rows)
    tab = pl.BlockSpec((rows, 128), lambda b, t: (t, 0))
    return pl.pallas_call(
        _nsa_proj_kernel,
        grid=(b_, t_ // rows),
        in_specs=[tile(d_), mod, mod, pl.BlockSpec(w_all.shape, lambda b, t: (0, 0), pipeline_mode=pl.Buffered(1)), tab, tab],
        out_specs=[tile(4 * KD), tile(4 * KD), tile(2 * KD), tile(128), tile(4 * KD)],
        out_shape=[jax.ShapeDtypeStruct((b_, t_, 4 * KD), bf16), jax.ShapeDtypeStruct((b_, t_, 4 * KD), f32),
                   jax.ShapeDtypeStruct((b_, t_, 2 * KD), f32), jax.ShapeDtypeStruct((b_, t_, 128), f32),
                   jax.ShapeDtypeStruct((b_, t_, 4 * KD), bf16)],
        compiler_params=pltpu.CompilerParams(dimension_semantics=("arbitrary", "arbitrary"),
                                             vmem_limit_bytes=V7X_VMEM_LIMIT_BYTES),
        name="nsa_project",
    )(x, shift, scale, w_all, cos, sin)


def _gelu_tanh(x):
    return 0.5 * x * (1.0 + jnp.tanh(math.sqrt(2.0 / math.pi) * (x + 0.044715 * (x * x * x))))


def _compress_kernel(xa_ref, xb_ref, pe_ref, w1_ref, b1_ref, w2_ref, w2r_ref, b2_ref, b2r_ref, w2t_ref, w2rt_ref,
                     b2t_ref, b2rt_ref, cos_ref, sin_ref, cost_ref, sint_ref, on_ref, ot_ref, *, projected):
    w1 = w1_ref[0]
    half = w1.shape[0] // 2
    bias = jnp.dot(pe_ref[0], w1, preferred_element_type=f32)[0:1] + b1_ref[0]
    nt = (((1,), (1,)), ((), ()))
    for g in range(xa_ref.shape[2]):
        if projected:
            pre = xa_ref[0, 0, g] + xb_ref[0, 0, g] + bias
        else:
            pre = (jnp.dot(xa_ref[0, 0, g], w1[:half], preferred_element_type=f32)
                   + jnp.dot(xb_ref[0, 0, g], w1[half:], preferred_element_type=f32) + bias)
        hb = _gelu_tanh(pre).astype(bf16)
        y = jnp.dot(hb, w2_ref[0], preferred_element_type=f32) + b2_ref[0]
        yr = jnp.dot(hb, w2r_ref[0], preferred_element_type=f32) + b2r_ref[0]
        on_ref[0, 0, g] = y * cos_ref[0] + yr * sin_ref[0]
        yt = lax.dot_general(w2t_ref[0], hb, nt, preferred_element_type=f32) + b2t_ref[0]
        yrt = lax.dot_general(w2rt_ref[0], hb, nt, preferred_element_type=f32) + b2rt_ref[0]
        ot_ref[0, 0, g] = yt * cost_ref[0] + yrt * sint_ref[0]


def compress_kv(chunks, pe, w1, b1, w2, b2, projected=False):
    b_, _, g_, n_ch, feat = chunks.shape
    if projected:
        feat = CMP_HIDDEN
        first, second = chunks[..., :feat], chunks[..., feat:]
    else:
        first = second = chunks
    chunks = first
    nxt = jnp.concatenate([second[:, :, :, 1:], jnp.zeros_like(second[:, :, :, :1])], axis=3)
    cend = jnp.arange(n_ch) * CMP_STRIDE + CMP_BLK - 1
    cos, sin = _rope_tables(cend, HEAD_DIM)
    cos = jnp.stack([cos, jnp.ones_like(cos)])
    sin = jnp.stack([sin, jnp.zeros_like(sin)])
    w2b = w2.astype(bf16)
    w2r = _rot_cols(w2).astype(bf16)
    b2r = _rot_cols(b2[:, None, :])
    pe8 = jnp.broadcast_to(pe.reshape(2, 1, -1), (2, 8, pe.shape[1] * pe.shape[2])).astype(bf16)
    xspec = pl.BlockSpec((1, 1, g_, n_ch, feat), lambda b, k: (b, k, 0, 0, 0))
    per_kv = lambda *s: pl.BlockSpec((1,) + s, lambda b, k: (k,) + (0,) * len(s))
    return pl.pallas_call(
        functools.partial(_compress_kernel, projected=projected),
        grid=(b_, 2),
        in_specs=[xspec, xspec, per_kv(8, pe8.shape[2]), per_kv(*w1.shape[1:]), per_kv(1, CMP_HIDDEN),
                  per_kv(CMP_HIDDEN, HEAD_DIM), per_kv(CMP_HIDDEN, HEAD_DIM), per_kv(1, HEAD_DIM), per_kv(1, HEAD_DIM),
                  per_kv(HEAD_DIM, CMP_HIDDEN), per_kv(HEAD_DIM, CMP_HIDDEN), per_kv(HEAD_DIM, 1), per_kv(HEAD_DIM, 1),
                  per_kv(n_ch, HEAD_DIM), per_kv(n_ch, HEAD_DIM), per_kv(HEAD_DIM, n_ch), per_kv(HEAD_DIM, n_ch)],
        out_specs=[pl.BlockSpec((1, 1, g_, n_ch, HEAD_DIM), lambda b, k: (b, k, 0, 0, 0)),
                   pl.BlockSpec((1, 1, g_, HEAD_DIM, n_ch), lambda b, k: (b, k, 0, 0, 0))],
        out_shape=[jax.ShapeDtypeStruct((b_, 2, g_, n_ch, HEAD_DIM), f32),
                   jax.ShapeDtypeStruct((b_, 2, g_, HEAD_DIM, n_ch), f32)],
        compiler_params=pltpu.CompilerParams(dimension_semantics=("arbitrary",) * 2,
                                             vmem_limit_bytes=V7X_VMEM_LIMIT_BYTES),
        name="compress_kv",
    )(chunks, nxt, pe8, w1.astype(bf16), b1[:, None, :], w2b, w2r, b2[:, None, :], b2r,
      w2b.transpose(0, 2, 1), w2r.transpose(0, 2, 1), b2[:, :, None], b2r.transpose(0, 2, 1),
      cos, sin, cos.transpose(0, 2, 1), sin.transpose(0, 2, 1))


def _cmp_to_sel_matrix(n_c, n_sel):
    cs = jnp.arange(n_c)[:, None] * CMP_STRIDE
    ss = jnp.arange(n_sel)[None, :] * SEL_BLK
    ov = jnp.minimum(cs + CMP_BLK, ss + SEL_BLK) - jnp.maximum(cs, ss)
    return jnp.clip(ov, 0, None).astype(f32) / CMP_BLK


def _lanes(col, n):
    if n % 128:
        return jnp.broadcast_to(col, (col.shape[0], n))
    tile = jnp.broadcast_to(col, (col.shape[0], 128))
    return tile if n == 128 else jnp.concatenate([tile] * (n // 128), axis=1)


def _softmax_rows(s, valid):
    n = s.shape[1]
    e = jnp.where(valid, jnp.exp(s - _lanes(jnp.max(s, axis=1, keepdims=True), n)), 0.0)
    l = jnp.sum(e, axis=1, keepdims=True)
    return e * _lanes(1.0 / jnp.maximum(l, 1e-30), n)


def _unselected_blocks(imp_t, qpos, n_sel):
    jj = lax.broadcasted_iota(jnp.int32, imp_t.shape, 0)
    cur = qpos // SEL_BLK
    forced = (jj == 0) | (jj == cur) | (jj == cur - 1)
    sc0 = jnp.where((jj <= cur) & (jj < n_sel), jnp.where(forced, -NEG, imp_t), NEG)
    jjf = jj.astype(f32)

    def pick(_, carry):
        sc, nonsel = carry
        m = jnp.max(sc, axis=0, keepdims=True)
        idx = jnp.min(jnp.where(sc == m, jjf, float(imp_t.shape[0])), axis=0, keepdims=True)
        hit = (jjf == idx) & (m > NEG)
        return jnp.where(hit, NEG, sc), jnp.where(hit, 0.0, nonsel)

    return lax.fori_loop(0, min(TOPK, n_sel), pick, (sc0, jnp.ones(imp_t.shape, f32)))[1]


def _nsa_attn_kernel(q_ref, kct_ref, vc_ref, ks_ref, vs_ref, *refs, n_cmp, n_sel):
    n_win = WINDOW // Q_BLOCK + 1
    kw_refs, vw_refs = refs[:n_win], refs[n_win:2 * n_win]
    gate_ref, msel_ref, o_ref, oc_sc, score_sc, nonsel_sc, m_sc, acc_sc = refs[2 * n_win:]
    qt = pl.program_id(1)
    s0 = qt * Q_BLOCK
    rq = Q_PER_KV * Q_BLOCK
    row_q = s0 + (lax.broadcasted_iota(jnp.int32, (rq, 1), 0) & (Q_BLOCK - 1))

    def cmp_branch(ncol):
        cend = lax.broadcasted_iota(jnp.int32, (1, ncol), 1) * CMP_STRIDE + (CMP_BLK - 1)
        cvalid = cend <= row_q
        for g in range(N_KV):
            s = jnp.dot(q_ref[0, g, 0], kct_ref[0, g, :, 0:ncol], preferred_element_type=f32)
            p = _softmax_rows(jnp.where(cvalid, s, NEG), cvalid)
            oc_sc[g] = jnp.dot(p.astype(bf16), vc_ref[0, g, 0:ncol, :], preferred_element_type=f32)
            ps = p[0:Q_BLOCK]
            for r in range(1, Q_PER_KV):
                ps = ps + p[r * Q_BLOCK:(r + 1) * Q_BLOCK]
            hi = ps.astype(bf16)
            lo = (ps - hi.astype(f32)).astype(bf16)
            imp = (jnp.dot(hi, msel_ref[0:ncol, :], preferred_element_type=f32)
                   + jnp.dot(lo, msel_ref[0:ncol, :], preferred_element_type=f32))
            score_sc[:, g * Q_BLOCK:(g + 1) * Q_BLOCK] = imp.T

    widths = [w for w in range(128, n_cmp + 1, 128)] or [n_cmp]
    n_live = (s0 + Q_BLOCK - CMP_BLK) // CMP_STRIDE + 1
    lax.switch(jnp.clip((n_live + 127) // 128 - 1, 0, len(widths) - 1), [functools.partial(cmp_branch, w) for w in widths])

    qpos = s0 + (lax.broadcasted_iota(jnp.int32, (n_sel, N_KV * Q_BLOCK), 1) & (Q_BLOCK - 1))
    nonsel = _unselected_blocks(score_sc[...], qpos, n_sel)
    for g in range(N_KV):
        nonsel_sc[g] = nonsel[:, g * Q_BLOCK:(g + 1) * Q_BLOCK].T.astype(bf16)

    n_tiles = (s0 + Q_BLOCK + SEL_TK - 1) // SEL_TK
    key_l = lax.broadcasted_iota(jnp.int32, (1, SEL_TK), 1)
    blk_j = lax.broadcasted_iota(jnp.int32, (n_sel, SEL_TK), 0)
    blk_l = lax.broadcasted_iota(jnp.int32, (n_sel, SEL_TK), 1)
    wkey_l = lax.broadcasted_iota(jnp.int32, (1, Q_BLOCK), 1)
    m_sc[...] = jnp.full_like(m_sc, NEG)
    acc_sc[...] = jnp.zeros_like(acc_sc)

    def sel_tile(kt, causal):
        expand = jnp.where(blk_j == (kt * SEL_TK + blk_l) // SEL_BLK, NEG, 0.0).astype(bf16)
        ss = []
        for g in range(N_KV):
            s = jnp.dot(q_ref[0, g, 0], ks_ref[0, g, kt], preferred_element_type=f32)
            bias = jnp.dot(nonsel_sc[g], expand, preferred_element_type=f32)
            s = s + jnp.concatenate([bias] * Q_PER_KV, axis=0)
            if causal:
                s = jnp.where(kt * SEL_TK + key_l <= row_q, s, NEG)
            ss.append(s)
        for g in range(N_KV):
            m_old = m_sc[g]
            m_new = jnp.maximum(m_old, jnp.broadcast_to(jnp.max(ss[g], axis=1, keepdims=True), m_old.shape))
            p = jnp.exp(ss[g] - jnp.concatenate([m_new] * (SEL_TK // 128), axis=1)).astype(bf16)
            acc_sc[g] = jnp.exp(m_old - m_new) * acc_sc[g] + jnp.dot(p, vs_ref[0, g, kt], preferred_element_type=f32)
            m_sc[g] = m_new

    def body(kt, carry):
        sel_tile(kt, False)
        return carry

    lax.fori_loop(0, n_tiles - 1, body, 0)
    sel_tile(n_tiles - 1, True)

    for g in range(N_KV):
        q = q_ref[0, g, 0]
        acc = acc_sc[g]
        o_s = acc[:, 0:HEAD_DIM] * (1.0 / jnp.maximum(acc[:, HEAD_DIM:HEAD_DIM + 1], 1e-30))

        s_parts = []
        for i in range(n_win):
            kpos = (qt - (n_win - 1) + i) * Q_BLOCK + wkey_l
            dpos = row_q - kpos
            ok = (kpos >= 0) & (dpos >= 0) & (dpos < WINDOW)
            s_parts.append(jnp.where(ok, jnp.dot(q, kw_refs[i][0, g, 0], preferred_element_type=f32), NEG))
        s_w = jnp.concatenate(s_parts, axis=1)
        e_w = jnp.exp(s_w - _lanes(jnp.max(s_w, axis=1, keepdims=True), s_w.shape[1])).astype(bf16)
        acc_w = jnp.dot(e_w[:, 0:Q_BLOCK], vw_refs[0][0, g, 0], preferred_element_type=f32)
        for i in range(1, n_win):
            acc_w = acc_w + jnp.dot(e_w[:, i * Q_BLOCK:(i + 1) * Q_BLOCK], vw_refs[i][0, g, 0], preferred_element_type=f32)
        o_w = acc_w[:, 0:HEAD_DIM] * (1.0 / jnp.maximum(acc_w[:, HEAD_DIM:HEAD_DIM + 1], 1e-30))

        o_c = oc_sc[g]
        gates = gate_ref[0]
        for r in range(Q_PER_KV):
            c = (g * Q_PER_KV + r) * 3
            rows = slice(r * Q_BLOCK, (r + 1) * Q_BLOCK)
            o = (gates[:, c:c + 1] * o_c[rows] + gates[:, c + 1:c + 2] * o_s[rows] + gates[:, c + 2:c + 3] * o_w[rows])
            o_ref[0, g, 0, rows, :] = o.astype(bf16)


def _with_ones(v):
    pad = jnp.zeros(v.shape[:-1] + (128 - v.shape[-1] - 1,), v.dtype)
    return jnp.concatenate([v, jnp.ones(v.shape[:-1] + (1,), v.dtype), pad], axis=-1)


def nsa_attention(q, kvb, gates, kct, vc):
    b_, t_, _ = q.shape
    n_qt, n_st, n_sel = t_ // Q_BLOCK, t_ // SEL_TK, t_ // SEL_BLK
    n_cmp = kct.shape[-1]
    rq = Q_PER_KV * Q_BLOCK
    q5 = q.reshape(b_, n_qt, Q_BLOCK, N_KV, Q_PER_KV, HEAD_DIM).transpose(0, 3, 1, 4, 2, 5).reshape(b_, N_KV, n_qt, rq, HEAD_DIM)

    def tiles(x, tk):
        return x.reshape(b_, t_ // tk, tk, N_KV, HEAD_DIM).transpose(0, 3, 1, 2, 4)

    ks = tiles(kvb[..., 0:KD], SEL_TK).transpose(0, 1, 2, 4, 3)
    vs = _with_ones(tiles(kvb[..., KD:2 * KD], SEL_TK))
    kw = tiles(kvb[..., 2 * KD:3 * KD], Q_BLOCK).transpose(0, 1, 2, 4, 3)
    vw = _with_ones(tiles(kvb[..., 3 * KD:4 * KD], Q_BLOCK))
    msel = _cmp_to_sel_matrix(n_cmp, n_sel).astype(bf16)
    per_b = lambda x: pl.BlockSpec((1,) + x.shape[1:], lambda b, t: (b,) + (0,) * (x.ndim - 1), pipeline_mode=pl.Buffered(1))
    qspec = pl.BlockSpec((1, N_KV, 1, rq, HEAD_DIM), lambda b, t: (b, 0, t, 0, 0))
    n_win = WINDOW // Q_BLOCK + 1

    def win_specs(x):
        def spec(i):
            return pl.BlockSpec((1, N_KV, 1) + x.shape[3:], lambda b, t: (b, 0, jnp.maximum(t - (n_win - 1) + i, 0), 0, 0))
        return [spec(i) for i in range(n_win)]

    o5 = pl.pallas_call(
        functools.partial(_nsa_attn_kernel, n_cmp=n_cmp, n_sel=n_sel),
        grid=(b_, n_qt),
        in_specs=[qspec, per_b(kct), per_b(vc), per_b(ks), per_b(vs)] + win_specs(kw) + win_specs(vw)
        + [pl.BlockSpec((1, Q_BLOCK, 128), lambda b, t: (b, t, 0)), pl.BlockSpec(msel.shape, lambda b, t: (0, 0))],
        out_specs=qspec,
        out_shape=jax.ShapeDtypeStruct(q5.shape, bf16),
        scratch_shapes=[pltpu.VMEM((N_KV, rq, HEAD_DIM), f32), pltpu.VMEM((n_sel, N_KV * Q_BLOCK), f32),
                        pltpu.VMEM((N_KV, Q_BLOCK, n_sel), bf16), pltpu.VMEM((N_KV, rq, 128), f32), pltpu.VMEM((N_KV, rq, 128), f32)],
        compiler_params=pltpu.CompilerParams(dimension_semantics=("arbitrary", "arbitrary"),
                                             vmem_limit_bytes=V7X_VMEM_LIMIT_BYTES),
        name="nsa_attention",
    )(q5, kct, vc, ks, vs, *([kw] * n_win), *([vw] * n_win), gates, msel)
    return o5.reshape(b_, N_KV, n_qt, Q_PER_KV, Q_BLOCK, HEAD_DIM).transpose(0, 2, 4, 1, 3, 5).reshape(b_, t_, N_HEADS * HEAD_DIM)


def _mixer_out_kernel(y_ref, x_ref, gt_ref, w_ref, lg_ref, lb_ref, o_ref):
    f = jnp.dot(y_ref[0], w_ref[...], preferred_element_type=f32)
    y = ALPHA * x_ref[0] + (1.0 + gt_ref[0]) * f
    mu = jnp.mean(y, axis=-1, keepdims=True)
    yc = y - mu
    var = jnp.mean(yc * yc, axis=-1, keepdims=True)
    o_ref[0] = yc * lax.rsqrt(var + LN_EPS) * lg_ref[...] + lb_ref[...]


def mixer_out(y, x, gate, w, ln_g, ln_b):
    b_, t_, d_ = x.shape
    rows = min(PROJ_ROWS, t_)
    vec = pl.BlockSpec((1, d_), lambda b, t: (0, 0))
    return pl.pallas_call(
        _mixer_out_kernel,
        grid=(b_, t_ // rows),
        in_specs=[pl.BlockSpec((1, rows, y.shape[2]), lambda b, t: (b, t, 0)), pl.BlockSpec((1, rows, d_), lambda b, t: (b, t, 0)),
                  _mod_spec(gate, rows), pl.BlockSpec(w.shape, lambda b, t: (0, 0)), vec, vec],
        out_specs=pl.BlockSpec((1, rows, d_), lambda b, t: (b, t, 0)),
        out_shape=jax.ShapeDtypeStruct(x.shape, f32),
        compiler_params=pltpu.CompilerParams(dimension_semantics=("arbitrary", "arbitrary"),
                                             vmem_limit_bytes=V7X_VMEM_LIMIT_BYTES),
        name="mixer_out",
    )(y, x, gate, w, ln_g.reshape(1, d_), ln_b.reshape(1, d_))


def nsa_prompt_layer(x, shift, scale, gate, w_in, w_o, pe, cw1, cb1, cw2, cb2, ln_g, ln_b):
    b_, t_, _ = x.shape
    cos, sin = _rope_tables(jnp.arange(t_), 128)
    q, rows, win, gates, kvb = nsa_project(x, shift, scale, _nsa_proj_weights(w_in), cos, sin)
    n_ch = t_ // CMP_STRIDE
    chunks = rows[..., 0:2 * KD].astype(bf16).reshape(b_, n_ch, CMP_STRIDE, 2, N_KV, HEAD_DIM)
    chunks = chunks.transpose(0, 3, 4, 1, 2, 5).reshape(b_, 2, N_KV, n_ch, CMP_STRIDE * HEAD_DIM)
    cmp_n, cmp_t = compress_kv(chunks, pe, cw1, cb1, cw2, cb2)
    o = nsa_attention(q, kvb, gates, cmp_t[:, 0].astype(bf16), cmp_n[:, 1].astype(bf16))
    x_new = mixer_out(o, x, gate, w_o.astype(bf16), ln_g, ln_b)
    return x_new, rows.reshape(b_, t_, 4, N_KV, HEAD_DIM), win.reshape(b_, t_, 2, N_KV, HEAD_DIM)


PAGES_PER_STEP = 32
S_PAD = 8
CHUNKS_PER_PAGE = PAGE_SIZE // CMP_STRIDE
WIN_KEYS_PAD = 128


def _page_specs(block, tail):
    def spec(i):
        return pl.BlockSpec(block, lambda b, pg, pt: (pt[b, pg * PAGES_PER_STEP + i],) + tail)
    return [spec(i) for i in range(PAGES_PER_STEP)]


def _cmp_pages_kernel(pt_ref, *refs):
    page_refs = refs[:PAGES_PER_STEP]
    perm_ref, w_ref, o_ref = refs[PAGES_PER_STEP:]
    nt = (((1,), (1,)), ((), ()))
    xp = [lax.dot_general(perm_ref[...], r[0, 0].reshape(2 * KD, PAGE_SIZE).astype(bf16), nt, preferred_element_type=f32)
          for r in page_refs]
    cpp = CHUNKS_PER_PAGE
    xs = [jnp.concatenate([x[s * cpp:(s + 1) * cpp] for x in xp], axis=0) for s in range(CMP_STRIDE)]
    for kv in range(2):
        groups = []
        for g in range(N_KV):
            lo = (kv * N_KV + g) * HEAD_DIM
            groups.append(jnp.concatenate([x[:, lo:lo + HEAD_DIM] for x in xs], axis=1))
        xc = jnp.concatenate(groups, axis=0).astype(bf16)
        o_ref[0, 0, kv] = jnp.dot(xc, w_ref[kv], preferred_element_type=f32)


def cmp_pages(pages, li, page_table, w01):
    b_, n_pages = page_table.shape
    n_pg = n_pages // PAGES_PER_STEP
    rows = PAGES_PER_STEP * N_KV * CHUNKS_PER_PAGE
    pos = jnp.arange(PAGE_SIZE)
    perm = (pos[None, :] == (pos[:, None] % CHUNKS_PER_PAGE) * CMP_STRIDE + pos[:, None] // CHUNKS_PER_PAGE).astype(bf16)
    out = pl.pallas_call(
        _cmp_pages_kernel,
        grid_spec=pltpu.PrefetchScalarGridSpec(
            num_scalar_prefetch=1, grid=(b_, n_pg),
            in_specs=_page_specs((1, 1, 2, N_KV, HEAD_DIM, PAGE_SIZE), (li, 0, 0, 0, 0))
            + [pl.BlockSpec(perm.shape, lambda b, pg, pt: (0, 0)), pl.BlockSpec(w01.shape, lambda b, pg, pt: (0, 0, 0))],
            out_specs=pl.BlockSpec((1, 1, 2, rows, w01.shape[2]), lambda b, pg, pt: (b, pg, 0, 0, 0))),
        out_shape=jax.ShapeDtypeStruct((b_, n_pg, 2, rows, w01.shape[2]), f32),
        compiler_params=pltpu.CompilerParams(dimension_semantics=("arbitrary", "arbitrary"),
                                             vmem_limit_bytes=V7X_VMEM_LIMIT_BYTES),
        name="cmp_pages",
    )(page_table, *([pages] * PAGES_PER_STEP), perm, w01)
    out = out.reshape(b_, n_pg, 2, N_KV, PAGES_PER_STEP, CHUNKS_PER_PAGE, w01.shape[2])
    return out.transpose(0, 2, 3, 1, 4, 5, 6).reshape(b_, 2, N_KV, n_pages * CHUNKS_PER_PAGE, w01.shape[2])


def _sample_cmp_win_kernel(q_ref, kct_ref, vc_ref, kw_ref, vw_ref, msel_ref, oc_ref, ow_ref, imp_ref, *, past, w_buf, n_wk):
    rows = Q_PER_KV * S_PAD
    qpos = past + (lax.broadcasted_iota(jnp.int32, (rows, 1), 0) & (S_PAD - 1))
    cend = lax.broadcasted_iota(jnp.int32, (1, kct_ref.shape[3]), 1) * CMP_STRIDE + (CMP_BLK - 1)
    cvalid = cend <= qpos
    widx = lax.broadcasted_iota(jnp.int32, (1, kw_ref.shape[3]), 1)
    kwpos = past - w_buf + widx
    dpos = qpos - kwpos
    wok = (dpos >= 0) & (dpos < WINDOW) & (kwpos >= 0) & (widx < n_wk)
    for g in range(N_KV):
        q = q_ref[0, g]
        s = jnp.dot(q, kct_ref[0, g], preferred_element_type=f32)
        p = _softmax_rows(jnp.where(cvalid, s, NEG), cvalid)
        oc_ref[0, g] = jnp.dot(p.astype(bf16), vc_ref[0, g], preferred_element_type=f32)
        hi = p.astype(bf16)
        lo = (p - hi.astype(f32)).astype(bf16)
        imp = jnp.dot(hi, msel_ref[...], preferred_element_type=f32) + jnp.dot(lo, msel_ref[...], preferred_element_type=f32)
        tot = imp[0:S_PAD]
        for r in range(1, Q_PER_KV):
            tot = tot + imp[r * S_PAD:(r + 1) * S_PAD]
        imp_ref[0, g] = tot
        sw = jnp.where(wok, jnp.dot(q, kw_ref[0, g], preferred_element_type=f32), NEG)
        ew = jnp.exp(sw - _lanes(jnp.max(sw, axis=1, keepdims=True), sw.shape[1])).astype(bf16)
        acc = jnp.dot(ew, vw_ref[0, g], preferred_element_type=f32)
        ow_ref[0, g] = acc[:, 0:HEAD_DIM] * (1.0 / jnp.maximum(acc[:, HEAD_DIM:HEAD_DIM + 1], 1e-30))


def _topk_kernel(imp_ref, o_ref, *, past, n_sel):
    qpos = past + (lax.broadcasted_iota(jnp.int32, imp_ref.shape, 1) & (S_PAD - 1))
    o_ref[...] = _unselected_blocks(imp_ref[...], qpos, n_sel)


def _sample_sel_kernel(pt_ref, *refs, past, n_new):
    page_refs = refs[:PAGES_PER_STEP]
    q_ref, nonsel_ref, knew_ref, vnew_ref, os_ref, m_sc, l_sc, acc_sc = refs[PAGES_PER_STEP:]
    pg = pl.program_id(1)
    nt = (((1,), (1,)), ((), ()))
    rows = Q_PER_KV * S_PAD

    @pl.when(pg == 0)
    def _():
        m_sc[...] = jnp.full_like(m_sc, NEG)
        l_sc[...] = jnp.zeros_like(l_sc)
        acc_sc[...] = jnp.zeros_like(acc_sc)

    def accumulate(s, pv):
        m_old = m_sc[...]
        m_new = jnp.maximum(m_old, jnp.max(s, axis=1, keepdims=True))
        p = jnp.exp(s - m_new)
        alpha = jnp.exp(m_old - m_new)
        l_sc[...] = alpha * l_sc[...] + jnp.sum(p, axis=1, keepdims=True)
        pb = p.astype(bf16)
        upd = jnp.concatenate([pv(g, pb[g * rows:(g + 1) * rows]) for g in range(N_KV)], axis=0)
        acc_sc[...] = alpha * acc_sc[...] + upd
        m_sc[...] = m_new

    nk = PAGES_PER_STEP * PAGE_SIZE
    kt = [jnp.concatenate([r[0, 0, 0, g] for r in page_refs], axis=1).astype(bf16) for g in range(N_KV)]
    vt = [jnp.concatenate([r[0, 0, 1, g] for r in page_refs], axis=1).astype(bf16) for g in range(N_KV)]
    nonsel = nonsel_ref[0]
    s = jnp.concatenate([jnp.dot(q_ref[0, g], kt[g], preferred_element_type=f32) for g in range(N_KV)], axis=0)
    blk_j = lax.broadcasted_iota(jnp.int32, (nonsel.shape[1], nk), 0)
    key = pg * nk + lax.broadcasted_iota(jnp.int32, (nonsel.shape[1], nk), 1)
    expand = jnp.where(blk_j == key // SEL_BLK, NEG, 0.0).astype(bf16)
    accumulate(s + jnp.dot(nonsel, expand, preferred_element_type=f32),
               lambda g, p: lax.dot_general(p, vt[g], nt, preferred_element_type=f32))

    @pl.when(pg == pl.num_programs(1) - 1)
    def _():
        sn = jnp.concatenate([jnp.dot(q_ref[0, g], knew_ref[0, g], preferred_element_type=f32) for g in range(N_KV)], axis=0)
        t_new = lax.broadcasted_iota(jnp.int32, (1, sn.shape[1]), 1)
        s_row = lax.broadcasted_iota(jnp.int32, (sn.shape[0], 1), 0) & (S_PAD - 1)
        new_blk = past // SEL_BLK
        bias_new = nonsel[:, new_blk:new_blk + 1].astype(f32) * NEG
        accumulate(jnp.where((t_new < n_new) & (t_new <= s_row), sn + bias_new, NEG),
                   lambda g, p: jnp.dot(p, vnew_ref[0, g], preferred_element_type=f32))
        o = acc_sc[...] * (1.0 / jnp.maximum(l_sc[...], 1e-30))
        for g in range(N_KV):
            os_ref[0, g] = o[g * rows:(g + 1) * rows]


def _gated_out_kernel(oc_ref, os_ref, ow_ref, gc_ref, gs_ref, gw_ref, x_ref, gt_ref, w_ref, lg_ref, lb_ref, o_ref):
    o = gc_ref[0] * oc_ref[0] + gs_ref[0] * os_ref[0] + gw_ref[0] * ow_ref[0]
    f = jnp.dot(o.astype(bf16), w_ref[...], preferred_element_type=f32)
    y = ALPHA * x_ref[0] + (1.0 + gt_ref[0]) * f
    mu = jnp.mean(y, axis=-1, keepdims=True)
    yc = y - mu
    var = jnp.mean(yc * yc, axis=-1, keepdims=True)
    o_ref[0] = yc * lax.rsqrt(var + LN_EPS) * lg_ref[...] + lb_ref[...]


def nsa_sample_layer(x, shift, scale, gate, cache_kv, page_table, li, win_buf, w_in, w_o, pe, cw1, cb1, cw2, cb2, ln_g, ln_b):
    b_, s_, d_ = x.shape
    n_pool, n_pages = cache_kv.shape[0], page_table.shape[1]
    past = n_pages * PAGE_SIZE
    w_buf = win_buf.shape[1]
    n_ch = past // CMP_STRIDE
    n_sel = past // SEL_BLK + 1
    assert past % SEL_BLK == 0 and s_ <= S_PAD and n_pages % PAGES_PER_STEP == 0
    assert (past + s_ - CMP_BLK) // CMP_STRIDE + 2 == n_ch
    rows_n = b_ * s_
    rq = Q_PER_KV * S_PAD
    flat = lambda m: jnp.broadcast_to(m, (b_, s_, d_)).reshape(1, rows_n, d_)
    xf = x.reshape(1, rows_n, d_)
    cos, sin = _rope_tables(past + (jnp.arange(rows_n) % s_), 128)
    q, rows, win, gates, kvb = nsa_project(xf, flat(shift), flat(scale), _nsa_proj_weights(w_in), cos, sin)

    pages = cache_kv.transpose(0, 1, 3, 4, 5, 2)
    half = cw1.shape[1] // 2
    w01 = jnp.concatenate([cw1[:, :half], cw1[:, half:]], axis=2).astype(bf16)
    cmp_n, cmp_t = compress_kv(cmp_pages(pages, li, page_table, w01), pe, cw1, cb1, cw2, cb2, projected=True)

    q32 = jnp.pad(q.reshape(b_, s_, N_KV, Q_PER_KV, HEAD_DIM), ((0, 0), (0, S_PAD - s_), (0, 0), (0, 0), (0, 0)))
    q32 = q32.transpose(0, 2, 3, 1, 4).reshape(b_, N_KV, rq, HEAD_DIM)

    w_all = jnp.concatenate([win_buf, win.reshape(b_, s_, 2, N_KV, HEAD_DIM)], axis=1)
    n_wk = w_buf + s_
    wk_pad = -(-n_wk // WIN_KEYS_PAD) * WIN_KEYS_PAD
    w_pad = jnp.pad(w_all, ((0, 0), (0, wk_pad - n_wk), (0, 0), (0, 0), (0, 0))).astype(bf16)
    kw = w_pad[:, :, 0].transpose(0, 2, 3, 1)
    vw = _with_ones(w_pad[:, :, 1].transpose(0, 2, 1, 3))
    n_sel_pad = 2 * 128
    msel = jnp.pad(_cmp_to_sel_matrix(n_ch, n_sel), ((0, 0), (0, n_sel_pad - n_sel))).astype(bf16)
    per_b = lambda a: pl.BlockSpec((1,) + a.shape[1:], lambda b: (b,) + (0,) * (a.ndim - 1))
    kct, vc = cmp_t[:, 0].astype(bf16), cmp_n[:, 1].astype(bf16)
    o_shape = jax.ShapeDtypeStruct((b_, N_KV, rq, HEAD_DIM), f32)
    o_spec = pl.BlockSpec((1, N_KV, rq, HEAD_DIM), lambda b: (b, 0, 0, 0))
    oc, ow, imp = pl.pallas_call(
        functools.partial(_sample_cmp_win_kernel, past=past, w_buf=w_buf, n_wk=n_wk),
        grid=(b_,),
        in_specs=[per_b(q32), per_b(kct), per_b(vc), per_b(kw), per_b(vw), pl.BlockSpec(msel.shape, lambda b: (0, 0))],
        out_specs=[o_spec, o_spec, pl.BlockSpec((1, N_KV, S_PAD, n_sel_pad), lambda b: (b, 0, 0, 0))],
        out_shape=[o_shape, o_shape, jax.ShapeDtypeStruct((b_, N_KV, S_PAD, n_sel_pad), f32)],
        compiler_params=pltpu.CompilerParams(dimension_semantics=("arbitrary",)),
        name="sample_cmp_win",
    )(q32, kct, vc, kw, vw, msel)

    sel_rows = -(-n_sel // 8) * 8
    imp_t = imp[..., :sel_rows].transpose(3, 0, 1, 2).reshape(sel_rows, b_ * N_KV * S_PAD)
    nonsel_t = pl.pallas_call(
        functools.partial(_topk_kernel, past=past, n_sel=n_sel),
        out_shape=jax.ShapeDtypeStruct(imp_t.shape, f32),
        name="sample_topk",
    )(imp_t)
    nonsel = nonsel_t.reshape(sel_rows, b_, N_KV, 1, S_PAD).transpose(1, 2, 3, 4, 0)
    nonsel = jnp.broadcast_to(nonsel, (b_, N_KV, Q_PER_KV, S_PAD, sel_rows)).reshape(b_, N_KV * rq, sel_rows)
    nonsel = jnp.pad(nonsel, ((0, 0), (0, 0), (0, n_sel_pad - sel_rows)), constant_values=1.0).astype(bf16)

    new_pad = lambda a: jnp.pad(a.reshape(b_, s_, N_KV, HEAD_DIM), ((0, 0), (0, 128 - s_), (0, 0), (0, 0)))
    knew = new_pad(kvb[..., 0:KD]).transpose(0, 2, 3, 1)
    vnew = new_pad(kvb[..., KD:2 * KD]).transpose(0, 2, 1, 3)
    rows_all = N_KV * rq
    per_b2 = lambda a: pl.BlockSpec((1,) + a.shape[1:], lambda b, pg, pt: (b,) + (0,) * (a.ndim - 1))
    o_s = pl.pallas_call(
        functools.partial(_sample_sel_kernel, past=past, n_new=s_),
        grid_spec=pltpu.PrefetchScalarGridSpec(
            num_scalar_prefetch=1, grid=(b_, n_pages // PAGES_PER_STEP),
            in_specs=_page_specs((1, 1, 2, N_KV, HEAD_DIM, PAGE_SIZE), (li, 1, 0, 0, 0))
            + [per_b2(q32), per_b2(nonsel), per_b2(knew), per_b2(vnew)],
            out_specs=pl.BlockSpec((1, N_KV, rq, HEAD_DIM), lambda b, pg, pt: (b, 0, 0, 0)),
            scratch_shapes=[pltpu.VMEM((rows_all, 1), f32), pltpu.VMEM((rows_all, 1), f32), pltpu.VMEM((rows_all, HEAD_DIM), f32)]),
        out_shape=o_shape,
        compiler_params=pltpu.CompilerParams(dimension_semantics=("arbitrary", "arbitrary"),
                                             vmem_limit_bytes=V7X_VMEM_LIMIT_BYTES),
        name="sample_sel",
    )(page_table, *([pages] * PAGES_PER_STEP), q32, nonsel, knew, vnew)

    tok = lambda o: o.reshape(b_, N_KV, Q_PER_KV, S_PAD, HEAD_DIM)[:, :, :, :s_].transpose(0, 3, 1, 2, 4).reshape(1, rows_n, N_HEADS * HEAD_DIM)
    gexp = lambda br: jnp.repeat(gates[..., br:N_GATE:3], HEAD_DIM, axis=-1)
    full = lambda n: pl.BlockSpec((1, rows_n, n), lambda i: (0, 0, 0))
    vec = pl.BlockSpec((1, d_), lambda i: (0, 0))
    x_new = pl.pallas_call(
        _gated_out_kernel,
        grid=(1,),
        in_specs=[full(N_HEADS * HEAD_DIM)] * 6 + [full(d_), full(d_), pl.BlockSpec(w_o.shape, lambda i: (0, 0)), vec, vec],
        out_specs=full(d_),
        out_shape=jax.ShapeDtypeStruct((1, rows_n, d_), f32),
        name="sample_gated_out",
    )(tok(oc), tok(o_s), tok(ow), gexp(0), gexp(1), gexp(2), xf, flat(gate), w_o.astype(bf16), ln_g.reshape(1, d_), ln_b.reshape(1, d_))
    return x_new.reshape(b_, s_, d_), rows.reshape(b_, s_, 4, N_KV, HEAD_DIM), w_all[:, n_wk - w_buf:]


HPG = SSM_HEADS // SSM_GROUPS
GN = SSM_GROUPS * D_STATE
GW = D_INNER // SSM_GROUPS
DT_PAD = 128
CONV_KEEP = 8


def _ssd_proj_kernel(x_ref, sh_ref, sc_ref, w_ref, z_ref, xbc_ref, dt_ref):
    hb = (x_ref[0] * (1.0 + sc_ref[0]) + sh_ref[0]).astype(bf16)
    z_ref[0] = jnp.dot(hb, w_ref[:, 0:D_INNER], preferred_element_type=f32)
    xbc_ref[0] = jnp.dot(hb, w_ref[:, D_INNER:D_INNER + CONV_DIM], preferred_element_type=f32)
    dt_ref[0] = jnp.dot(hb, w_ref[:, D_INNER + CONV_DIM:], preferred_element_type=f32)


def ssd_project(x, shift, scale, w_in):
    b_, t_, d_ = x.shape
    rows = min(PROJ_ROWS, t_)
    w_all = jnp.pad(w_in, ((0, 0), (0, DT_PAD - SSM_HEADS))).astype(bf16)
    tile = lambda n: pl.BlockSpec((1, rows, n), lambda b, t: (b, t, 0))
    mod = _mod_spec(shift, rows)
    return pl.pallas_call(
        _ssd_proj_kernel,
        grid=(b_, t_ // rows),
        in_specs=[tile(d_), mod, mod, pl.BlockSpec(w_all.shape, lambda b, t: (0, 0), pipeline_mode=pl.Buffered(1))],
        out_specs=[tile(D_INNER), tile(CONV_DIM), tile(DT_PAD)],
        out_shape=[jax.ShapeDtypeStruct((b_, t_, D_INNER), f32), jax.ShapeDtypeStruct((b_, t_, CONV_DIM), f32),
                   jax.ShapeDtypeStruct((b_, t_, DT_PAD), f32)],
        compiler_params=pltpu.CompilerParams(dimension_semantics=("arbitrary", "arbitrary"),
                                             vmem_limit_bytes=V7X_VMEM_LIMIT_BYTES),
        name="ssd_project",
    )(x, shift, scale, w_all)


def _split3(v):
    p1 = v.astype(bf16)
    r1 = v - p1.astype(f32)
    p2 = r1.astype(bf16)
    p3 = (r1 - p2.astype(f32)).astype(bf16)
    return p1, p2, p3


def _dot3(parts, m, left):
    out = None
    for p in parts:
        t = jnp.dot(m, p, preferred_element_type=f32) if left else jnp.dot(p, m, preferred_element_type=f32)
        out = t if out is None else out + t
    return out


def _ssd_scan_kernel(xbc_ref, dt_ref, z_ref, cst_ref, h0_ref, cw_ref, cb_ref, dtb_ref, a_ref, d_ref, ng_ref, ex_ref,
                     y_ref, ht_ref, win_sc, st_sc, *, n_valid):
    c = pl.program_id(1)
    L = SSM_CHUNK

    @pl.when(c == 0)
    def _():
        win_sc[0:CONV_KEEP, :] = cst_ref[0]
        st_sc[...] = h0_ref[0]

    win_sc[CONV_KEEP:CONV_KEEP + L, :] = xbc_ref[0]
    acc = cb_ref[...] + win_sc[pl.ds(CONV_KEEP, L), :] * cw_ref[CONV_W - 1:CONV_W, :]
    for k in range(CONV_W - 1):
        acc = acc + win_sc[pl.ds(CONV_KEEP - (CONV_W - 1) + k, L), :] * cw_ref[k:k + 1, :]
    win_sc[0:CONV_KEEP, :] = win_sc[L:L + CONV_KEEP, :]
    xbc = acc * jax.nn.sigmoid(acc)
    xs = xbc[:, 0:D_INNER]

    t_row = lax.broadcasted_iota(jnp.int32, (L, 1), 0)
    dt_in = dt_ref[0] + dtb_ref[...]
    dt = jnp.maximum(dt_in, 0.0) + jnp.log1p(jnp.exp(-jnp.abs(dt_in)))
    dt = jnp.where(t_row < n_valid, dt, 0.0)
    a = dt * a_ref[...]
    ii = lax.broadcasted_iota(jnp.int32, (L, L), 0)
    jj = lax.broadcasted_iota(jnp.int32, (L, L), 1)
    lower = ii >= jj
    tri = jnp.where(lower, 1.0, 0.0).astype(bf16)
    a_cs = _dot3(_split3(a), tri, left=True)
    a_cs_t = a_cs.T
    ex = ex_ref[...]
    dt_e = _dot3(_split3(dt), ex, left=False)
    acs_e = _dot3(_split3(a_cs), ex, left=False)
    a_tot_e = acs_e[L - 1:L, :]
    xd = xs * dt_e
    xdd = (xd * jnp.exp(a_tot_e - acs_e)).astype(bf16)
    xdb = xd.astype(bf16)
    grow = jnp.exp(acs_e)
    lane_lo = lax.broadcasted_iota(jnp.int32, (L, 2 * SSM_HEAD_DIM), 1) < SSM_HEAD_DIM
    nt = (((1,), (1,)), ((), ()))

    for g in range(SSM_GROUPS):
        gl = slice(g * GW, (g + 1) * GW)
        bm = xbc[:, D_INNER + g * D_STATE:D_INNER + (g + 1) * D_STATE]
        cm = xbc[:, D_INNER + GN + g * D_STATE:D_INNER + GN + (g + 1) * D_STATE].astype(bf16)
        cb = lax.dot_general(cm, bm.astype(bf16), nt, preferred_element_type=f32)
        st_old = st_sc[:, gl]
        y_g = jnp.dot(cm, st_old.astype(bf16), preferred_element_type=f32) * grow[:, gl]
        st_sc[:, gl] = jnp.exp(a_tot_e[:, gl]) * st_old + jnp.dot(bm.T.astype(bf16), xdd[:, gl], preferred_element_type=f32)
        pairs = []
        for k in range(HPG // 2):
            h0 = g * HPG + 2 * k
            ms = []
            for h in (h0, h0 + 1):
                seg = a_cs[:, h:h + 1] - a_cs_t[h:h + 1, :]
                ms.append((cb * jnp.where(lower, jnp.exp(seg), 0.0)).astype(bf16))
            xp = xdb[:, h0 * SSM_HEAD_DIM:(h0 + 2) * SSM_HEAD_DIM]
            pairs.append(jnp.where(lane_lo, jnp.dot(ms[0], xp, preferred_element_type=f32),
                                   jnp.dot(ms[1], xp, preferred_element_type=f32)))
        y_g = y_g + jnp.concatenate(pairs, axis=1) + d_ref[:, gl] * xs[:, gl]
        zg = z_ref[0, :, gl]
        y_g = y_g * (zg * jax.nn.sigmoid(zg))
        y_g = y_g * lax.rsqrt(jnp.mean(y_g * y_g, axis=-1, keepdims=True) + LN_EPS)
        y_ref[0, :, gl] = (y_g * ng_ref[:, gl]).astype(bf16)

    @pl.when(c == pl.num_programs(1) - 1)
    def _():
        ht_ref[0] = st_sc[...]


def ssd_scan(xbc, dt_raw, z, conv_state, h0, conv_w, conv_b, dt_bias, a_log, d_skip, norm_g, n_valid):
    b_, t_, _ = xbc.shape
    L = SSM_CHUNK
    cst = jnp.pad(conv_state, ((0, 0), (CONV_KEEP - (CONV_W - 1), 0), (0, 0)))
    h0t = h0.transpose(0, 3, 1, 2).reshape(b_, D_STATE, D_INNER)
    vec = lambda v: jnp.pad(v, (0, DT_PAD - SSM_HEADS)).reshape(1, DT_PAD)
    ex = (jnp.arange(DT_PAD)[:, None] == (jnp.arange(D_INNER) // SSM_HEAD_DIM)[None, :]).astype(bf16)
    tile = lambda n: pl.BlockSpec((1, L, n), lambda b, c: (b, c, 0))
    per_b = lambda r, n: pl.BlockSpec((1, r, n), lambda b, c: (b, 0, 0))
    const = lambda r, n: pl.BlockSpec((r, n), lambda b, c: (0, 0))
    y, ht = pl.pallas_call(
        functools.partial(_ssd_scan_kernel, n_valid=n_valid),
        grid=(b_, t_ // L),
        in_specs=[tile(CONV_DIM), tile(DT_PAD), tile(D_INNER), per_b(CONV_KEEP, CONV_DIM), per_b(D_STATE, D_INNER),
                  const(CONV_W, CONV_DIM), const(1, CONV_DIM), const(1, DT_PAD), const(1, DT_PAD), const(1, D_INNER),
                  const(1, D_INNER), const(DT_PAD, D_INNER)],
        out_specs=[tile(D_INNER), per_b(D_STATE, D_INNER)],
        out_shape=[jax.ShapeDtypeStruct((b_, t_, D_INNER), bf16), jax.ShapeDtypeStruct((b_, D_STATE, D_INNER), f32)],
        scratch_shapes=[pltpu.VMEM((L + CONV_KEEP, CONV_DIM), f32), pltpu.VMEM((D_STATE, D_INNER), f32)],
        compiler_params=pltpu.CompilerParams(dimension_semantics=("arbitrary", "arbitrary"),
                                             vmem_limit_bytes=V7X_VMEM_LIMIT_BYTES),
        name="ssd_scan",
    )(xbc, dt_raw, z, cst, h0t, conv_w, conv_b.reshape(1, CONV_DIM), vec(dt_bias), vec(-jnp.exp(a_log)),
      jnp.repeat(d_skip, SSM_HEAD_DIM).reshape(1, D_INNER), norm_g.reshape(1, D_INNER), ex)
    return y, ht.reshape(b_, D_STATE, SSM_HEADS, SSM_HEAD_DIM).transpose(0, 2, 3, 1)


def ssd_layer(x, shift, scale, gate, conv_state, ssm_state, w_in, conv_w, conv_b, dt_bias, a_log, d_skip, norm_g, w_out,
              ln_g, ln_b):
    b_, t_, d_ = x.shape
    fold = t_ < SSM_CHUNK
    if fold:
        flat = lambda m: jnp.broadcast_to(m, (b_, t_, d_)).reshape(1, b_ * t_, d_)
        xf, shift, scale, gate = x.reshape(1, b_ * t_, d_), flat(shift), flat(scale), flat(gate)
    else:
        xf = x
    z, xbc, dt_raw = ssd_project(xf, shift, scale, w_in)
    if fold:
        padt = lambda v: jnp.pad(v.reshape(b_, t_, -1), ((0, 0), (0, SSM_CHUNK - t_), (0, 0)))
        z, xbc, dt_raw = padt(z), padt(xbc), padt(dt_raw)
    y, new_ssm = ssd_scan(xbc, dt_raw, z, conv_state, ssm_state, conv_w, conv_b, dt_bias, a_log, d_skip, norm_g, t_)
    new_conv = jnp.concatenate([conv_state, xbc[:, :t_]], axis=1)[:, t_:]
    if fold:
        y = y[:, :t_].reshape(1, b_ * t_, D_INNER)
    x_new = mixer_out(y, xf, gate, w_out.astype(bf16), ln_g, ln_b).reshape(b_, t_, d_)
    return x_new, new_conv, new_ssm


ADA_COLS = 1152


def _adaln_kernel(c_ref, w_ref, b_ref, o_ref):
    c = c_ref[...]
    h = (c * jax.nn.sigmoid(c)).astype(bf16)
    o_ref[...] = jnp.dot(h, w_ref[...].astype(bf16), preferred_element_type=f32) + b_ref[...]


def adaln(c, w, b):
    r_, d_ = c.shape
    n_ = w.shape[1]
    return pl.pallas_call(
        _adaln_kernel,
        grid=(n_ // ADA_COLS,),
        in_specs=[pl.BlockSpec((r_, d_), lambda n: (0, 0)), pl.BlockSpec((d_, ADA_COLS), lambda n: (0, n)),
                  pl.BlockSpec((1, ADA_COLS), lambda n: (0, n))],
        out_specs=pl.BlockSpec((r_, ADA_COLS), lambda n: (0, n)),
        out_shape=jax.ShapeDtypeStruct((r_, n_), f32),
        compiler_params=pltpu.CompilerParams(dimension_semantics=("arbitrary",)),
        name="adaln",
    )(c, w, b.reshape(1, n_))


def _trunk_layer(x, m, i, mixer, ln_g, ln_b, ffn_a, ffn_b):
    b_, t_, d_ = x.shape
    m = m.reshape(b_, 3, 3, 1, D_MODEL)

    def ffn(v, s, w):
        mods = [m[:, s, k] for k in range(3)]
        if t_ < 8:
            mods = [jnp.broadcast_to(z, (b_, t_, d_)).reshape(1, b_ * t_, d_) for z in mods]
            return ffn_sublayer(v.reshape(1, b_ * t_, d_), *mods, *w, ln_g[i, s], ln_b[i, s]).reshape(b_, t_, d_)
        return ffn_sublayer(v, *mods, *w, ln_g[i, s], ln_b[i, s])

    x = ffn(x, 0, ffn_a)
    x, st = mixer(x, m[:, 1, 0], m[:, 1, 1], m[:, 1, 2], ln_g[i, 1], ln_b[i, 1])
    x = ffn(x, 2, ffn_b)
    return x, st


def kernel(x_prompt, x_sample, cache_kv, cache_win, state_conv, state_ssm, page_table, c_prompt, c_sample, ada_w, ada_b, ln_g, ln_b, ffn_w1, ffn_w3, ffn_w2, nsa_w_in, nsa_w_o, nsa_cmp_pe, nsa_cmp_w1, nsa_cmp_b1, nsa_cmp_w2, nsa_cmp_b2, ssm_w_in, ssm_conv_w, ssm_conv_b, ssm_dt_bias, ssm_a_log, ssm_d, ssm_norm_g, ssm_w_out):
    xp, xs = x_prompt, x_sample
    nb_p, nb_s = c_prompt.shape[0], c_sample.shape[0]
    c_all = jnp.concatenate([c_prompt, c_sample], axis=0)
    c_all = jnp.pad(c_all, ((0, -(nb_p + nb_s) % 8), (0, 0)))
    kv_p, kv_s, win_p, win_s, conv_p, conv_s, ssm_p, ssm_s = [], [], [], [], [], [], [], []
    for i in range(DEPTH):
        j = i // N_MIXERS
        ffn_a = _chunk_ffn_weights(ffn_w1[i, 0], ffn_w3[i, 0], ffn_w2[i, 0])
        ffn_b = _chunk_ffn_weights(ffn_w1[i, 1], ffn_w3[i, 1], ffn_w2[i, 1])
        m_all = adaln(c_all, ada_w[i], ada_b[i])
        m_p, m_s = m_all[:nb_p], m_all[nb_p:nb_p + nb_s]
        lw = (ln_g, ln_b, ffn_a, ffn_b)
        if i % N_MIXERS == 0:
            nw = (nsa_w_in[j], nsa_w_o[j], nsa_cmp_pe[j], nsa_cmp_w1[j], nsa_cmp_b1[j], nsa_cmp_w2[j], nsa_cmp_b2[j])
            def prompt_mixer(x, sh, sc, gt, lg, lb):
                x_new, rows, win = nsa_prompt_layer(x, sh, sc, gt, *nw, lg, lb)
                return x_new, (rows, win[:, x.shape[1] - min(WINDOW, x.shape[1]):])

            xp, (r_p, w_p) = _trunk_layer(xp, m_p, i, prompt_mixer, *lw)
            def sample_mixer(x, sh, sc, gt, lg, lb):
                x_new, rows, win = nsa_sample_layer(x, sh, sc, gt, cache_kv, page_table, j, cache_win[:, j], *nw, lg, lb)
                return x_new, (rows, win)

            xs, (r_s, w_s) = _trunk_layer(xs, m_s, i, sample_mixer, *lw)
            kv_p.append(r_p); kv_s.append(r_s); win_p.append(w_p); win_s.append(w_s)
        else:
            sw = (ssm_w_in[j], ssm_conv_w[j], ssm_conv_b[j], ssm_dt_bias[j], ssm_a_log[j], ssm_d[j], ssm_norm_g[j], ssm_w_out[j])
            zc = jnp.zeros((xp.shape[0], CONV_W - 1, CONV_DIM), f32)
            zs = jnp.zeros((xp.shape[0], SSM_HEADS, SSM_HEAD_DIM, D_STATE), f32)
            def ssd_mixer(conv0, ssm0):
                def run(x, sh, sc, gt, lg, lb):
                    x_new, cv, st = ssd_layer(x, sh, sc, gt, conv0, ssm0, *sw, lg, lb)
                    return x_new, (cv, st)
                return run

            xp, (cv_p, st_p) = _trunk_layer(xp, m_p, i, ssd_mixer(zc, zs), *lw)
            xs, (cv_s, st_s) = _trunk_layer(xs, m_s, i, ssd_mixer(state_conv[:, j], state_ssm[:, j]), *lw)
            conv_p.append(cv_p); conv_s.append(cv_s); ssm_p.append(st_p); ssm_s.append(st_s)
    return (xp, xs, jnp.stack(kv_p, 1), jnp.stack(kv_s, 1), jnp.stack(win_p, 1), jnp.stack(win_s, 1),
            jnp.stack(conv_p, 1), jnp.stack(conv_s, 1), jnp.stack(ssm_p, 1), jnp.stack(ssm_s, 1))
```

```python
import functools
import math

import jax
import jax.numpy as jnp
from jax import lax
from jax.experimental import pallas as pl
from jax.experimental.pallas import tpu as pltpu

f32 = jnp.float32
bf16 = jnp.bfloat16

D_MODEL = 1024
DEPTH = 4
PAGE_SIZE = 128
N_HEADS = 16
HEAD_DIM = 64
N_KV = 4
Q_PER_KV = N_HEADS // N_KV
CMP_BLK = 32
CMP_STRIDE = 16
CMP_HIDDEN = 2 * HEAD_DIM
SEL_BLK = 64
TOPK = 16
WINDOW = 512
Q_BLOCK = 256
ROPE_THETA = 10000.0
D_INNER = 2 * D_MODEL
SSM_HEAD_DIM = 64
SSM_HEADS = D_INNER // SSM_HEAD_DIM
SSM_GROUPS = 4
D_STATE = 128
CONV_W = 4
CONV_DIM = D_INNER + 2 * SSM_GROUPS * D_STATE
SSM_CHUNK = 128
D_FF = 256 * ((8 * D_MODEL // 3 + 255) // 256)
N_MIXERS = 2
ALPHA = (2 * DEPTH) ** 0.25
N_ADA = 9
LN_EPS = 1e-5

V7X_VMEM_LIMIT_BYTES = 56 * 1024 * 1024
FF_CHUNK = 256
FFN_ROWS = 512


def _ffn_kernel(x_ref, sh_ref, sc_ref, gt_ref, w1_ref, w3_ref, w2_ref, lg_ref, lb_ref, o_ref, acc_ref):
    x = x_ref[0]
    hb = (x * (1.0 + sc_ref[0]) + sh_ref[0]).astype(bf16)
    acc_ref[...] = jnp.zeros_like(acc_ref)

    def chunk(c, carry):
        a = jnp.dot(hb, w1_ref[c], preferred_element_type=f32)
        b = jnp.dot(hb, w3_ref[c], preferred_element_type=f32)
        g = (a * jax.nn.sigmoid(a) * b).astype(bf16)
        acc_ref[...] += jnp.dot(g, w2_ref[c], preferred_element_type=f32)
        return carry

    for c in range(w1_ref.shape[0]):
        chunk(c, 0)
    y = ALPHA * x + (1.0 + gt_ref[0]) * (0.5 * acc_ref[...])
    mu = jnp.mean(y, axis=-1, keepdims=True)
    yc = y - mu
    var = jnp.mean(yc * yc, axis=-1, keepdims=True)
    o_ref[0] = yc * lax.rsqrt(var + LN_EPS) * lg_ref[...] + lb_ref[...]


def _mod_spec(m, rows):
    if m.shape[1] == 1:
        return pl.BlockSpec((1, 1, m.shape[2]), lambda b, t: (b, 0, 0))
    return pl.BlockSpec((1, rows, m.shape[2]), lambda b, t: (b, t, 0))


def ffn_sublayer(x, shift, scale, gate, w1c, w3c, w2c, ln_g, ln_b):
    b_, t_, d_ = x.shape
    rows = min(FFN_ROWS, t_)
    mod_spec = _mod_spec(shift, rows)
    const3 = lambda b, t: (0, 0, 0)
    wspec = lambda w: pl.BlockSpec(w.shape, const3, pipeline_mode=pl.Buffered(1))
    vec = pl.BlockSpec((1, d_), lambda b, t: (0, 0))
    return pl.pallas_call(
        _ffn_kernel,
        grid=(b_, t_ // rows),
        in_specs=[pl.BlockSpec((1, rows, d_), lambda b, t: (b, t, 0)), mod_spec, mod_spec, mod_spec,
                  wspec(w1c), wspec(w3c), wspec(w2c), vec, vec],
        out_specs=pl.BlockSpec((1, rows, d_), lambda b, t: (b, t, 0)),
        out_shape=jax.ShapeDtypeStruct(x.shape, f32),
        scratch_shapes=[pltpu.VMEM((rows, d_), f32)],
        compiler_params=pltpu.CompilerParams(dimension_semantics=("arbitrary", "arbitrary"),
                                             vmem_limit_bytes=V7X_VMEM_LIMIT_BYTES),
        name="ffn_sublayer",
    )(x, shift, scale, gate, w1c, w3c, w2c, ln_g.reshape(1, d_), ln_b.reshape(1, d_))


def _chunk_ffn_weights(w1, w3, w2):
    n = D_FF // FF_CHUNK
    w1c = w1.astype(bf16).reshape(D_MODEL, n, FF_CHUNK).transpose(1, 0, 2)
    w3c = w3.astype(bf16).reshape(D_MODEL, n, FF_CHUNK).transpose(1, 0, 2)
    w2c = w2.astype(bf16).reshape(n, FF_CHUNK, D_MODEL)
    return w1c, w3c, w2c


NEG = -1e30
PROJ_ROWS = 512
SEL_TK = 512
KD = N_KV * HEAD_DIM
_C_Q, _C_QR, _C_CMP, _C_SEL, _C_WIN, _C_G, _C_END = 0, 1024, 2048, 2560, 3328, 4096, 4224
N_GATE = 3 * N_HEADS


def _rot_cols(w):
    w4 = w.reshape(w.shape[0], -1, 2, HEAD_DIM // 2)
    return jnp.stack([-w4[:, :, 1], w4[:, :, 0]], axis=2).reshape(w.shape)


def _nsa_proj_weights(w_in):
    qd = N_HEADS * HEAD_DIM
    wq = w_in[:, :qd] * HEAD_DIM ** -0.5
    kc, vc, ks, vs, kw, vw = [w_in[:, qd + i * KD: qd + (i + 1) * KD] for i in range(6)]
    wg = jnp.pad(w_in[:, qd + 6 * KD:], ((0, 0), (0, 128 - N_GATE)))
    return jnp.concatenate([wq, _rot_cols(wq), kc, vc, ks, _rot_cols(ks), vs, kw, _rot_cols(kw), vw, wg], axis=1).astype(bf16)


def _rope_tables(pos, width):
    half = HEAD_DIM // 2
    inv = ROPE_THETA ** (-jnp.arange(half, dtype=f32) / half)
    ang = jnp.tile(pos.astype(f32)[:, None] * inv, (1, width // half))
    return jnp.cos(ang), jnp.sin(ang)


def _nsa_proj_kernel(x_ref, sh_ref, sc_ref, w_ref, cos_ref, sin_ref, q_ref, rows_ref, win_ref, gate_ref, kvb_ref):
    hb = (x_ref[0] * (1.0 + sc_ref[0]) + sh_ref[0]).astype(bf16)
    cos, sin = cos_ref[...], sin_ref[...]

    def mm(lo, hi):
        return jnp.dot(hb, w_ref[:, lo:hi], preferred_element_type=f32)

    def rope(a, b, c):
        return a[:, c * 128:(c + 1) * 128] * cos + b[:, c * 128:(c + 1) * 128] * sin

    qa, qb = mm(_C_Q, _C_QR), mm(_C_QR, _C_CMP)
    for c in range(N_HEADS * HEAD_DIM // 128):
        q_ref[0, :, c * 128:(c + 1) * 128] = rope(qa, qb, c).astype(bf16)
    rows_ref[0, :, 0:2 * KD] = mm(_C_CMP, _C_SEL)
    sel = mm(_C_SEL, _C_WIN)
    win = mm(_C_WIN, _C_G)
    for c in range(KD // 128):
        ks = rope(sel[:, 0:KD], sel[:, KD:2 * KD], c)
        kw = rope(win[:, 0:KD], win[:, KD:2 * KD], c)
        rows_ref[0, :, 2 * KD + c * 128:2 * KD + (c + 1) * 128] = ks
        win_ref[0, :, c * 128:(c + 1) * 128] = kw
        kvb_ref[0, :, c * 128:(c + 1) * 128] = ks.astype(bf16)
        kvb_ref[0, :, 2 * KD + c * 128:2 * KD + (c + 1) * 128] = kw.astype(bf16)
    rows_ref[0, :, 3 * KD:4 * KD] = sel[:, 2 * KD:3 * KD]
    win_ref[0, :, KD:2 * KD] = win[:, 2 * KD:3 * KD]
    kvb_ref[0, :, KD:2 * KD] = sel[:, 2 * KD:3 * KD].astype(bf16)
    kvb_ref[0, :, 3 * KD:4 * KD] = win[:, 2 * KD:3 * KD].astype(bf16)
    gate_ref[0] = jax.nn.sigmoid(mm(_C_G, _C_END))


def nsa_project(x, shift, scale, w_all, cos, sin):
    b_, t_, d_ = x.shape
    rows = min(PROJ_ROWS, t_)
    tile = lambda n: pl.BlockSpec((1, rows, n), lambda b, t: (b, t, 0))
    mod = _mod_spec(shift, rows)
    tab = pl.BlockSpec((rows, 128), lambda b, t: (t, 0))
    return pl.pallas_call(
        _nsa_proj_kernel,
        grid=(b_, t_ // rows),
        in_specs=[tile(d_), mod, mod, pl.BlockSpec(w_all.shape, lambda b, t: (0, 0), pipeline_mode=pl.Buffered(1)), tab, tab],
        out_specs=[tile(4 * KD), tile(4 * KD), tile(2 * KD), tile(128), tile(4 * KD)],
        out_shape=[jax.ShapeDtypeStruct((b_, t_, 4 * KD), bf16), jax.ShapeDtypeStruct((b_, t_, 4 * KD), f32),
                   jax.ShapeDtypeStruct((b_, t_, 2 * KD), f32), jax.ShapeDtypeStruct((b_, t_, 128), f32),
                   jax.ShapeDtypeStruct((b_, t_, 4 * KD), bf16)],
        compiler_params=pltpu.CompilerParams(dimension_semantics=("arbitrary", "arbitrary"),
                                             vmem_limit_bytes=V7X_VMEM_LIMIT_BYTES),
        name="nsa_project",
    )(x, shift, scale, w_all, cos, sin)


def _gelu_tanh(x):
    return 0.5 * x * (1.0 + jnp.tanh(math.sqrt(2.0 / math.pi) * (x + 0.044715 * (x * x * x))))


def _compress_kernel(xa_ref, xb_ref, pe_ref, w1_ref, b1_ref, w2_ref, w2r_ref, b2_ref, b2r_ref, w2t_ref, w2rt_ref,
                     b2t_ref, b2rt_ref, cos_ref, sin_ref, cost_ref, sint_ref, on_ref, ot_ref, *, projected):
    w1 = w1_ref[0]
    half = w1.shape[0] // 2
    bias = jnp.dot(pe_ref[0], w1, preferred_element_type=f32)[0:1] + b1_ref[0]
    nt = (((1,), (1,)), ((), ()))
    for g in range(xa_ref.shape[2]):
        if projected:
            pre = xa_ref[0, 0, g] + xb_ref[0, 0, g] + bias
        else:
            pre = (jnp.dot(xa_ref[0, 0, g], w1[:half], preferred_element_type=f32)
                   + jnp.dot(xb_ref[0, 0, g], w1[half:], preferred_element_type=f32) + bias)
        hb = _gelu_tanh(pre).astype(bf16)
        y = jnp.dot(hb, w2_ref[0], preferred_element_type=f32) + b2_ref[0]
        yr = jnp.dot(hb, w2r_ref[0], preferred_element_type=f32) + b2r_ref[0]
        on_ref[0, 0, g] = y * cos_ref[0] + yr * sin_ref[0]
        yt = lax.dot_general(w2t_ref[0], hb, nt, preferred_element_type=f32) + b2t_ref[0]
        yrt = lax.dot_general(w2rt_ref[0], hb, nt, preferred_element_type=f32) + b2rt_ref[0]
        ot_ref[0, 0, g] = yt * cost_ref[0] + yrt * sint_ref[0]


def compress_kv(chunks, pe, w1, b1, w2, b2, projected=False):
    b_, _, g_, n_ch, feat = chunks.shape
    if projected:
        feat = CMP_HIDDEN
        first, second = chunks[..., :feat], chunks[..., feat:]
    else:
        first = second = chunks
    chunks = first
    nxt = jnp.concatenate([second[:, :, :, 1:], jnp.zeros_like(second[:, :, :, :1])], axis=3)
    cend = jnp.arange(n_ch) * CMP_STRIDE + CMP_BLK - 1
    cos, sin = _rope_tables(cend, HEAD_DIM)
    cos = jnp.stack([cos, jnp.ones_like(cos)])
    sin = jnp.stack([sin, jnp.zeros_like(sin)])
    w2b = w2.astype(bf16)
    w2r = _rot_cols(w2).astype(bf16)
    b2r = _rot_cols(b2[:, None, :])
    pe8 = jnp.broadcast_to(pe.reshape(2, 1, -1), (2, 8, pe.shape[1] * pe.shape[2])).astype(bf16)
    xspec = pl.BlockSpec((1, 1, g_, n_ch, feat), lambda b, k: (b, k, 0, 0, 0))
    per_kv = lambda *s: pl.BlockSpec((1,) + s, lambda b, k: (k,) + (0,) * len(s))
    return pl.pallas_call(
        functools.partial(_compress_kernel, projected=projected),
        grid=(b_, 2),
        in_specs=[xspec, xspec, per_kv(8, pe8.shape[2]), per_kv(*w1.shape[1:]), per_kv(1, CMP_HIDDEN),
                  per_kv(CMP_HIDDEN, HEAD_DIM), per_kv(CMP_HIDDEN, HEAD_DIM), per_kv(1, HEAD_DIM), per_kv(1, HEAD_DIM),
                  per_kv(HEAD_DIM, CMP_HIDDEN), per_kv(HEAD_DIM, CMP_HIDDEN), per_kv(HEAD_DIM, 1), per_kv(HEAD_DIM, 1),
                  per_kv(n_ch, HEAD_DIM), per_kv(n_ch, HEAD_DIM), per_kv(HEAD_DIM, n_ch), per_kv(HEAD_DIM, n_ch)],
        out_specs=[pl.BlockSpec((1, 1, g_, n_ch, HEAD_DIM), lambda b, k: (b, k, 0, 0, 0)),
                   pl.BlockSpec((1, 1, g_, HEAD_DIM, n_ch), lambda b, k: (b, k, 0, 0, 0))],
        out_shape=[jax.ShapeDtypeStruct((b_, 2, g_, n_ch, HEAD_DIM), f32),
                   jax.ShapeDtypeStruct((b_, 2, g_, HEAD_DIM, n_ch), f32)],
        compiler_params=pltpu.CompilerParams(dimension_semantics=("arbitrary",) * 2,
                                             vmem_limit_bytes=V7X_VMEM_LIMIT_BYTES),
        name="compress_kv",
    )(chunks, nxt, pe8, w1.astype(bf16), b1[:, None, :], w2b, w2r, b2[:, None, :], b2r,
      w2b.transpose(0, 2, 1), w2r.transpose(0, 2, 1), b2[:, :, None], b2r.transpose(0, 2, 1),
      cos, sin, cos.transpose(0, 2, 1), sin.transpose(0, 2, 1))


def _cmp_to_sel_matrix(n_c, n_sel):
    cs = jnp.arange(n_c)[:, None] * CMP_STRIDE
    ss = jnp.arange(n_sel)[None, :] * SEL_BLK
    ov = jnp.minimum(cs + CMP_BLK, ss + SEL_BLK) - jnp.maximum(cs, ss)
    return jnp.clip(ov, 0, None).astype(f32) / CMP_BLK


def _lanes(col, n):
    if n % 128:
        return jnp.broadcast_to(col, (col.shape[0], n))
    tile = jnp.broadcast_to(col, (col.shape[0], 128))
    return tile if n == 128 else jnp.concatenate([tile] * (n // 128), axis=1)


def _softmax_rows(s, valid):
    n = s.shape[1]
    e = jnp.where(valid, jnp.exp(s - _lanes(jnp.max(s, axis=1, keepdims=True), n)), 0.0)
    l = jnp.sum(e, axis=1, keepdims=True)
    return e * _lanes(1.0 / jnp.maximum(l, 1e-30), n)


def _unselected_blocks(imp_t, qpos, n_sel):
    jj = lax.broadcasted_iota(jnp.int32, imp_t.shape, 0)
    cur = qpos // SEL_BLK
    forced = (jj == 0) | (jj == cur) | (jj == cur - 1)
    sc0 = jnp.where((jj <= cur) & (jj < n_sel), jnp.where(forced, -NEG, imp_t), NEG)
    jjf = jj.astype(f32)

    def pick(_, carry):
        sc, nonsel = carry
        m = jnp.max(sc, axis=0, keepdims=True)
        idx = jnp.min(jnp.where(sc == m, jjf, float(imp_t.shape[0])), axis=0, keepdims=True)
        hit = (jjf == idx) & (m > NEG)
        return jnp.where(hit, NEG, sc), jnp.where(hit, 0.0, nonsel)

    return lax.fori_loop(0, min(TOPK, n_sel), pick, (sc0, jnp.ones(imp_t.shape, f32)))[1]


def _nsa_attn_kernel(q_ref, kct_ref, vc_ref, ks_ref, vs_ref, *refs, n_cmp, n_sel):
    n_win = WINDOW // Q_BLOCK + 1
    kw_refs, vw_refs = refs[:n_win], refs[n_win:2 * n_win]
    gate_ref, msel_ref, o_ref, oc_sc, score_sc, qa_sc, m_sc, acc_sc = refs[2 * n_win:]
    qt = pl.program_id(1)
    s0 = qt * Q_BLOCK
    rq = Q_PER_KV * Q_BLOCK
    row_q = s0 + (lax.broadcasted_iota(jnp.int32, (rq, 1), 0) & (Q_BLOCK - 1))

    def cmp_branch(ncol):
        cend = lax.broadcasted_iota(jnp.int32, (1, ncol), 1) * CMP_STRIDE + (CMP_BLK - 1)
        cvalid = cend <= row_q
        for g in range(N_KV):
            s = jnp.dot(q_ref[0, g, 0], kct_ref[0, g, :, 0:ncol], preferred_element_type=f32)
            p = _softmax_rows(jnp.where(cvalid, s, NEG), cvalid)
            oc_sc[g] = jnp.dot(p.astype(bf16), vc_ref[0, g, 0:ncol, :], preferred_element_type=f32)
            ps = p[0:Q_BLOCK]
            for r in range(1, Q_PER_KV):
                ps = ps + p[r * Q_BLOCK:(r + 1) * Q_BLOCK]
            hi = ps.astype(bf16)
            lo = (ps - hi.astype(f32)).astype(bf16)
            imp = (jnp.dot(hi, msel_ref[0:ncol, :], preferred_element_type=f32)
                   + jnp.dot(lo, msel_ref[0:ncol, :], preferred_element_type=f32))
            score_sc[:, g * Q_BLOCK:(g + 1) * Q_BLOCK] = imp.T

    widths = [w for w in range(128, n_cmp + 1, 128)] or [n_cmp]
    n_live = (s0 + Q_BLOCK - CMP_BLK) // CMP_STRIDE + 1
    lax.switch(jnp.clip((n_live + 127) // 128 - 1, 0, len(widths) - 1), [functools.partial(cmp_branch, w) for w in widths])

    qpos = s0 + (lax.broadcasted_iota(jnp.int32, (n_sel, N_KV * Q_BLOCK), 1) & (Q_BLOCK - 1))
    nonsel = _unselected_blocks(score_sc[...], qpos, n_sel)
    for g in range(N_KV):
        ns = nonsel[:, g * Q_BLOCK:(g + 1) * Q_BLOCK].T.astype(bf16)
        qa_sc[g] = jnp.concatenate([jnp.concatenate([ns] * Q_PER_KV, axis=0), q_ref[0, g, 0]], axis=1)

    n_tiles = (s0 + Q_BLOCK + SEL_TK - 1) // SEL_TK
    key_l = lax.broadcasted_iota(jnp.int32, (1, SEL_TK), 1)
    blk_j = lax.broadcasted_iota(jnp.int32, (n_sel, SEL_TK), 0)
    blk_l = lax.broadcasted_iota(jnp.int32, (n_sel, SEL_TK), 1)
    wkey_l = lax.broadcasted_iota(jnp.int32, (1, Q_BLOCK), 1)
    m_sc[...] = jnp.full_like(m_sc, NEG)
    acc_sc[...] = jnp.zeros_like(acc_sc)

    def sel_tile(kt, causal):
        expand = jnp.where(blk_j == (kt * SEL_TK + blk_l) // SEL_BLK, NEG, 0.0).astype(bf16)
        ss = []
        for g in range(N_KV):
            k_aug = jnp.concatenate([expand, ks_ref[0, g, kt]], axis=0)
            s = jnp.dot(qa_sc[g], k_aug, preferred_element_type=f32)
            if causal:
                s = jnp.where(kt * SEL_TK + key_l <= row_q, s, NEG)
            ss.append(s)
        for g in range(N_KV):
            m_old = m_sc[g]
            m_new = jnp.maximum(m_old, jnp.broadcast_to(jnp.max(ss[g], axis=1, keepdims=True), m_old.shape))
            p = jnp.exp(ss[g] - jnp.concatenate([m_new] * (SEL_TK // 128), axis=1)).astype(bf16)
            acc_sc[g] = jnp.exp(m_old - m_new) * acc_sc[g] + jnp.dot(p, vs_ref[0, g, kt], preferred_element_type=f32)
            m_sc[g] = m_new

    def body(kt, carry):
        sel_tile(kt, False)
        return carry

    lax.fori_loop(0, n_tiles - 1, body, 0)
    sel_tile(n_tiles - 1, True)

    for g in range(N_KV):
        q = q_ref[0, g, 0]
        acc = acc_sc[g]
        o_s = acc[:, 0:HEAD_DIM] * (1.0 / jnp.maximum(acc[:, HEAD_DIM:HEAD_DIM + 1], 1e-30))

        s_parts = []
        for i in range(n_win):
            kpos = (qt - (n_win - 1) + i) * Q_BLOCK + wkey_l
            dpos = row_q - kpos
            ok = (kpos >= 0) & (dpos >= 0) & (dpos < WINDOW)
            s_parts.append(jnp.where(ok, jnp.dot(q, kw_refs[i][0, g, 0], preferred_element_type=f32), NEG))
        s_w = jnp.concatenate(s_parts, axis=1)
        e_w = jnp.exp(s_w - _lanes(jnp.max(s_w, axis=1, keepdims=True), s_w.shape[1])).astype(bf16)
        acc_w = jnp.dot(e_w[:, 0:Q_BLOCK], vw_refs[0][0, g, 0], preferred_element_type=f32)
        for i in range(1, n_win):
            acc_w = acc_w + jnp.dot(e_w[:, i * Q_BLOCK:(i + 1) * Q_BLOCK], vw_refs[i][0, g, 0], preferred_element_type=f32)
        o_w = acc_w[:, 0:HEAD_DIM] * (1.0 / jnp.maximum(acc_w[:, HEAD_DIM:HEAD_DIM + 1], 1e-30))

        o_c = oc_sc[g]
        gates = gate_ref[0]
        for r in range(Q_PER_KV):
            c = (g * Q_PER_KV + r) * 3
            rows = slice(r * Q_BLOCK, (r + 1) * Q_BLOCK)
            o = (gates[:, c:c + 1] * o_c[rows] + gates[:, c + 1:c + 2] * o_s[rows] + gates[:, c + 2:c + 3] * o_w[rows])
            o_ref[0, g, 0, rows, :] = o.astype(bf16)


def _with_ones(v):
    pad = jnp.zeros(v.shape[:-1] + (128 - v.shape[-1] - 1,), v.dtype)
    return jnp.concatenate([v, jnp.ones(v.shape[:-1] + (1,), v.dtype), pad], axis=-1)


def nsa_attention(q, kvb, gates, kct, vc):
    b_, t_, _ = q.shape
    n_qt, n_st, n_sel = t_ // Q_BLOCK, t_ // SEL_TK, t_ // SEL_BLK
    n_cmp = kct.shape[-1]
    rq = Q_PER_KV * Q_BLOCK
    q5 = q.reshape(b_, n_qt, Q_BLOCK, N_KV, Q_PER_KV, HEAD_DIM).transpose(0, 3, 1, 4, 2, 5).reshape(b_, N_KV, n_qt, rq, HEAD_DIM)

    def tiles(x, tk):
        return x.reshape(b_, t_ // tk, tk, N_KV, HEAD_DIM).transpose(0, 3, 1, 2, 4)

    ks = tiles(kvb[..., 0:KD], SEL_TK).transpose(0, 1, 2, 4, 3)
    vs = _with_ones(tiles(kvb[..., KD:2 * KD], SEL_TK))
    kw = tiles(kvb[..., 2 * KD:3 * KD], Q_BLOCK).transpose(0, 1, 2, 4, 3)
    vw = _with_ones(tiles(kvb[..., 3 * KD:4 * KD], Q_BLOCK))
    msel = _cmp_to_sel_matrix(n_cmp, n_sel).astype(bf16)
    per_b = lambda x: pl.BlockSpec((1,) + x.shape[1:], lambda b, t: (b,) + (0,) * (x.ndim - 1), pipeline_mode=pl.Buffered(1))
    qspec = pl.BlockSpec((1, N_KV, 1, rq, HEAD_DIM), lambda b, t: (b, 0, t, 0, 0))
    n_win = WINDOW // Q_BLOCK + 1

    def win_specs(x):
        def spec(i):
            return pl.BlockSpec((1, N_KV, 1) + x.shape[3:], lambda b, t: (b, 0, jnp.maximum(t - (n_win - 1) + i, 0), 0, 0))
        return [spec(i) for i in range(n_win)]

    o5 = pl.pallas_call(
        functools.partial(_nsa_attn_kernel, n_cmp=n_cmp, n_sel=n_sel),
        grid=(b_, n_qt),
        in_specs=[qspec, per_b(kct), per_b(vc), per_b(ks), per_b(vs)] + win_specs(kw) + win_specs(vw)
        + [pl.BlockSpec((1, Q_BLOCK, 128), lambda b, t: (b, t, 0)), pl.BlockSpec(msel.shape, lambda b, t: (0, 0))],
        out_specs=qspec,
        out_shape=jax.ShapeDtypeStruct(q5.shape, bf16),
        scratch_shapes=[pltpu.VMEM((N_KV, rq, HEAD_DIM), f32), pltpu.VMEM((n_sel, N_KV * Q_BLOCK), f32),
                        pltpu.VMEM((N_KV, rq, n_sel + HEAD_DIM), bf16), pltpu.VMEM((N_KV, rq, 128), f32), pltpu.VMEM((N_KV, rq, 128), f32)],
        compiler_params=pltpu.CompilerParams(dimension_semantics=("arbitrary", "arbitrary"),
                                             vmem_limit_bytes=V7X_VMEM_LIMIT_BYTES),
        name="nsa_attention",
    )(q5, kct, vc, ks, vs, *([kw] * n_win), *([vw] * n_win), gates, msel)
    return o5.reshape(b_, N_KV, n_qt, Q_PER_KV, Q_BLOCK, HEAD_DIM).transpose(0, 2, 4, 1, 3, 5).reshape(b_, t_, N_HEADS * HEAD_DIM)


def _mixer_out_kernel(y_ref, x_ref, gt_ref, w_ref, lg_ref, lb_ref, o_ref):
    f = jnp.dot(y_ref[0], w_ref[...], preferred_element_type=f32)
    y = ALPHA * x_ref[0] + (1.0 + gt_ref[0]) * f
    mu = jnp.mean(y, axis=-1, keepdims=True)
    yc = y - mu
    var = jnp.mean(yc * yc, axis=-1, keepdims=True)
    o_ref[0] = yc * lax.rsqrt(var + LN_EPS) * lg_ref[...] + lb_ref[...]


def mixer_out(y, x, gate, w, ln_g, ln_b):
    b_, t_, d_ = x.shape
    rows = min(PROJ_ROWS, t_)
    vec = pl.BlockSpec((1, d_), lambda b, t: (0, 0))
    return pl.pallas_call(
        _mixer_out_kernel,
        grid=(b_, t_ // rows),
        in_specs=[pl.BlockSpec((1, rows, y.shape[2]), lambda b, t: (b, t, 0)), pl.BlockSpec((1, rows, d_), lambda b, t: (b, t, 0)),
                  _mod_spec(gate, rows), pl.BlockSpec(w.shape, lambda b, t: (0, 0)), vec, vec],
        out_specs=pl.BlockSpec((1, rows, d_), lambda b, t: (b, t, 0)),
        out_shape=jax.ShapeDtypeStruct(x.shape, f32),
        compiler_params=pltpu.CompilerParams(dimension_semantics=("arbitrary", "arbitrary"),
                                             vmem_limit_bytes=V7X_VMEM_LIMIT_BYTES),
        name="mixer_out",
    )(y, x, gate, w, ln_g.reshape(1, d_), ln_b.reshape(1, d_))


def nsa_prompt_layer(x, shift, scale, gate, w_in, w_o, pe, cw1, cb1, cw2, cb2, ln_g, ln_b):
    b_, t_, _ = x.shape
    cos, sin = _rope_tables(jnp.arange(t_), 128)
    q, rows, win, gates, kvb = nsa_project(x, shift, scale, _nsa_proj_weights(w_in), cos, sin)
    n_ch = t_ // CMP_STRIDE
    chunks = rows[..., 0:2 * KD].astype(bf16).reshape(b_, n_ch, CMP_STRIDE, 2, N_KV, HEAD_DIM)
    chunks = chunks.transpose(0, 3, 4, 1, 2, 5).reshape(b_, 2, N_KV, n_ch, CMP_STRIDE * HEAD_DIM)
    cmp_n, cmp_t = compress_kv(chunks, pe, cw1, cb1, cw2, cb2)
    o = nsa_attention(q, kvb, gates, cmp_t[:, 0].astype(bf16), cmp_n[:, 1].astype(bf16))
    x_new = mixer_out(o, x, gate, w_o.astype(bf16), ln_g, ln_b)
    return x_new, rows.reshape(b_, t_, 4, N_KV, HEAD_DIM), win.reshape(b_, t_, 2, N_KV, HEAD_DIM)


PAGES_PER_STEP = 32
S_PAD = 8
CHUNKS_PER_PAGE = PAGE_SIZE // CMP_STRIDE
WIN_KEYS_PAD = 128


def _page_specs(block, tail):
    def spec(i):
        return pl.BlockSpec(block, lambda b, pg, pt: (pt[b, pg * PAGES_PER_STEP + i],) + tail)
    return [spec(i) for i in range(PAGES_PER_STEP)]


def _cmp_pages_kernel(pt_ref, *refs):
    page_refs = refs[:PAGES_PER_STEP]
    perm_ref, w_ref, o_ref = refs[PAGES_PER_STEP:]
    nt = (((1,), (1,)), ((), ()))
    xp = [lax.dot_general(perm_ref[...], r[0, 0].reshape(2 * KD, PAGE_SIZE).astype(bf16), nt, preferred_element_type=f32)
          for r in page_refs]
    cpp = CHUNKS_PER_PAGE
    xs = [jnp.concatenate([x[s * cpp:(s + 1) * cpp] for x in xp], axis=0) for s in range(CMP_STRIDE)]
    for kv in range(2):
        groups = []
        for g in range(N_KV):
            lo = (kv * N_KV + g) * HEAD_DIM
            groups.append(jnp.concatenate([x[:, lo:lo + HEAD_DIM] for x in xs], axis=1))
        xc = jnp.concatenate(groups, axis=0).astype(bf16)
        o_ref[0, 0, kv] = jnp.dot(xc, w_ref[kv], preferred_element_type=f32)


def cmp_pages(pages, li, page_table, w01):
    b_, n_pages = page_table.shape
    n_pg = n_pages // PAGES_PER_STEP
    rows = PAGES_PER_STEP * N_KV * CHUNKS_PER_PAGE
    pos = jnp.arange(PAGE_SIZE)
    perm = (pos[None, :] == (pos[:, None] % CHUNKS_PER_PAGE) * CMP_STRIDE + pos[:, None] // CHUNKS_PER_PAGE).astype(bf16)
    out = pl.pallas_call(
        _cmp_pages_kernel,
        grid_spec=pltpu.PrefetchScalarGridSpec(
            num_scalar_prefetch=1, grid=(b_, n_pg),
            in_specs=_page_specs((1, 1, 2, N_KV, HEAD_DIM, PAGE_SIZE), (li, 0, 0, 0, 0))
            + [pl.BlockSpec(perm.shape, lambda b, pg, pt: (0, 0)), pl.BlockSpec(w01.shape, lambda b, pg, pt: (0, 0, 0))],
            out_specs=pl.BlockSpec((1, 1, 2, rows, w01.shape[2]), lambda b, pg, pt: (b, pg, 0, 0, 0))),
        out_shape=jax.ShapeDtypeStruct((b_, n_pg, 2, rows, w01.shape[2]), f32),
        compiler_params=pltpu.CompilerParams(dimension_semantics=("arbitrary", "arbitrary"),
                                             vmem_limit_bytes=V7X_VMEM_LIMIT_BYTES),
        name="cmp_pages",
    )(page_table, *([pages] * PAGES_PER_STEP), perm, w01)
    out = out.reshape(b_, n_pg, 2, N_KV, PAGES_PER_STEP, CHUNKS_PER_PAGE, w01.shape[2])
    return out.transpose(0, 2, 3, 1, 4, 5, 6).reshape(b_, 2, N_KV, n_pages * CHUNKS_PER_PAGE, w01.shape[2])


def _sample_cmp_win_kernel(q_ref, kct_ref, vc_ref, kw_ref, vw_ref, msel_ref, oc_ref, ow_ref, imp_ref, *, past, w_buf, n_wk):
    rows = Q_PER_KV * S_PAD
    qpos = past + (lax.broadcasted_iota(jnp.int32, (rows, 1), 0) & (S_PAD - 1))
    cend = lax.broadcasted_iota(jnp.int32, (1, kct_ref.shape[3]), 1) * CMP_STRIDE + (CMP_BLK - 1)
    cvalid = cend <= qpos
    widx = lax.broadcasted_iota(jnp.int32, (1, kw_ref.shape[3]), 1)
    kwpos = past - w_buf + widx
    dpos = qpos - kwpos
    wok = (dpos >= 0) & (dpos < WINDOW) & (kwpos >= 0) & (widx < n_wk)
    for g in range(N_KV):
        q = q_ref[0, g]
        s = jnp.dot(q, kct_ref[0, g], preferred_element_type=f32)
        p = _softmax_rows(jnp.where(cvalid, s, NEG), cvalid)
        oc_ref[0, g] = jnp.dot(p.astype(bf16), vc_ref[0, g], preferred_element_type=f32)
        hi = p.astype(bf16)
        lo = (p - hi.astype(f32)).astype(bf16)
        imp = jnp.dot(hi, msel_ref[...], preferred_element_type=f32) + jnp.dot(lo, msel_ref[...], preferred_element_type=f32)
        tot = imp[0:S_PAD]
        for r in range(1, Q_PER_KV):
            tot = tot + imp[r * S_PAD:(r + 1) * S_PAD]
        imp_ref[0, g] = tot
        sw = jnp.where(wok, jnp.dot(q, kw_ref[0, g], preferred_element_type=f32), NEG)
        ew = jnp.exp(sw - _lanes(jnp.max(sw, axis=1, keepdims=True), sw.shape[1])).astype(bf16)
        acc = jnp.dot(ew, vw_ref[0, g], preferred_element_type=f32)
        ow_ref[0, g] = acc[:, 0:HEAD_DIM] * (1.0 / jnp.maximum(acc[:, HEAD_DIM:HEAD_DIM + 1], 1e-30))


def _topk_kernel(imp_ref, o_ref, *, past, n_sel):
    qpos = past + (lax.broadcasted_iota(jnp.int32, imp_ref.shape, 1) & (S_PAD - 1))
    o_ref[...] = _unselected_blocks(imp_ref[...], qpos, n_sel)


def _sample_sel_kernel(pt_ref, *refs, past, n_new):
    page_refs = refs[:PAGES_PER_STEP]
    q_ref, nonsel_ref, knew_ref, vnew_ref, os_ref, m_sc, l_sc, acc_sc = refs[PAGES_PER_STEP:]
    pg = pl.program_id(1)
    nt = (((1,), (1,)), ((), ()))
    rows = Q_PER_KV * S_PAD

    @pl.when(pg == 0)
    def _():
        m_sc[...] = jnp.full_like(m_sc, NEG)
        l_sc[...] = jnp.zeros_like(l_sc)
        acc_sc[...] = jnp.zeros_like(acc_sc)

    def accumulate(s, pv):
        m_old = m_sc[...]
        m_new = jnp.maximum(m_old, jnp.max(s, axis=1, keepdims=True))
        p = jnp.exp(s - m_new)
        alpha = jnp.exp(m_old - m_new)
        l_sc[...] = alpha * l_sc[...] + jnp.sum(p, axis=1, keepdims=True)
        pb = p.astype(bf16)
        upd = jnp.concatenate([pv(g, pb[g * rows:(g + 1) * rows]) for g in range(N_KV)], axis=0)
        acc_sc[...] = alpha * acc_sc[...] + upd
        m_sc[...] = m_new

    nk = PAGES_PER_STEP * PAGE_SIZE
    kt = [jnp.concatenate([r[0, 0, 0, g] for r in page_refs], axis=1).astype(bf16) for g in range(N_KV)]
    vt = [jnp.concatenate([r[0, 0, 1, g] for r in page_refs], axis=1).astype(bf16) for g in range(N_KV)]
    nonsel = nonsel_ref[0]
    s = jnp.concatenate([jnp.dot(q_ref[0, g], kt[g], preferred_element_type=f32) for g in range(N_KV)], axis=0)
    blk_j = lax.broadcasted_iota(jnp.int32, (nonsel.shape[1], nk), 0)
    key = pg * nk + lax.broadcasted_iota(jnp.int32, (nonsel.shape[1], nk), 1)
    expand = jnp.where(blk_j == key // SEL_BLK, NEG, 0.0).astype(bf16)
    accumulate(s + jnp.dot(nonsel, expand, preferred_element_type=f32),
               lambda g, p: lax.dot_general(p, vt[g], nt, preferred_element_type=f32))

    @pl.when(pg == pl.num_programs(1) - 1)
    def _():
        sn = jnp.concatenate([jnp.dot(q_ref[0, g], knew_ref[0, g], preferred_element_type=f32) for g in range(N_KV)], axis=0)
        t_new = lax.broadcasted_iota(jnp.int32, (1, sn.shape[1]), 1)
        s_row = lax.broadcasted_iota(jnp.int32, (sn.shape[0], 1), 0) & (S_PAD - 1)
        new_blk = past // SEL_BLK
        bias_new = nonsel[:, new_blk:new_blk + 1].astype(f32) * NEG
        accumulate(jnp.where((t_new < n_new) & (t_new <= s_row), sn + bias_new, NEG),
                   lambda g, p: jnp.dot(p, vnew_ref[0, g], preferred_element_type=f32))
        o = acc_sc[...] * (1.0 / jnp.maximum(l_sc[...], 1e-30))
        for g in range(N_KV):
            os_ref[0, g] = o[g * rows:(g + 1) * rows]


def _gated_out_kernel(oc_ref, os_ref, ow_ref, gc_ref, gs_ref, gw_ref, x_ref, gt_ref, w_ref, lg_ref, lb_ref, o_ref):
    o = gc_ref[0] * oc_ref[0] + gs_ref[0] * os_ref[0] + gw_ref[0] * ow_ref[0]
    f = jnp.dot(o.astype(bf16), w_ref[...], preferred_element_type=f32)
    y = ALPHA * x_ref[0] + (1.0 + gt_ref[0]) * f
    mu = jnp.mean(y, axis=-1, keepdims=True)
    yc = y - mu
    var = jnp.mean(yc * yc, axis=-1, keepdims=True)
    o_ref[0] = yc * lax.rsqrt(var + LN_EPS) * lg_ref[...] + lb_ref[...]


def nsa_sample_layer(x, shift, scale, gate, cache_kv, page_table, li, win_buf, w_in, w_o, pe, cw1, cb1, cw2, cb2, ln_g, ln_b):
    b_, s_, d_ = x.shape
    n_pool, n_pages = cache_kv.shape[0], page_table.shape[1]
    past = n_pages * PAGE_SIZE
    w_buf = win_buf.shape[1]
    n_ch = past // CMP_STRIDE
    n_sel = past // SEL_BLK + 1
    assert past % SEL_BLK == 0 and s_ <= S_PAD and n_pages % PAGES_PER_STEP == 0
    assert (past + s_ - CMP_BLK) // CMP_STRIDE + 2 == n_ch
    rows_n = b_ * s_
    rq = Q_PER_KV * S_PAD
    flat = lambda m: jnp.broadcast_to(m, (b_, s_, d_)).reshape(1, rows_n, d_)
    xf = x.reshape(1, rows_n, d_)
    cos, sin = _rope_tables(past + (jnp.arange(rows_n) % s_), 128)
    q, rows, win, gates, kvb = nsa_project(xf, flat(shift), flat(scale), _nsa_proj_weights(w_in), cos, sin)

    pages = cache_kv.transpose(0, 1, 3, 4, 5, 2)
    half = cw1.shape[1] // 2
    w01 = jnp.concatenate([cw1[:, :half], cw1[:, half:]], axis=2).astype(bf16)
    cmp_n, cmp_t = compress_kv(cmp_pages(pages, li, page_table, w01), pe, cw1, cb1, cw2, cb2, projected=True)

    q32 = jnp.pad(q.reshape(b_, s_, N_KV, Q_PER_KV, HEAD_DIM), ((0, 0), (0, S_PAD - s_), (0, 0), (0, 0), (0, 0)))
    q32 = q32.transpose(0, 2, 3, 1, 4).reshape(b_, N_KV, rq, HEAD_DIM)

    w_all = jnp.concatenate([win_buf, win.reshape(b_, s_, 2, N_KV, HEAD_DIM)], axis=1)
    n_wk = w_buf + s_
    wk_pad = -(-n_wk // WIN_KEYS_PAD) * WIN_KEYS_PAD
    w_pad = jnp.pad(w_all, ((0, 0), (0, wk_pad - n_wk), (0, 0), (0, 0), (0, 0))).astype(bf16)
    kw = w_pad[:, :, 0].transpose(0, 2, 3, 1)
    vw = _with_ones(w_pad[:, :, 1].transpose(0, 2, 1, 3))
    n_sel_pad = 2 * 128
    msel = jnp.pad(_cmp_to_sel_matrix(n_ch, n_sel), ((0, 0), (0, n_sel_pad - n_sel))).astype(bf16)
    per_b = lambda a: pl.BlockSpec((1,) + a.shape[1:], lambda b: (b,) + (0,) * (a.ndim - 1))
    kct, vc = cmp_t[:, 0].astype(bf16), cmp_n[:, 1].astype(bf16)
    o_shape = jax.ShapeDtypeStruct((b_, N_KV, rq, HEAD_DIM), f32)
    o_spec = pl.BlockSpec((1, N_KV, rq, HEAD_DIM), lambda b: (b, 0, 0, 0))
    oc, ow, imp = pl.pallas_call(
        functools.partial(_sample_cmp_win_kernel, past=past, w_buf=w_buf, n_wk=n_wk),
        grid=(b_,),
        in_specs=[per_b(q32), per_b(kct), per_b(vc), per_b(kw), per_b(vw), pl.BlockSpec(msel.shape, lambda b: (0, 0))],
        out_specs=[o_spec, o_spec, pl.BlockSpec((1, N_KV, S_PAD, n_sel_pad), lambda b: (b, 0, 0, 0))],
        out_shape=[o_shape, o_shape, jax.ShapeDtypeStruct((b_, N_KV, S_PAD, n_sel_pad), f32)],
        compiler_params=pltpu.CompilerParams(dimension_semantics=("arbitrary",)),
        name="sample_cmp_win",
    )(q32, kct, vc, kw, vw, msel)

    sel_rows = -(-n_sel // 8) * 8
    imp_t = imp[..., :sel_rows].transpose(3, 0, 1, 2).reshape(sel_rows, b_ * N_KV * S_PAD)
    nonsel_t = pl.pallas_call(
        functools.partial(_topk_kernel, past=past, n_sel=n_sel),
        out_shape=jax.ShapeDtypeStruct(imp_t.shape, f32),
        name="sample_topk",
    )(imp_t)
    nonsel = nonsel_t.reshape(sel_rows, b_, N_KV, 1, S_PAD).transpose(1, 2, 3, 4, 0)
    nonsel = jnp.broadcast_to(nonsel, (b_, N_KV, Q_PER_KV, S_PAD, sel_rows)).reshape(b_, N_KV * rq, sel_rows)
    nonsel = jnp.pad(nonsel, ((0, 0), (0, 0), (0, n_sel_pad - sel_rows)), constant_values=1.0).astype(bf16)

    new_pad = lambda a: jnp.pad(a.reshape(b_, s_, N_KV, HEAD_DIM), ((0, 0), (0, 128 - s_), (0, 0), (0, 0)))
    knew = new_pad(kvb[..., 0:KD]).transpose(0, 2, 3, 1)
    vnew = new_pad(kvb[..., KD:2 * KD]).transpose(0, 2, 1, 3)
    rows_all = N_KV * rq
    per_b2 = lambda a: pl.BlockSpec((1,) + a.shape[1:], lambda b, pg, pt: (b,) + (0,) * (a.ndim - 1))
    o_s = pl.pallas_call(
        functools.partial(_sample_sel_kernel, past=past, n_new=s_),
        grid_spec=pltpu.PrefetchScalarGridSpec(
            num_scalar_prefetch=1, grid=(b_, n_pages // PAGES_PER_STEP),
            in_specs=_page_specs((1, 1, 2, N_KV, HEAD_DIM, PAGE_SIZE), (li, 1, 0, 0, 0))
            + [per_b2(q32), per_b2(nonsel), per_b2(knew), per_b2(vnew)],
            out_specs=pl.BlockSpec((1, N_KV, rq, HEAD_DIM), lambda b, pg, pt: (b, 0, 0, 0)),
            scratch_shapes=[pltpu.VMEM((rows_all, 1), f32), pltpu.VMEM((rows_all, 1), f32), pltpu.VMEM((rows_all, HEAD_DIM), f32)]),
        out_shape=o_shape,
        compiler_params=pltpu.CompilerParams(dimension_semantics=("arbitrary", "arbitrary"),
                                             vmem_limit_bytes=V7X_VMEM_LIMIT_BYTES),
        name="sample_sel",
    )(page_table, *([pages] * PAGES_PER_STEP), q32, nonsel, knew, vnew)

    tok = lambda o: o.reshape(b_, N_KV, Q_PER_KV, S_PAD, HEAD_DIM)[:, :, :, :s_].transpose(0, 3, 1, 2, 4).reshape(1, rows_n, N_HEADS * HEAD_DIM)
    gexp = lambda br: jnp.repeat(gates[..., br:N_GATE:3], HEAD_DIM, axis=-1)
    full = lambda n: pl.BlockSpec((1, rows_n, n), lambda i: (0, 0, 0))
    vec = pl.BlockSpec((1, d_), lambda i: (0, 0))
    x_new = pl.pallas_call(
        _gated_out_kernel,
        grid=(1,),
        in_specs=[full(N_HEADS * HEAD_DIM)] * 6 + [full(d_), full(d_), pl.BlockSpec(w_o.shape, lambda i: (0, 0)), vec, vec],
        out_specs=full(d_),
        out_shape=jax.ShapeDtypeStruct((1, rows_n, d_), f32),
        name="sample_gated_out",
    )(tok(oc), tok(o_s), tok(ow), gexp(0), gexp(1), gexp(2), xf, flat(gate), w_o.astype(bf16), ln_g.reshape(1, d_), ln_b.reshape(1, d_))
    return x_new.reshape(b_, s_, d_), rows.reshape(b_, s_, 4, N_KV, HEAD_DIM), w_all[:, n_wk - w_buf:]


HPG = SSM_HEADS // SSM_GROUPS
GN = SSM_GROUPS * D_STATE
GW = D_INNER // SSM_GROUPS
DT_PAD = 128
CONV_KEEP = 8


def _ssd_proj_kernel(x_ref, sh_ref, sc_ref, w_ref, z_ref, xbc_ref, dt_ref):
    hb = (x_ref[0] * (1.0 + sc_ref[0]) + sh_ref[0]).astype(bf16)
    z_ref[0] = jnp.dot(hb, w_ref[:, 0:D_INNER], preferred_element_type=f32)
    xbc_ref[0] = jnp.dot(hb, w_ref[:, D_INNER:D_INNER + CONV_DIM], preferred_element_type=f32)
    dt_ref[0] = jnp.dot(hb, w_ref[:, D_INNER + CONV_DIM:], preferred_element_type=f32)


def ssd_project(x, shift, scale, w_in):
    b_, t_, d_ = x.shape
    rows = min(PROJ_ROWS, t_)
    w_all = jnp.pad(w_in, ((0, 0), (0, DT_PAD - SSM_HEADS))).astype(bf16)
    tile = lambda n: pl.BlockSpec((1, rows, n), lambda b, t: (b, t, 0))
    mod = _mod_spec(shift, rows)
    return pl.pallas_call(
        _ssd_proj_kernel,
        grid=(b_, t_ // rows),
        in_specs=[tile(d_), mod, mod, pl.BlockSpec(w_all.shape, lambda b, t: (0, 0), pipeline_mode=pl.Buffered(1))],
        out_specs=[tile(D_INNER), tile(CONV_DIM), tile(DT_PAD)],
        out_shape=[jax.ShapeDtypeStruct((b_, t_, D_INNER), f32), jax.ShapeDtypeStruct((b_, t_, CONV_DIM), f32),
                   jax.ShapeDtypeStruct((b_, t_, DT_PAD), f32)],
        compiler_params=pltpu.CompilerParams(dimension_semantics=("arbitrary", "arbitrary"),
                                             vmem_limit_bytes=V7X_VMEM_LIMIT_BYTES),
        name="ssd_project",
    )(x, shift, scale, w_all)


def _split3(v):
    p1 = v.astype(bf16)
    r1 = v - p1.astype(f32)
    p2 = r1.astype(bf16)
    p3 = (r1 - p2.astype(f32)).astype(bf16)
    return p1, p2, p3


def _dot3(parts, m, left):
    out = None
    for p in parts:
        t = jnp.dot(m, p, preferred_element_type=f32) if left else jnp.dot(p, m, preferred_element_type=f32)
        out = t if out is None else out + t
    return out


def _ssd_scan_kernel(xbc_ref, dt_ref, z_ref, cst_ref, h0_ref, cw_ref, cb_ref, dtb_ref, a_ref, d_ref, ng_ref, ex_ref,
                     y_ref, ht_ref, win_sc, st_sc, *, n_valid):
    c = pl.program_id(1)
    L = SSM_CHUNK

    @pl.when(c == 0)
    def _():
        win_sc[0:CONV_KEEP, :] = cst_ref[0]
        st_sc[...] = h0_ref[0]

    win_sc[CONV_KEEP:CONV_KEEP + L, :] = xbc_ref[0]
    acc = cb_ref[...] + win_sc[pl.ds(CONV_KEEP, L), :] * cw_ref[CONV_W - 1:CONV_W, :]
    for k in range(CONV_W - 1):
        acc = acc + win_sc[pl.ds(CONV_KEEP - (CONV_W - 1) + k, L), :] * cw_ref[k:k + 1, :]
    win_sc[0:CONV_KEEP, :] = win_sc[L:L + CONV_KEEP, :]
    xbc = acc * jax.nn.sigmoid(acc)
    xs = xbc[:, 0:D_INNER]

    t_row = lax.broadcasted_iota(jnp.int32, (L, 1), 0)
    dt_in = dt_ref[0] + dtb_ref[...]
    dt = jnp.maximum(dt_in, 0.0) + jnp.log1p(jnp.exp(-jnp.abs(dt_in)))
    dt = jnp.where(t_row < n_valid, dt, 0.0)
    a = dt * a_ref[...]
    ii = lax.broadcasted_iota(jnp.int32, (L, L), 0)
    jj = lax.broadcasted_iota(jnp.int32, (L, L), 1)
    lower = ii >= jj
    tri = jnp.where(lower, 1.0, 0.0).astype(bf16)
    a_cs = _dot3(_split3(a), tri, left=True)
    a_cs_t = a_cs.T
    ex = ex_ref[...]
    dt_e = _dot3(_split3(dt), ex, left=False)
    acs_e = _dot3(_split3(a_cs), ex, left=False)
    a_tot_e = acs_e[L - 1:L, :]
    xd = xs * dt_e
    xdd = (xd * jnp.exp(a_tot_e - acs_e)).astype(bf16)
    xdb = xd.astype(bf16)
    grow = jnp.exp(acs_e)
    lane_lo = lax.broadcasted_iota(jnp.int32, (L, 2 * SSM_HEAD_DIM), 1) < SSM_HEAD_DIM
    nt = (((1,), (1,)), ((), ()))

    for g in range(SSM_GROUPS):
        gl = slice(g * GW, (g + 1) * GW)
        bm = xbc[:, D_INNER + g * D_STATE:D_INNER + (g + 1) * D_STATE]
        cm = xbc[:, D_INNER + GN + g * D_STATE:D_INNER + GN + (g + 1) * D_STATE].astype(bf16)
        cb = lax.dot_general(cm, bm.astype(bf16), nt, preferred_element_type=f32)
        st_old = st_sc[:, gl]
        y_g = jnp.dot(cm, st_old.astype(bf16), preferred_element_type=f32) * grow[:, gl]
        st_sc[:, gl] = jnp.exp(a_tot_e[:, gl]) * st_old + jnp.dot(bm.T.astype(bf16), xdd[:, gl], preferred_element_type=f32)
        pairs = []
        for k in range(HPG // 2):
            h0 = g * HPG + 2 * k
            ms = []
            for h in (h0, h0 + 1):
                seg = a_cs[:, h:h + 1] - a_cs_t[h:h + 1, :]
                ms.append((cb * jnp.where(lower, jnp.exp(seg), 0.0)).astype(bf16))
            xp = xdb[:, h0 * SSM_HEAD_DIM:(h0 + 2) * SSM_HEAD_DIM]
            pairs.append(jnp.where(lane_lo, jnp.dot(ms[0], xp, preferred_element_type=f32),
                                   jnp.dot(ms[1], xp, preferred_element_type=f32)))
        y_g = y_g + jnp.concatenate(pairs, axis=1) + d_ref[:, gl] * xs[:, gl]
        zg = z_ref[0, :, gl]
        y_g = y_g * (zg * jax.nn.sigmoid(zg))
        y_g = y_g * lax.rsqrt(jnp.mean(y_g * y_g, axis=-1, keepdims=True) + LN_EPS)
        y_ref[0, :, gl] = (y_g * ng_ref[:, gl]).astype(bf16)

    @pl.when(c == pl.num_programs(1) - 1)
    def _():
        ht_ref[0] = st_sc[...]


def ssd_scan(xbc, dt_raw, z, conv_state, h0, conv_w, conv_b, dt_bias, a_log, d_skip, norm_g, n_valid):
    b_, t_, _ = xbc.shape
    L = SSM_CHUNK
    cst = jnp.pad(conv_state, ((0, 0), (CONV_KEEP - (CONV_W - 1), 0), (0, 0)))
    h0t = h0.transpose(0, 3, 1, 2).reshape(b_, D_STATE, D_INNER)
    vec = lambda v: jnp.pad(v, (0, DT_PAD - SSM_HEADS)).reshape(1, DT_PAD)
    ex = (jnp.arange(DT_PAD)[:, None] == (jnp.arange(D_INNER) // SSM_HEAD_DIM)[None, :]).astype(bf16)
    tile = lambda n: pl.BlockSpec((1, L, n), lambda b, c: (b, c, 0))
    per_b = lambda r, n: pl.BlockSpec((1, r, n), lambda b, c: (b, 0, 0))
    const = lambda r, n: pl.BlockSpec((r, n), lambda b, c: (0, 0))
    y, ht = pl.pallas_call(
        functools.partial(_ssd_scan_kernel, n_valid=n_valid),
        grid=(b_, t_ // L),
        in_specs=[tile(CONV_DIM), tile(DT_PAD), tile(D_INNER), per_b(CONV_KEEP, CONV_DIM), per_b(D_STATE, D_INNER),
                  const(CONV_W, CONV_DIM), const(1, CONV_DIM), const(1, DT_PAD), const(1, DT_PAD), const(1, D_INNER),
                  const(1, D_INNER), const(DT_PAD, D_INNER)],
        out_specs=[tile(D_INNER), per_b(D_STATE, D_INNER)],
        out_shape=[jax.ShapeDtypeStruct((b_, t_, D_INNER), bf16), jax.ShapeDtypeStruct((b_, D_STATE, D_INNER), f32)],
        scratch_shapes=[pltpu.VMEM((L + CONV_KEEP, CONV_DIM), f32), pltpu.VMEM((D_STATE, D_INNER), f32)],
        compiler_params=pltpu.CompilerParams(dimension_semantics=("arbitrary", "arbitrary"),
                                             vmem_limit_bytes=V7X_VMEM_LIMIT_BYTES),
        name="ssd_scan",
    )(xbc, dt_raw, z, cst, h0t, conv_w, conv_b.reshape(1, CONV_DIM), vec(dt_bias), vec(-jnp.exp(a_log)),
      jnp.repeat(d_skip, SSM_HEAD_DIM).reshape(1, D_INNER), norm_g.reshape(1, D_INNER), ex)
    return y, ht.reshape(b_, D_STATE, SSM_HEADS, SSM_HEAD_DIM).transpose(0, 2, 3, 1)


def ssd_layer(x, shift, scale, gate, conv_state, ssm_state, w_in, conv_w, conv_b, dt_bias, a_log, d_skip, norm_g, w_out,
              ln_g, ln_b):
    b_, t_, d_ = x.shape
    fold = t_ < SSM_CHUNK
    if fold:
        flat = lambda m: jnp.broadcast_to(m, (b_, t_, d_)).reshape(1, b_ * t_, d_)
        xf, shift, scale, gate = x.reshape(1, b_ * t_, d_), flat(shift), flat(scale), flat(gate)
    else:
        xf = x
    z, xbc, dt_raw = ssd_project(xf, shift, scale, w_in)
    if fold:
        padt = lambda v: jnp.pad(v.reshape(b_, t_, -1), ((0, 0), (0, SSM_CHUNK - t_), (0, 0)))
        z, xbc, dt_raw = padt(z), padt(xbc), padt(dt_raw)
    y, new_ssm = ssd_scan(xbc, dt_raw, z, conv_state, ssm_state, conv_w, conv_b, dt_bias, a_log, d_skip, norm_g, t_)
    new_conv = jnp.concatenate([conv_state, xbc[:, :t_]], axis=1)[:, t_:]
    if fold:
        y = y[:, :t_].reshape(1, b_ * t_, D_INNER)
    x_new = mixer_out(y, xf, gate, w_out.astype(bf16), ln_g, ln_b).reshape(b_, t_, d_)
    return x_new, new_conv, new_ssm


ADA_COLS = 1152


def _adaln_kernel(c_ref, w_ref, b_ref, o_ref):
    c = c_ref[...]
    h = (c * jax.nn.sigmoid(c)).astype(bf16)
    o_ref[...] = jnp.dot(h, w_ref[...].astype(bf16), preferred_element_type=f32) + b_ref[...]


def adaln(c, w, b):
    r_, d_ = c.shape
    n_ = w.shape[1]
    return pl.pallas_call(
        _adaln_kernel,
        grid=(n_ // ADA_COLS,),
        in_specs=[pl.BlockSpec((r_, d_), lambda n: (0, 0)), pl.BlockSpec((d_, ADA_COLS), lambda n: (0, n)),
                  pl.BlockSpec((1, ADA_COLS), lambda n: (0, n))],
        out_specs=pl.BlockSpec((r_, ADA_COLS), lambda n: (0, n)),
        out_shape=jax.ShapeDtypeStruct((r_, n_), f32),
        compiler_params=pltpu.CompilerParams(dimension_semantics=("arbitrary",)),
        name="adaln",
    )(c, w, b.reshape(1, n_))


def _trunk_layer(x, m, i, mixer, ln_g, ln_b, ffn_a, ffn_b):
    b_, t_, d_ = x.shape
    m = m.reshape(b_, 3, 3, 1, D_MODEL)

    def ffn(v, s, w):
        mods = [m[:, s, k] for k in range(3)]
        if t_ < 8:
            mods = [jnp.broadcast_to(z, (b_, t_, d_)).reshape(1, b_ * t_, d_) for z in mods]
            return ffn_sublayer(v.reshape(1, b_ * t_, d_), *mods, *w, ln_g[i, s], ln_b[i, s]).reshape(b_, t_, d_)
        return ffn_sublayer(v, *mods, *w, ln_g[i, s], ln_b[i, s])

    x = ffn(x, 0, ffn_a)
    x, st = mixer(x, m[:, 1, 0], m[:, 1, 1], m[:, 1, 2], ln_g[i, 1], ln_b[i, 1])
    x = ffn(x, 2, ffn_b)
    return x, st


def kernel(x_prompt, x_sample, cache_kv, cache_win, state_conv, state_ssm, page_table, c_prompt, c_sample, ada_w, ada_b, ln_g, ln_b, ffn_w1, ffn_w3, ffn_w2, nsa_w_in, nsa_w_o, nsa_cmp_pe, nsa_cmp_w1, nsa_cmp_b1, nsa_cmp_w2, nsa_cmp_b2, ssm_w_in, ssm_conv_w, ssm_conv_b, ssm_dt_bias, ssm_a_log, ssm_d, ssm_norm_g, ssm_w_out):
    xp, xs = x_prompt, x_sample
    nb_p, nb_s = c_prompt.shape[0], c_sample.shape[0]
    c_all = jnp.concatenate([c_prompt, c_sample], axis=0)
    c_all = jnp.pad(c_all, ((0, -(nb_p + nb_s) % 8), (0, 0)))
    kv_p, kv_s, win_p, win_s, conv_p, conv_s, ssm_p, ssm_s = [], [], [], [], [], [], [], []
    for i in range(DEPTH):
        j = i // N_MIXERS
        ffn_a = _chunk_ffn_weights(ffn_w1[i, 0], ffn_w3[i, 0], ffn_w2[i, 0])
        ffn_b = _chunk_ffn_weights(ffn_w1[i, 1], ffn_w3[i, 1], ffn_w2[i, 1])
        m_all = adaln(c_all, ada_w[i], ada_b[i])
        m_p, m_s = m_all[:nb_p], m_all[nb_p:nb_p + nb_s]
        lw = (ln_g, ln_b, ffn_a, ffn_b)
        if i % N_MIXERS == 0:
            nw = (nsa_w_in[j], nsa_w_o[j], nsa_cmp_pe[j], nsa_cmp_w1[j], nsa_cmp_b1[j], nsa_cmp_w2[j], nsa_cmp_b2[j])
            def prompt_mixer(x, sh, sc, gt, lg, lb):
                x_new, rows, win = nsa_prompt_layer(x, sh, sc, gt, *nw, lg, lb)
                return x_new, (rows, win[:, x.shape[1] - min(WINDOW, x.shape[1]):])

            xp, (r_p, w_p) = _trunk_layer(xp, m_p, i, prompt_mixer, *lw)
            def sample_mixer(x, sh, sc, gt, lg, lb):
                x_new, rows, win = nsa_sample_layer(x, sh, sc, gt, cache_kv, page_table, j, cache_win[:, j], *nw, lg, lb)
                return x_new, (rows, win)

            xs, (r_s, w_s) = _trunk_layer(xs, m_s, i, sample_mixer, *lw)
            kv_p.append(r_p); kv_s.append(r_s); win_p.append(w_p); win_s.append(w_s)
        else:
            sw = (ssm_w_in[j], ssm_conv_w[j], ssm_conv_b[j], ssm_dt_bias[j], ssm_a_log[j], ssm_d[j], ssm_norm_g[j], ssm_w_out[j])
            zc = jnp.zeros((xp.shape[0], CONV_W - 1, CONV_DIM), f32)
            zs = jnp.zeros((xp.shape[0], SSM_HEADS, SSM_HEAD_DIM, D_STATE), f32)
            def ssd_mixer(conv0, ssm0):
                def run(x, sh, sc, gt, lg, lb):
                    x_new, cv, st = ssd_layer(x, sh, sc, gt, conv0, ssm0, *sw, lg, lb)
                    return x_new, (cv, st)
                return run

            xp, (cv_p, st_p) = _trunk_layer(xp, m_p, i, ssd_mixer(zc, zs), *lw)
            xs, (cv_s, st_s) = _trunk_layer(xs, m_s, i, ssd_mixer(state_conv[:, j], state_ssm[:, j]), *lw)
            conv_p.append(cv_p); conv_s.append(cv_s); ssm_p.append(st_p); ssm_s.append(st_s)
    return (xp, xs, jnp.stack(kv_p, 1), jnp.stack(kv_s, 1), jnp.stack(win_p, 1), jnp.stack(win_s, 1),
            jnp.stack(conv_p, 1), jnp.stack(conv_s, 1), jnp.stack(ssm_p, 1), jnp.stack(ssm_s, 1))
```
